```python
import math
import jax
import jax.numpy as jnp
from jax import lax
import numpy as np

D_MODEL = 1024
BATCH = 16
SEQ = 2048
DEPTH = 4

CHUNK = 64
EPS = 1e-6
N_BRANCH = 3
BRANCH_WIDTH = 512

DN_HEADS = 4
DN_DK = 128
DN_DV = 128
DN_CONV = 4
DN_QK = DN_HEADS * DN_DK
DN_VW = DN_HEADS * DN_DV
DN_QKV = 2 * DN_QK + DN_VW

RET_HEADS = 4
RET_DK = 64
RET_DV = 128
RET_QK = RET_HEADS * RET_DK
RET_VW = RET_HEADS * RET_DV
ROPE_BASE = 10000.0

LRU_WIDTH = 512
LRU_BLOCKS = 4
LRU_BLOCK = LRU_WIDTH // LRU_BLOCKS
LRU_CONV = 4
LRU_C = 8.0

D_FF = 2816
FFN_CONV = 3

IN_SIZES = (DN_QKV, DN_HEADS, DN_HEADS, DN_VW, RET_QK, RET_QK, RET_VW, RET_VW, LRU_WIDTH, LRU_WIDTH, N_BRANCH * D_MODEL)
N_IN = sum(IN_SIZES)

kernel_name = 'hybrid_deltanet_retention_rglru_convffn'


def _rms_norm(x, g):
    xf = x.astype(jnp.float32)
    y = xf * lax.rsqrt(jnp.mean(xf * xf, axis=-1, keepdims=True) + EPS)
    return (y * g.astype(jnp.float32)).astype(x.dtype)


def _l2norm(t):
    return t * lax.rsqrt(jnp.sum(t * t, axis=-1, keepdims=True) + EPS)


def _head_rms(t):
    return t * lax.rsqrt(jnp.mean(t * t, axis=-1, keepdims=True) + EPS)


def _head_groupnorm(t):
    mu = jnp.mean(t, axis=-1, keepdims=True)
    var = jnp.mean(jnp.square(t - mu), axis=-1, keepdims=True)
    return (t - mu) * lax.rsqrt(var + EPS)


def _causal_dwconv(x, w, b=None):
    width = w.shape[0]
    seq = x.shape[1]
    xp = jnp.pad(x, ((0, 0), (width - 1, 0), (0, 0)))
    y = xp[:, 0:seq, :] * w[0]
    for j in range(1, width):
        y = y + xp[:, j:j + seq, :] * w[j]
    return y if b is None else y + b


def _to_chunks(t, n):
    b, _, h, d = t.shape
    return t.reshape(b, n, CHUNK, h, d).transpose(0, 3, 1, 2, 4)


def _from_chunks(t):
    b, h, n, c, d = t.shape
    return t.transpose(0, 2, 3, 1, 4).reshape(b, n * c, h, d)


def _rotary(t, positions):
    half = t.shape[-1] // 2
    inv = ROPE_BASE ** (-jnp.arange(half, dtype=jnp.float32) / half)
    ang = positions.astype(jnp.float32)[:, :, None, None] * inv
    cos, sin = jnp.cos(ang), jnp.sin(ang)
    t1, t2 = t[..., :half], t[..., half:]
    return jnp.concatenate([t1 * cos - t2 * sin, t1 * sin + t2 * cos], axis=-1)


def _gated_delta_rule(q, k, v, g, beta):
    bsz, seq, nh, dk = q.shape
    dv = v.shape[-1]
    n = seq // CHUNK
    q = _l2norm(q) * (dk ** -0.5)
    k = _l2norm(k)
    q, k, v = _to_chunks(q, n), _to_chunks(k, n), _to_chunks(v, n)
    g = _to_chunks(g[..., None], n)[..., 0]
    beta = _to_chunks(beta[..., None], n)[..., 0]
    g = jnp.cumsum(g, axis=-1)
    causal = jnp.tril(jnp.ones((CHUNK, CHUNK), dtype=bool))
    strict = jnp.tril(jnp.ones((CHUNK, CHUNK), dtype=bool), -1)
    diff = g[..., :, None] - g[..., None, :]
    decay = jnp.where(causal, jnp.exp(jnp.where(causal, diff, 0.0)), 0.0)
    k_beta = k * beta[..., None]
    kk = jnp.einsum('bhnck,bhnsk->bhncs', k_beta, k)
    eye = jnp.eye(CHUNK, dtype=jnp.float32)
    lower = eye + jnp.where(strict, kk * decay, 0.0)
    t_inv = lax.linalg.triangular_solve(lower, jnp.broadcast_to(eye, lower.shape), left_side=True, lower=True)
    u = jnp.einsum('bhncs,bhnsv->bhncv', t_inv, v * beta[..., None])
    w = jnp.einsum('bhncs,bhnsk->bhnck', t_inv, k_beta * jnp.exp(g)[..., None])
    attn = jnp.einsum('bhnck,bhnsk->bhncs', q, k) * decay
    q_dec = q * jnp.exp(g)[..., None]
    k_dec = k * jnp.exp(g[..., -1:] - g)[..., None]
    g_end = jnp.exp(g[..., -1])

    def step(state, inp):
        qd, kd, uc, wc, ac, ge = inp
        v_new = uc - jnp.einsum('bhck,bhkv->bhcv', wc, state)
        out = jnp.einsum('bhck,bhkv->bhcv', qd, state) + jnp.einsum('bhcs,bhsv->bhcv', ac, v_new)
        state = state * ge[..., None, None] + jnp.einsum('bhck,bhcv->bhkv', kd, v_new)
        return state, out

    xs = (jnp.moveaxis(q_dec, 2, 0), jnp.moveaxis(k_dec, 2, 0), jnp.moveaxis(u, 2, 0),
          jnp.moveaxis(w, 2, 0), jnp.moveaxis(attn, 2, 0), jnp.moveaxis(g_end, 2, 0))
    s0 = jnp.zeros((bsz, nh, dk, dv), jnp.float32)
    _, out = lax.scan(step, s0, xs)
    return _from_chunks(jnp.moveaxis(out, 0, 2))


def _retention(q, k, v, log_gamma):
    bsz, seq, nh, dk = q.shape
    dv = v.shape[-1]
    n = seq // CHUNK
    k = k * (dk ** -0.5)
    q, k, v = _to_chunks(q, n), _to_chunks(k, n), _to_chunks(v, n)
    idx = jnp.arange(CHUNK, dtype=jnp.float32)
    dist = idx[:, None] - idx[None, :]
    causal = dist >= 0
    dmask = jnp.where(causal, jnp.exp(jnp.where(causal, dist, 0.0) * log_gamma[:, None, None]), 0.0)
    scores = jnp.einsum('bhnck,bhnsk->bhncs', q, k) * dmask[:, None]
    inner = jnp.einsum('bhncs,bhnsv->bhncv', scores, v)
    xi = jnp.exp((idx + 1.0) * log_gamma[:, None])
    zeta = jnp.exp((CHUNK - 1.0 - idx) * log_gamma[:, None])
    g_chunk = jnp.exp(CHUNK * log_gamma)
    kz = k * zeta[:, None, :, None]

    def step(state, inp):
        qc, kc, vc = inp
        out = jnp.einsum('bhck,bhkv->bhcv', qc, state) * xi[:, :, None]
        state = state * g_chunk[:, None, None] + jnp.einsum('bhck,bhcv->bhkv', kc, vc)
        return state, out

    s0 = jnp.zeros((bsz, nh, dk, dv), jnp.float32)
    _, cross = lax.scan(step, s0, (jnp.moveaxis(q, 2, 0), jnp.moveaxis(kz, 2, 0), jnp.moveaxis(v, 2, 0)))
    return _from_chunks(jnp.moveaxis(cross, 0, 2) + inner)


def _rg_lru(x, w_a, b_a, w_x, b_x, lam):
    bsz, seq, width = x.shape
    xb = x.reshape(bsz, seq, LRU_BLOCKS, LRU_BLOCK)
    r = jax.nn.sigmoid(jnp.einsum('btnc,ncd->btnd', xb, w_a) + b_a).reshape(bsz, seq, width)
    i = jax.nn.sigmoid(jnp.einsum('btnc,ncd->btnd', xb, w_x) + b_x).reshape(bsz, seq, width)
    log_a = -LRU_C * r * jax.nn.softplus(-lam)
    a = jnp.exp(log_a)
    b = jnp.sqrt(-jnp.expm1(2.0 * log_a)) * (i * x)

    def combine(left, right):
        a1, b1 = left
        a2, b2 = right
        return a1 * a2, a2 * b1 + b2

    _, h = lax.associative_scan(combine, (a, b), axis=1)
    return h


def _hybrid_mixer(h, positions, log_gamma, w_in, dn_conv_w, dn_a_log, dn_dt_bias, dn_norm_w,
                  lru_conv_w, lru_conv_b, lru_wa, lru_ba, lru_wx, lru_bx, lru_lambda, w_branch, w_out):
    bsz, seq, _ = h.shape
    f32 = jnp.float32
    u = h @ w_in
    split_points = [int(p) for p in np.cumsum(IN_SIZES)[:-1]]
    (a_qkv, a_alpha, a_beta, a_z, b_q, b_k, b_v, b_g, c_x, c_g, gate_logits) = jnp.split(u, split_points, axis=-1)

    a_qkv = jax.nn.silu(_causal_dwconv(a_qkv, dn_conv_w)).astype(f32)
    qa, ka, va = jnp.split(a_qkv, [DN_QK, 2 * DN_QK], axis=-1)
    qa = qa.reshape(bsz, seq, DN_HEADS, DN_DK)
    ka = ka.reshape(bsz, seq, DN_HEADS, DN_DK)
    va = va.reshape(bsz, seq, DN_HEADS, DN_DV)
    g_a = -jnp.exp(dn_a_log.astype(f32)) * jax.nn.softplus(a_alpha.astype(f32) + dn_dt_bias.astype(f32))
    beta_a = jax.nn.sigmoid(a_beta.astype(f32))
    oa = _gated_delta_rule(qa, ka, va, g_a, beta_a)
    z_a = a_z.astype(f32).reshape(bsz, seq, DN_HEADS, DN_DV)
    oa = _head_rms(oa) * dn_norm_w.astype(f32) * jax.nn.silu(z_a)
    y_a = oa.reshape(bsz, seq, DN_VW).astype(h.dtype)

    qb = _rotary(b_q.astype(f32).reshape(bsz, seq, RET_HEADS, RET_DK), positions)
    kb = _rotary(b_k.astype(f32).reshape(bsz, seq, RET_HEADS, RET_DK), positions)
    vb = b_v.astype(f32).reshape(bsz, seq, RET_HEADS, RET_DV)
    ob = _retention(qb, kb, vb, log_gamma)
    ob = _head_groupnorm(ob) * jax.nn.silu(b_g.astype(f32).reshape(bsz, seq, RET_HEADS, RET_DV))
    y_b = ob.reshape(bsz, seq, RET_VW).astype(h.dtype)

    xc = _causal_dwconv(c_x, lru_conv_w, lru_conv_b).astype(f32)
    hc = _rg_lru(xc, lru_wa.astype(f32), lru_ba.astype(f32), lru_wx.astype(f32), lru_bx.astype(f32), lru_lambda.astype(f32))
    y_c = (hc * jax.nn.gelu(c_g.astype(f32))).astype(h.dtype)

    ys = jnp.stack([y_a, y_b, y_c], axis=2)
    branch = jnp.einsum('btnw,nwd->btnd', ys, w_branch)
    gates = jax.nn.sigmoid(gate_logits.reshape(bsz, seq, N_BRANCH, D_MODEL))
    merged = jnp.sum(gates * branch, axis=2)
    return merged @ w_out


def _conv_ffn(h, w_up, conv_w, conv_b, w_down):
    up = _causal_dwconv(h @ w_up, conv_w, conv_b)
    gate, val = jnp.split(up, 2, axis=-1)
    return (jax.nn.silu(gate) * val) @ w_down


def setup_inputs(seed: int = 0) -> dict:
    key = jax.random.key(seed)
    ks = jax.random.split(key, 24)
    f32 = jnp.float32

    def nrm(k, shape, scale):
        return jax.random.normal(k, shape, f32) * scale

    x = nrm(ks[0], (BATCH, SEQ, D_MODEL), 1.0)
    offsets = jax.random.randint(ks[1], (BATCH, 1), 0, 4096, dtype=jnp.int32)
    positions = offsets + jnp.arange(SEQ, dtype=jnp.int32)[None, :]
    attn_norm = 1.0 + nrm(ks[2], (DEPTH, D_MODEL), 0.02)
    w_in = nrm(ks[3], (DEPTH, D_MODEL, N_IN), D_MODEL ** -0.5)
    dn_conv_w = nrm(ks[4], (DEPTH, DN_CONV, DN_QKV), DN_CONV ** -0.5)
    dn_a_log = jnp.log(jax.random.uniform(ks[5], (DEPTH, DN_HEADS), f32, 1.0, 16.0))
    dt = jnp.exp(jax.random.uniform(ks[6], (DEPTH, DN_HEADS), f32, math.log(0.001), math.log(0.1)))
    dn_dt_bias = dt + jnp.log(-jnp.expm1(-dt))
    dn_norm_w = 1.0 + nrm(ks[7], (DEPTH, DN_DV), 0.02)
    lru_conv_w = nrm(ks[8], (DEPTH, LRU_CONV, LRU_WIDTH), LRU_CONV ** -0.5)
    lru_conv_b = nrm(ks[9], (DEPTH, LRU_WIDTH), 0.01)
    lru_wa = nrm(ks[10], (DEPTH, LRU_BLOCKS, LRU_BLOCK, LRU_BLOCK), LRU_BLOCK ** -0.5)
    lru_ba = nrm(ks[11], (DEPTH, LRU_BLOCKS, LRU_BLOCK), 0.01)
    lru_wx = nrm(ks[12], (DEPTH, LRU_BLOCKS, LRU_BLOCK, LRU_BLOCK), LRU_BLOCK ** -0.5)
    lru_bx = nrm(ks[13], (DEPTH, LRU_BLOCKS, LRU_BLOCK), 0.01)
    a0 = jax.random.uniform(ks[14], (DEPTH, LRU_WIDTH), f32, 0.9, 0.999) ** (1.0 / LRU_C)
    lru_lambda = jnp.log(a0) - jnp.log1p(-a0)
    w_branch = nrm(ks[15], (DEPTH, N_BRANCH, BRANCH_WIDTH, D_MODEL), BRANCH_WIDTH ** -0.5)
    w_out = nrm(ks[16], (DEPTH, D_MODEL, D_MODEL), D_MODEL ** -0.5)
    ffn_norm = 1.0 + nrm(ks[17], (DEPTH, D_MODEL), 0.02)
    w_up = nrm(ks[18], (DEPTH, D_MODEL, 2 * D_FF), D_MODEL ** -0.5)
    ffn_conv_w = nrm(ks[19], (DEPTH, FFN_CONV, 2 * D_FF), FFN_CONV ** -0.5)
    ffn_conv_b = nrm(ks[20], (DEPTH, 2 * D_FF), 0.01)
    w_down = nrm(ks[21], (DEPTH, D_FF, D_MODEL), D_FF ** -0.5)
    final_norm = 1.0 + nrm(ks[22], (D_MODEL,), 0.02)
    return {'x': x, 'positions': positions, 'attn_norm': attn_norm, 'w_in': w_in, 'dn_conv_w': dn_conv_w,
            'dn_a_log': dn_a_log, 'dn_dt_bias': dn_dt_bias, 'dn_norm_w': dn_norm_w, 'lru_conv_w': lru_conv_w,
            'lru_conv_b': lru_conv_b, 'lru_wa': lru_wa, 'lru_ba': lru_ba, 'lru_wx': lru_wx, 'lru_bx': lru_bx,
            'lru_lambda': lru_lambda, 'w_branch': w_branch, 'w_out': w_out, 'ffn_norm': ffn_norm, 'w_up': w_up,
            'ffn_conv_w': ffn_conv_w, 'ffn_conv_b': ffn_conv_b, 'w_down': w_down, 'final_norm': final_norm}


def reference(x, positions, attn_norm, w_in, dn_conv_w, dn_a_log, dn_dt_bias, dn_norm_w, lru_conv_w, lru_conv_b,
              lru_wa, lru_ba, lru_wx, lru_bx, lru_lambda, w_branch, w_out, ffn_norm, w_up, ffn_conv_w, ffn_conv_b,
              w_down, final_norm):
    log_gamma = jnp.log(1.0 - 2.0 ** (-5.0 - jnp.arange(RET_HEADS, dtype=jnp.float32)))
    for layer in range(DEPTH):
        h = _rms_norm(x, attn_norm[layer])
        x = x + _hybrid_mixer(h, positions, log_gamma, w_in[layer], dn_conv_w[layer], dn_a_log[layer],
                              dn_dt_bias[layer], dn_norm_w[layer], lru_conv_w[layer], lru_conv_b[layer],
                              lru_wa[layer], lru_ba[layer], lru_wx[layer], lru_bx[layer], lru_lambda[layer],
                              w_branch[layer], w_out[layer])
        h = _rms_norm(x, ffn_norm[layer])
        x = x + _conv_ffn(h, w_up[layer], ffn_conv_w[layer], ffn_conv_b[layer], w_down[layer])
    return _rms_norm(x, final_norm)
```

```python
import functools
import math

import numpy as np
import jax
import jax.numpy as jnp
from jax import lax
from jax.experimental import pallas as pl
from jax.experimental.pallas import tpu as pltpu

F32 = jnp.float32
BF16 = jnp.bfloat16
HIGHEST = lax.Precision.HIGHEST

D_MODEL = 1024
DEPTH = 4
CHUNK = 64
EPS = 1e-6
N_BRANCH = 3
BRANCH_WIDTH = 512

DN_HEADS = 4
DN_DK = 128
DN_DV = 128
DN_CONV = 4
DN_QK = DN_HEADS * DN_DK
DN_VW = DN_HEADS * DN_DV
DN_QKV = 2 * DN_QK + DN_VW

RET_HEADS = 4
RET_DK = 64
RET_DV = 128
RET_QK = RET_HEADS * RET_DK
RET_VW = RET_HEADS * RET_DV
ROPE_BASE = 10000.0

LRU_WIDTH = 512
LRU_BLOCKS = 4
LRU_BLOCK = LRU_WIDTH // LRU_BLOCKS
LRU_CONV = 4
LRU_C = 8.0

D_FF = 2816
FFN_CONV = 3

SUBLANES = 8
LANES = 128
VMEM_LIMIT_BYTES = 56 * 1024 * 1024

RET_W = 2 * RET_QK + 2 * RET_VW
LRU_W = 2 * LRU_WIDTH
OFF_DN = 0
OFF_RET = OFF_DN + DN_QKV
OFF_LRU = OFF_RET + RET_W
OFF_GATE = OFF_LRU + LRU_W
OFF_Z = OFF_GATE + N_BRANCH * D_MODEL
N_MAIN = OFF_Z + DN_VW
AB_ROWS = 16

TM_IN = 1024
TN_IN = 1536
TT_MIX = 256
TM_MERGE = 512
TT_FFN = 512
FFN_COLS = 256


def _params(sem):
    return pltpu.CompilerParams(dimension_semantics=sem, vmem_limit_bytes=VMEM_LIMIT_BYTES)


def _sigmoid(x):
    return 1.0 / (1.0 + jnp.exp(-x))


def _silu(x):
    return x * _sigmoid(x)


def _softplus(x):
    return jnp.maximum(x, 0.0) + jnp.log1p(jnp.exp(-jnp.abs(x)))


def _dot(a, b):
    return jnp.dot(a.astype(BF16), b.astype(BF16), preferred_element_type=F32)


def _dot_nt(a, b):
    return lax.dot_general(a.astype(BF16), b.astype(BF16), (((1,), (1,)), ((), ())),
                           preferred_element_type=F32)


def _dot_tn(a, b):
    return lax.dot_general(a.astype(BF16), b.astype(BF16), (((0,), (0,)), ((), ())),
                           preferred_element_type=F32)


def _dot_f32(a, b):
    return jnp.dot(a, b, preferred_element_type=F32, precision=HIGHEST)


def _iota2(shape, dim):
    return lax.broadcasted_iota(jnp.int32, shape, dim)


def _in_proj_kernel(x_ref, g_ref, w_ref, wab_ref, wabt_ref, u_ref, ab_ref, abt_ref, h_scr):
    @pl.when(pl.program_id(1) == 0)
    def _():
        x = x_ref[...]
        h = x * lax.rsqrt(jnp.mean(x * x, axis=-1, keepdims=True) + EPS) * g_ref[...]
        hb = h.astype(BF16)
        h_scr[...] = hb
        ab_ref[...] = jnp.dot(hb, wab_ref[...], preferred_element_type=F32)
        abt_ref[...] = lax.dot_general(wabt_ref[...], hb, (((1,), (1,)), ((), ())),
                                       preferred_element_type=F32)

    u_ref[...] = jnp.dot(h_scr[...], w_ref[...], preferred_element_type=F32).astype(BF16)


def _in_proj(x2, g, w_main, w_ab, w_abt):
    m = x2.shape[0]
    tm, tn = min(TM_IN, m), TN_IN
    return pl.pallas_call(
        _in_proj_kernel,
        grid=(m // tm, N_MAIN // tn),
        in_specs=[
            pl.BlockSpec((tm, D_MODEL), lambda i, j: (i, 0)),
            pl.BlockSpec((1, D_MODEL), lambda i, j: (0, 0)),
            pl.BlockSpec((D_MODEL, tn), lambda i, j: (0, j)),
            pl.BlockSpec((D_MODEL, LANES), lambda i, j: (0, 0)),
            pl.BlockSpec((AB_ROWS, D_MODEL), lambda i, j: (0, 0)),
        ],
        out_specs=[
            pl.BlockSpec((tm, tn), lambda i, j: (i, j)),
            pl.BlockSpec((tm, LANES), lambda i, j: (i, 0)),
            pl.BlockSpec((AB_ROWS, tm), lambda i, j: (0, i)),
        ],
        out_shape=[
            jax.ShapeDtypeStruct((m, N_MAIN), BF16),
            jax.ShapeDtypeStruct((m, LANES), F32),
            jax.ShapeDtypeStruct((AB_ROWS, m), F32),
        ],
        scratch_shapes=[pltpu.VMEM((tm, D_MODEL), BF16)],
        compiler_params=_params(("parallel", "arbitrary")),
        name="in_proj",
    )(x2, g, w_main, w_ab, w_abt)


def _causal_conv(ext_scr, x_f32, w, width, tt, first):
    @pl.when(first)
    def _():
        ext_scr[0:SUBLANES, :] = jnp.zeros((SUBLANES, ext_scr.shape[1]), F32)

    ext_scr[SUBLANES:SUBLANES + tt, :] = x_f32
    y = None
    for j in range(width):
        off = SUBLANES - (width - 1) + j
        term = ext_scr[off:off + tt, :] * w[j:j + 1, :]
        y = term if y is None else y + term
    ext_scr[0:SUBLANES, :] = ext_scr[tt:tt + SUBLANES, :]
    return y


def _dn_kernel(qkv_ref, z_ref, ab_ref, abt_ref, convw_ref, hpr_ref, hpc_ref, normw_ref, o_ref,
               ext_scr, s_scr, *, tt):
    first = pl.program_id(1) == 0

    @pl.when(first)
    def _():
        s_scr[...] = jnp.zeros(s_scr.shape, F32)

    a = _silu(_causal_conv(ext_scr, qkv_ref[0].astype(F32), convw_ref[...], DN_CONV, tt, first))

    ab = ab_ref[...]
    abt = abt_ref[...]
    g_cols = -jnp.exp(hpr_ref[0:1, :]) * _softplus(ab + hpr_ref[1:2, :])
    beta_cols = _sigmoid(ab)
    g_rows = -jnp.exp(hpc_ref[:, 0:1]) * _softplus(abt + hpc_ref[:, 1:2])

    ri = _iota2((CHUNK, CHUNK), 0)
    ci = _iota2((CHUNK, CHUNK), 1)
    causal = ri >= ci
    strict = ri > ci
    tril = jnp.where(causal, 1.0, 0.0).astype(F32)
    triu = jnp.where(ri <= ci, 1.0, 0.0).astype(F32)
    eye = jnp.where(ri == ci, 1.0, 0.0).astype(F32)
    normw = normw_ref[...]
    z = z_ref[0].astype(F32)

    for c in range(tt // CHUNK):
        r0 = c * CHUNK
        gc_cols = _dot_f32(tril, g_cols[r0:r0 + CHUNK, :])
        gc_rows = _dot_f32(g_rows[:, r0:r0 + CHUNK], triu)
        for h in range(DN_HEADS):
            q = a[r0:r0 + CHUNK, h * DN_DK:(h + 1) * DN_DK]
            k = a[r0:r0 + CHUNK, DN_QK + h * DN_DK:DN_QK + (h + 1) * DN_DK]
            v = a[r0:r0 + CHUNK, 2 * DN_QK + h * DN_DV:2 * DN_QK + (h + 1) * DN_DV]
            q = q * lax.rsqrt(jnp.sum(q * q, axis=-1, keepdims=True) + EPS) * (DN_DK ** -0.5)
            k = k * lax.rsqrt(jnp.sum(k * k, axis=-1, keepdims=True) + EPS)
            gcc = gc_cols[:, h:h + 1]
            gcr = gc_rows[h:h + 1, :]
            beta = beta_cols[r0:r0 + CHUNK, DN_HEADS + h:DN_HEADS + h + 1]
            decay = jnp.where(causal, jnp.exp(jnp.where(causal, gcc - gcr, 0.0)), 0.0)
            eg = jnp.exp(gcc)
            g_last = gcc[CHUNK - 1:CHUNK, :]
            k_beta = k * beta
            low = jnp.where(strict, _dot_nt(k_beta, k) * decay, 0.0)
            t_inv = eye - low
            p = low
            for _ in range(5):
                p = _dot_f32(p, p)
                t_inv = t_inv + _dot_f32(t_inv, p)
            u = _dot(t_inv, v * beta)
            w = _dot(t_inv, k_beta * eg)
            attn = _dot_nt(q, k) * decay
            state = s_scr[h]
            v_new = u - _dot(w, state)
            out = _dot(q * eg, state) + _dot(attn, v_new)
            k_dec = k * jnp.exp(g_last - gcc)
            s_scr[h] = state * jnp.exp(g_last) + _dot_tn(k_dec, v_new)
            out = out * lax.rsqrt(jnp.mean(out * out, axis=-1, keepdims=True) + EPS)
            out = out * normw * _silu(z[r0:r0 + CHUNK, h * DN_DV:(h + 1) * DN_DV])
            o_ref[0, r0:r0 + CHUNK, h * DN_DV:(h + 1) * DN_DV] = out.astype(BF16)


def _deltanet(u3, ab, abt, conv_w, hpr, hpc, norm_w):
    b, t, _ = u3.shape
    tt = min(TT_MIX, t)
    nt = t // tt
    return pl.pallas_call(
        functools.partial(_dn_kernel, tt=tt),
        grid=(b, nt),
        in_specs=[
            pl.BlockSpec((1, tt, DN_QKV), lambda i, j: (i, j, OFF_DN // DN_QKV)),
            pl.BlockSpec((1, tt, DN_VW), lambda i, j: (i, j, OFF_Z // DN_VW)),
            pl.BlockSpec((tt, LANES), lambda i, j: (i * nt + j, 0)),
            pl.BlockSpec((AB_ROWS, tt), lambda i, j: (0, i * nt + j)),
            pl.BlockSpec((DN_CONV, DN_QKV), lambda i, j: (0, 0)),
            pl.BlockSpec((SUBLANES, LANES), lambda i, j: (0, 0)),
            pl.BlockSpec((AB_ROWS, LANES), lambda i, j: (0, 0)),
            pl.BlockSpec((1, DN_DV), lambda i, j: (0, 0)),
        ],
        out_specs=pl.BlockSpec((1, tt, DN_VW), lambda i, j: (i, j, 0)),
        out_shape=jax.ShapeDtypeStruct((b, t, DN_VW), BF16),
        scratch_shapes=[
            pltpu.VMEM((tt + SUBLANES, DN_QKV), F32),
            pltpu.VMEM((DN_HEADS, DN_DK, DN_DV), F32),
        ],
        compiler_params=_params(("parallel", "arbitrary")),
        name="deltanet",
    )(u3, u3, ab, abt, conv_w, hpr, hpc, norm_w)


def _ret_kernel(r_ref, pos_ref, inv_ref, dmask_ref, xi_ref, zeta_ref, gch_ref, o_ref, s_scr, *, tt):
    @pl.when(pl.program_id(1) == 0)
    def _():
        s_scr[...] = jnp.zeros(s_scr.shape, F32)

    r = r_ref[0].astype(F32)
    ang = pos_ref[0].astype(F32) * inv_ref[...]
    cos = jnp.cos(ang)
    sin = jnp.sin(ang)
    lane = _iota2((tt, LANES), 1)
    first_half = (lane % RET_DK) < (RET_DK // 2)
    sin_signed = jnp.where(first_half, -sin, sin)

    def rotary(t):
        swapped = jnp.where(first_half, pltpu.roll(t, LANES - RET_DK // 2, 1),
                            pltpu.roll(t, RET_DK // 2, 1))
        return t * cos + swapped * sin_signed

    q = jnp.concatenate([rotary(r[:, i * LANES:(i + 1) * LANES]) for i in range(RET_QK // LANES)], axis=1)
    k = jnp.concatenate([rotary(r[:, RET_QK + i * LANES:RET_QK + (i + 1) * LANES])
                         for i in range(RET_QK // LANES)], axis=1) * (RET_DK ** -0.5)
    v = r[:, 2 * RET_QK:2 * RET_QK + RET_VW]
    gate = r[:, 2 * RET_QK + RET_VW:]

    for c in range(tt // CHUNK):
        r0 = c * CHUNK
        for h in range(RET_HEADS):
            qc = q[r0:r0 + CHUNK, h * RET_DK:(h + 1) * RET_DK]
            kc = k[r0:r0 + CHUNK, h * RET_DK:(h + 1) * RET_DK]
            vc = v[r0:r0 + CHUNK, h * RET_DV:(h + 1) * RET_DV]
            scores = _dot_nt(qc, kc) * dmask_ref[h]
            inner = _dot(scores, vc)
            state = s_scr[h]
            cross = _dot(qc, state) * xi_ref[:, h:h + 1]
            s_scr[h] = state * gch_ref[h:h + 1, :] + _dot_tn(kc * zeta_ref[:, h:h + 1], vc)
            out = cross + inner
            mu = jnp.mean(out, axis=-1, keepdims=True)
            cen = out - mu
            var = jnp.mean(cen * cen, axis=-1, keepdims=True)
            out = cen * lax.rsqrt(var + EPS) * _silu(gate[r0:r0 + CHUNK, h * RET_DV:(h + 1) * RET_DV])
            o_ref[0, r0:r0 + CHUNK, h * RET_DV:(h + 1) * RET_DV] = out.astype(BF16)


def _retention(u3, pos3, inv_row, dmask, xi, zeta, gch):
    b, t, _ = u3.shape
    tt = min(TT_MIX, t)
    return pl.pallas_call(
        functools.partial(_ret_kernel, tt=tt),
        grid=(b, t // tt),
        in_specs=[
            pl.BlockSpec((1, tt, RET_W), lambda i, j: (i, j, OFF_RET // RET_W)),
            pl.BlockSpec((1, tt, 1), lambda i, j: (i, j, 0)),
            pl.BlockSpec((1, LANES), lambda i, j: (0, 0)),
            pl.BlockSpec((RET_HEADS, CHUNK, CHUNK), lambda i, j: (0, 0, 0)),
            pl.BlockSpec((CHUNK, LANES), lambda i, j: (0, 0)),
            pl.BlockSpec((CHUNK, LANES), lambda i, j: (0, 0)),
            pl.BlockSpec((SUBLANES, LANES), lambda i, j: (0, 0)),
        ],
        out_specs=pl.BlockSpec((1, tt, RET_VW), lambda i, j: (i, j, 0)),
        out_shape=jax.ShapeDtypeStruct((b, t, RET_VW), BF16),
        scratch_shapes=[pltpu.VMEM((RET_HEADS, RET_DK, RET_DV), F32)],
        compiler_params=_params(("parallel", "arbitrary")),
        name="retention",
    )(u3, pos3, inv_row, dmask, xi, zeta, gch)


def _lru_kernel(l_ref, convw_ref, convb_ref, wa_ref, ba_ref, wx_ref, bx_ref, lam_ref, o_ref,
                ext_scr, h_scr, *, tt):
    first = pl.program_id(1) == 0

    @pl.when(first)
    def _():
        h_scr[...] = jnp.zeros(h_scr.shape, F32)

    l = l_ref[0]
    xc = _causal_conv(ext_scr, l[:, 0:LRU_WIDTH].astype(F32), convw_ref[...], LRU_CONV, tt, first)
    xc = xc + convb_ref[...]
    rs, gs = [], []
    for n in range(LRU_BLOCKS):
        xb = xc[:, n * LRU_BLOCK:(n + 1) * LRU_BLOCK].astype(BF16)
        rs.append(jnp.dot(xb, wa_ref[n], preferred_element_type=F32))
        gs.append(jnp.dot(xb, wx_ref[n], preferred_element_type=F32))
    rgate = _sigmoid(jnp.concatenate(rs, axis=1) + ba_ref[...])
    igate = _sigmoid(jnp.concatenate(gs, axis=1) + bx_ref[...])
    log_a = (-LRU_C) * rgate * _softplus(-lam_ref[...])
    a = jnp.exp(log_a)
    one_minus = -jnp.tanh(log_a) * (a * a + 1.0)
    bv = jnp.sqrt(one_minus) * (igate * xc)

    row = _iota2((tt, LRU_WIDTH), 0)
    s = 1
    while s < tt:
        keep = row >= s
        a_prev = jnp.where(keep, pltpu.roll(a, s, 0), 1.0)
        b_prev = jnp.where(keep, pltpu.roll(bv, s, 0), 0.0)
        bv = a * b_prev + bv
        a = a * a_prev
        s *= 2
    hseq = a * h_scr[...] + bv
    h_scr[...] = hseq[tt - 1:tt, :]
    g = l[:, LRU_WIDTH:].astype(F32)
    gelu = 0.5 * g * (1.0 + jnp.tanh(math.sqrt(2.0 / math.pi) * (g + 0.044715 * (g * g * g))))
    o_ref[0] = (hseq * gelu).astype(BF16)


def _rglru(u3, conv_w, conv_b, wa, ba, wx, bx, lam):
    b, t, _ = u3.shape
    tt = min(TT_MIX, t)
    full2 = lambda i, j: (0, 0)
    return pl.pallas_call(
        functools.partial(_lru_kernel, tt=tt),
        grid=(b, t // tt),
        in_specs=[
            pl.BlockSpec((1, tt, LRU_W), lambda i, j: (i, j, OFF_LRU // LRU_W)),
            pl.BlockSpec((LRU_CONV, LRU_WIDTH), full2),
            pl.BlockSpec((1, LRU_WIDTH), full2),
            pl.BlockSpec((LRU_BLOCKS, LRU_BLOCK, LRU_BLOCK), lambda i, j: (0, 0, 0)),
            pl.BlockSpec((1, LRU_WIDTH), full2),
            pl.BlockSpec((LRU_BLOCKS, LRU_BLOCK, LRU_BLOCK), lambda i, j: (0, 0, 0)),
            pl.BlockSpec((1, LRU_WIDTH), full2),
            pl.BlockSpec((1, LRU_WIDTH), full2),
        ],
        out_specs=pl.BlockSpec((1, tt, LRU_WIDTH), lambda i, j: (i, j, 0)),
        out_shape=jax.ShapeDtypeStruct((b, t, LRU_WIDTH), BF16),
        scratch_shapes=[
            pltpu.VMEM((tt + SUBLANES, LRU_WIDTH), F32),
            pltpu.VMEM((1, LRU_WIDTH), F32),
        ],
        compiler_params=_params(("parallel", "arbitrary")),
        name="rglru",
    )(u3, conv_w, conv_b, wa, ba, wx, bx, lam)


def _merge_kernel(ya_ref, yb_ref, yc_ref, g0_ref, g1_ref, g2_ref, x_ref, wb_ref, wo_ref, o_ref):
    merged = _sigmoid(g0_ref[...].astype(F32)) * jnp.dot(ya_ref[...], wb_ref[0], preferred_element_type=F32)
    merged = merged + _sigmoid(g1_ref[...].astype(F32)) * jnp.dot(yb_ref[...], wb_ref[1],
                                                                   preferred_element_type=F32)
    merged = merged + _sigmoid(g2_ref[...].astype(F32)) * jnp.dot(yc_ref[...], wb_ref[2],
                                                                   preferred_element_type=F32)
    o_ref[...] = x_ref[...] + jnp.dot(merged.astype(BF16), wo_ref[...], preferred_element_type=F32)


def _merge(ya, yb, yc, u2, x2, wb, wo):
    m = x2.shape[0]
    tm = min(TM_MERGE, m)
    gate_blk = OFF_GATE // D_MODEL
    yspec = pl.BlockSpec((tm, BRANCH_WIDTH), lambda i: (i, 0))
    return pl.pallas_call(
        _merge_kernel,
        grid=(m // tm,),
        in_specs=[
            yspec, yspec, yspec,
            pl.BlockSpec((tm, D_MODEL), lambda i: (i, gate_blk)),
            pl.BlockSpec((tm, D_MODEL), lambda i: (i, gate_blk + 1)),
            pl.BlockSpec((tm, D_MODEL), lambda i: (i, gate_blk + 2)),
            pl.BlockSpec((tm, D_MODEL), lambda i: (i, 0)),
            pl.BlockSpec((N_BRANCH, BRANCH_WIDTH, D_MODEL), lambda i: (0, 0, 0)),
            pl.BlockSpec((D_MODEL, D_MODEL), lambda i: (0, 0)),
        ],
        out_specs=pl.BlockSpec((tm, D_MODEL), lambda i: (i, 0)),
        out_shape=jax.ShapeDtypeStruct((m, D_MODEL), F32),
        compiler_params=_params(("parallel",)),
        name="merge",
    )(ya, yb, yc, u2, u2, u2, x2, wb, wo)


def _ffn_kernel(x_ref, g_ref, wup_ref, cw_ref, cb_ref, wdn_ref, fg_ref, o_ref,
                tail_scr, extg_scr, extv_scr, act_scr, *, tt, final):
    first = pl.program_id(1) == 0

    @pl.when(first)
    def _():
        tail_scr[...] = jnp.zeros(tail_scr.shape, F32)

    x = x_ref[0]
    hb = (x * lax.rsqrt(jnp.mean(x * x, axis=-1, keepdims=True) + EPS) * g_ref[...]).astype(BF16)

    def conv(ext_scr, pre, col0):
        ext_scr[0:SUBLANES, :] = tail_scr[:, col0:col0 + FFN_COLS]
        ext_scr[SUBLANES:SUBLANES + tt, :] = pre
        y = cb_ref[:, col0:col0 + FFN_COLS]
        for j in range(FFN_CONV):
            off = SUBLANES - (FFN_CONV - 1) + j
            y = y + ext_scr[off:off + tt, :] * cw_ref[j:j + 1, col0:col0 + FFN_COLS]
        tail_scr[:, col0:col0 + FFN_COLS] = ext_scr[tt:tt + SUBLANES, :]
        return y

    for c in range(D_FF // FFN_COLS):
        c0 = c * FFN_COLS
        gate = conv(extg_scr, jnp.dot(hb, wup_ref[:, c0:c0 + FFN_COLS], preferred_element_type=F32), c0)
        val = conv(extv_scr, jnp.dot(hb, wup_ref[:, D_FF + c0:D_FF + c0 + FFN_COLS],
                                     preferred_element_type=F32), D_FF + c0)
        act_scr[:, c0:c0 + FFN_COLS] = (_silu(gate) * val).astype(BF16)

    y = x + jnp.dot(act_scr[...], wdn_ref[...], preferred_element_type=F32)
    if final:
        y = y * lax.rsqrt(jnp.mean(y * y, axis=-1, keepdims=True) + EPS) * fg_ref[...]
    o_ref[0] = y


def _conv_ffn(x3, g, w_up, conv_w, conv_b, w_down, final_g, final):
    b, t, _ = x3.shape
    tt = min(TT_FFN, t)
    full2 = lambda i, j: (0, 0)
    return pl.pallas_call(
        functools.partial(_ffn_kernel, tt=tt, final=final),
        grid=(b, t // tt),
        in_specs=[
            pl.BlockSpec((1, tt, D_MODEL), lambda i, j: (i, j, 0)),
            pl.BlockSpec((1, D_MODEL), full2),
            pl.BlockSpec((D_MODEL, 2 * D_FF), full2, pipeline_mode=pl.Buffered(1)),
            pl.BlockSpec((FFN_CONV, 2 * D_FF), full2),
            pl.BlockSpec((1, 2 * D_FF), full2),
            pl.BlockSpec((D_FF, D_MODEL), full2, pipeline_mode=pl.Buffered(1)),
            pl.BlockSpec((1, D_MODEL), full2),
        ],
        out_specs=pl.BlockSpec((1, tt, D_MODEL), lambda i, j: (i, j, 0)),
        out_shape=jax.ShapeDtypeStruct((b, t, D_MODEL), F32),
        scratch_shapes=[
            pltpu.VMEM((SUBLANES, 2 * D_FF), F32),
            pltpu.VMEM((tt + SUBLANES, FFN_COLS), F32),
            pltpu.VMEM((tt + SUBLANES, FFN_COLS), F32),
            pltpu.VMEM((tt, D_FF), BF16),
        ],
        compiler_params=_params(("parallel", "arbitrary")),
        name="conv_ffn",
    )(x3, g, w_up, conv_w, conv_b, w_down, final_g)


def _retention_tables():
    log_gamma = np.log(1.0 - 2.0 ** (-5.0 - np.arange(RET_HEADS, dtype=np.float32))).astype(np.float32)
    idx = np.arange(CHUNK, dtype=np.float32)
    dist = idx[:, None] - idx[None, :]
    causal = dist >= 0
    dmask = np.where(causal, np.exp(np.where(causal, dist, 0.0) * log_gamma[:, None, None]), 0.0)
    xi = np.zeros((CHUNK, LANES), np.float32)
    zeta = np.zeros((CHUNK, LANES), np.float32)
    xi[:, :RET_HEADS] = np.exp((idx[:, None] + 1.0) * log_gamma[None, :])
    zeta[:, :RET_HEADS] = np.exp((CHUNK - 1.0 - idx[:, None]) * log_gamma[None, :])
    gch = np.zeros((SUBLANES, LANES), np.float32)
    gch[:RET_HEADS, :] = np.exp(CHUNK * log_gamma)[:, None]
    half = RET_DK // 2
    inv = (ROPE_BASE ** (-np.arange(half, dtype=np.float32) / half)).astype(np.float32)
    inv_row = np.tile(inv, LANES // half)[None, :]
    return (jnp.asarray(dmask, F32), jnp.asarray(xi), jnp.asarray(zeta), jnp.asarray(gch),
            jnp.asarray(inv_row, F32))


def _split_w_in(w_in):
    sizes = (DN_QKV, DN_HEADS, DN_HEADS, DN_VW, RET_QK, RET_QK, RET_VW, RET_VW, LRU_WIDTH, LRU_WIDTH,
             N_BRANCH * D_MODEL)
    offs = np.concatenate([[0], np.cumsum(sizes)])
    part = lambda n: w_in[..., int(offs[n]):int(offs[n + 1])]
    main = jnp.concatenate([part(0), part(4), part(5), part(6), part(7), part(8), part(9), part(10), part(3)],
                           axis=-1).astype(BF16)
    ab = jnp.concatenate([part(1), part(2)], axis=-1)
    w_ab = jnp.pad(ab, ((0, 0), (0, 0), (0, LANES - 2 * DN_HEADS))).astype(BF16)
    w_abt = jnp.pad(jnp.swapaxes(ab, -1, -2), ((0, 0), (0, AB_ROWS - 2 * DN_HEADS), (0, 0))).astype(BF16)
    return main, w_ab, w_abt


def kernel(x, positions, attn_norm, w_in, dn_conv_w, dn_a_log, dn_dt_bias, dn_norm_w, lru_conv_w, lru_conv_b,
           lru_wa, lru_ba, lru_wx, lru_bx, lru_lambda, w_branch, w_out, ffn_norm, w_up, ffn_conv_w, ffn_conv_b,
           w_down, final_norm):
    bsz, seq, _ = x.shape
    m = bsz * seq
    depth = w_in.shape[0]
    dmask, xi, zeta, gch, inv_row = _retention_tables()
    w_main, w_ab, w_abt = _split_w_in(w_in)
    pos3 = positions[:, :, None]
    hp = jnp.stack([dn_a_log, dn_dt_bias], axis=1).astype(F32)
    hpr = jnp.pad(hp, ((0, 0), (0, SUBLANES - 2), (0, LANES - DN_HEADS)))
    hpc = jnp.pad(jnp.swapaxes(hp, 1, 2), ((0, 0), (0, AB_ROWS - DN_HEADS), (0, LANES - 2)))
    wb = w_branch.astype(BF16)
    wo = w_out.astype(BF16)
    wup = w_up.astype(BF16)
    wdn = w_down.astype(BF16)
    wa = lru_wa.astype(BF16)
    wx = lru_wx.astype(BF16)

    x2 = x.reshape(m, D_MODEL)
    for l in range(depth):
        u2, ab, abt = _in_proj(x2, attn_norm[l][None, :], w_main[l], w_ab[l], w_abt[l])
        u3 = u2.reshape(bsz, seq, N_MAIN)
        ya = _deltanet(u3, ab, abt, dn_conv_w[l], hpr[l], hpc[l], dn_norm_w[l][None, :])
        yb = _retention(u3, pos3, inv_row, dmask, xi, zeta, gch)
        yc = _rglru(u3, lru_conv_w[l], lru_conv_b[l][None, :], wa[l], lru_ba[l].reshape(1, LRU_WIDTH),
                    wx[l], lru_bx[l].reshape(1, LRU_WIDTH), lru_lambda[l][None, :])
        x2 = _merge(ya.reshape(m, DN_VW), yb.reshape(m, RET_VW), yc.reshape(m, LRU_WIDTH), u2, x2, wb[l], wo[l])
        x3 = _conv_ffn(x2.reshape(bsz, seq, D_MODEL), ffn_norm[l][None, :], wup[l], ffn_conv_w[l],
                       ffn_conv_b[l][None, :], wdn[l], final_norm[None, :], final=(l == depth - 1))
        x2 = x3.reshape(m, D_MODEL)
    return x2.reshape(bsz, seq, D_MODEL)
```

```python
import functools
import math

import numpy as np
import jax
import jax.numpy as jnp
from jax import lax
from jax.experimental import pallas as pl
from jax.experimental.pallas import tpu as pltpu

F32 = jnp.float32
BF16 = jnp.bfloat16
HIGHEST = lax.Precision.HIGHEST

D_MODEL = 1024
DEPTH = 4
CHUNK = 64
EPS = 1e-6
N_BRANCH = 3
BRANCH_WIDTH = 512

DN_HEADS = 4
DN_DK = 128
DN_DV = 128
DN_CONV = 4
DN_QK = DN_HEADS * DN_DK
DN_VW = DN_HEADS * DN_DV
DN_QKV = 2 * DN_QK + DN_VW

RET_HEADS = 4
RET_DK = 64
RET_DV = 128
RET_QK = RET_HEADS * RET_DK
RET_VW = RET_HEADS * RET_DV
ROPE_BASE = 10000.0

LRU_WIDTH = 512
LRU_BLOCKS = 4
LRU_BLOCK = LRU_WIDTH // LRU_BLOCKS
LRU_CONV = 4
LRU_C = 8.0

D_FF = 2816
FFN_CONV = 3

SUBLANES = 8
LANES = 128
VMEM_LIMIT_BYTES = 56 * 1024 * 1024

RET_W = 2 * RET_QK + 2 * RET_VW
LRU_W = 2 * LRU_WIDTH
OFF_DN = 0
OFF_RET = OFF_DN + DN_QKV
OFF_LRU = OFF_RET + RET_W
OFF_GATE = OFF_LRU + LRU_W
OFF_Z = OFF_GATE + N_BRANCH * D_MODEL
N_MAIN = OFF_Z + DN_VW
AB_ROWS = 16

TM_IN = 1024
TN_IN = 1536
TT_MIX = 256
TM_MERGE = 512
TT_FFN = 512
FFN_COLS = 256


def _params(sem):
    return pltpu.CompilerParams(dimension_semantics=sem, vmem_limit_bytes=VMEM_LIMIT_BYTES)


def _sigmoid(x):
    return 1.0 / (1.0 + jnp.exp(-x))


def _silu(x):
    return x * _sigmoid(x)


def _softplus(x):
    return jnp.maximum(x, 0.0) + jnp.log1p(jnp.exp(-jnp.abs(x)))


def _dot(a, b):
    return jnp.dot(a.astype(BF16), b.astype(BF16), preferred_element_type=F32)


def _dot_nt(a, b):
    return lax.dot_general(a.astype(BF16), b.astype(BF16), (((1,), (1,)), ((), ())),
                           preferred_element_type=F32)


def _dot_tn(a, b):
    return lax.dot_general(a.astype(BF16), b.astype(BF16), (((0,), (0,)), ((), ())),
                           preferred_element_type=F32)


def _dot_f32(a, b):
    return jnp.dot(a, b, preferred_element_type=F32, precision=HIGHEST)


def _iota2(shape, dim):
    return lax.broadcasted_iota(jnp.int32, shape, dim)


def _in_proj_kernel(x_ref, g_ref, w_ref, wab_ref, wabt_ref, u_ref, ab_ref, abt_ref, h_scr):
    @pl.when(pl.program_id(1) == 0)
    def _():
        x = x_ref[...]
        h = x * lax.rsqrt(jnp.mean(x * x, axis=-1, keepdims=True) + EPS) * g_ref[...]
        hb = h.astype(BF16)
        h_scr[...] = hb
        ab_ref[...] = jnp.dot(hb, wab_ref[...], preferred_element_type=F32)
        abt_ref[...] = lax.dot_general(wabt_ref[...], hb, (((1,), (1,)), ((), ())),
                                       preferred_element_type=F32)

    u_ref[...] = jnp.dot(h_scr[...], w_ref[...], preferred_element_type=F32).astype(BF16)


def _in_proj(x2, g, w_main, w_ab, w_abt):
    m = x2.shape[0]
    tm, tn = min(TM_IN, m), TN_IN
    return pl.pallas_call(
        _in_proj_kernel,
        grid=(m // tm, N_MAIN // tn),
        in_specs=[
            pl.BlockSpec((tm, D_MODEL), lambda i, j: (i, 0)),
            pl.BlockSpec((1, D_MODEL), lambda i, j: (0, 0)),
            pl.BlockSpec((D_MODEL, tn), lambda i, j: (0, j)),
            pl.BlockSpec((D_MODEL, LANES), lambda i, j: (0, 0)),
            pl.BlockSpec((AB_ROWS, D_MODEL), lambda i, j: (0, 0)),
        ],
        out_specs=[
            pl.BlockSpec((tm, tn), lambda i, j: (i, j)),
            pl.BlockSpec((tm, LANES), lambda i, j: (i, 0)),
            pl.BlockSpec((AB_ROWS, tm), lambda i, j: (0, i)),
        ],
        out_shape=[
            jax.ShapeDtypeStruct((m, N_MAIN), BF16),
            jax.ShapeDtypeStruct((m, LANES), F32),
            jax.ShapeDtypeStruct((AB_ROWS, m), F32),
        ],
        scratch_shapes=[pltpu.VMEM((tm, D_MODEL), BF16)],
        compiler_params=_params(("parallel", "arbitrary")),
        name="in_proj",
    )(x2, g, w_main, w_ab, w_abt)


def _causal_conv(ext_scr, x_f32, w, width, tt, first):
    @pl.when(first)
    def _():
        ext_scr[0:SUBLANES, :] = jnp.zeros((SUBLANES, ext_scr.shape[1]), F32)

    ext_scr[SUBLANES:SUBLANES + tt, :] = x_f32
    y = None
    for j in range(width):
        off = SUBLANES - (width - 1) + j
        term = ext_scr[off:off + tt, :] * w[j:j + 1, :]
        y = term if y is None else y + term
    ext_scr[0:SUBLANES, :] = ext_scr[tt:tt + SUBLANES, :]
    return y


def _dn_kernel(qkv_ref, z_ref, ab_ref, abt_ref, convw_ref, hpr_ref, hpc_ref, normw_ref, o_ref,
               ext_scr, s_scr, *, tt):
    first = pl.program_id(1) == 0

    @pl.when(first)
    def _():
        s_scr[...] = jnp.zeros(s_scr.shape, F32)

    a = _silu(_causal_conv(ext_scr, qkv_ref[0].astype(F32), convw_ref[...], DN_CONV, tt, first))

    ab = ab_ref[...]
    abt = abt_ref[...]
    g_cols = -jnp.exp(hpr_ref[0:1, :]) * _softplus(ab + hpr_ref[1:2, :])
    beta_cols = _sigmoid(ab)
    g_rows = -jnp.exp(hpc_ref[:, 0:1]) * _softplus(abt + hpc_ref[:, 1:2])

    ri = _iota2((tt, tt), 0)
    ci = _iota2((tt, tt), 1)
    xr = ri ^ ci
    lower = ri > ci
    causal = (xr < CHUNK) & (ri >= ci)
    eye = jnp.where(ri == ci, 1.0, 0.0).astype(F32)
    gc_cols = _dot_f32(jnp.where(causal, 1.0, 0.0).astype(F32), g_cols)
    gc_rows = _dot_f32(g_rows, jnp.where((xr < CHUNK) & (ri <= ci), 1.0, 0.0).astype(F32))
    normw = normw_ref[...]
    z = z_ref[0].astype(F32)

    heads = range(DN_HEADS)
    q, k, v, gcc, beta, decay, eg, k_beta = [], [], [], [], [], [], [], []
    for h in heads:
        qh = a[:, h * DN_DK:(h + 1) * DN_DK]
        kh = a[:, DN_QK + h * DN_DK:DN_QK + (h + 1) * DN_DK]
        q.append(qh * lax.rsqrt(jnp.sum(qh * qh, axis=-1, keepdims=True) + EPS) * (DN_DK ** -0.5))
        k.append(kh * lax.rsqrt(jnp.sum(kh * kh, axis=-1, keepdims=True) + EPS))
        v.append(a[:, 2 * DN_QK + h * DN_DV:2 * DN_QK + (h + 1) * DN_DV])
        gcc.append(gc_cols[:, h:h + 1])
        beta.append(beta_cols[:, DN_HEADS + h:DN_HEADS + h + 1])
        decay.append(jnp.where(causal, jnp.exp(gcc[h] - gc_rows[h:h + 1, :]), 0.0))
        eg.append(jnp.exp(gcc[h]))
        k_beta.append(k[h] * beta[h])
    low = [jnp.where(lower, _dot_nt(k_beta[h], k[h]) * decay[h], 0.0) for h in heads]
    attn = [_dot_nt(q[h], k[h]) * decay[h] for h in heads]
    d4 = [jnp.where(xr < 4, low[h], 0.0) for h in heads]
    d4sq = [_dot(d4[h], d4[h]) for h in heads]
    t_inv = [(eye - d4[h]) + _dot(eye - d4[h], d4sq[h]) for h in heads]
    for lv in range(2, 6):
        off_t = [_dot(jnp.where((xr >> lv) == 1, low[h], 0.0), t_inv[h]) for h in heads]
        t_inv = [t_inv[h] - _dot(t_inv[h], off_t[h]) for h in heads]
    uw = [_dot(t_inv[h], jnp.concatenate([v[h] * beta[h], k_beta[h] * eg[h]], axis=1)) for h in heads]
    q_dec = [q[h] * eg[h] for h in heads]
    state = [s_scr[h] for h in heads]
    for c in range(tt // CHUNK):
        rows = slice(c * CHUNK, (c + 1) * CHUNK)
        g_last = [gcc[h][(c + 1) * CHUNK - 1:(c + 1) * CHUNK, :] for h in heads]
        v_new = [uw[h][rows, :DN_DV] - _dot(uw[h][rows, DN_DV:], state[h]) for h in heads]
        out = [_dot(q_dec[h][rows], state[h]) + _dot(attn[h][rows, rows], v_new[h]) for h in heads]
        k_dec = [k[h][rows] * jnp.exp(g_last[h] - gcc[h][rows]) for h in heads]
        state = [state[h] * jnp.exp(g_last[h]) + _dot_tn(k_dec[h], v_new[h]) for h in heads]
        for h in heads:
            o = out[h] * lax.rsqrt(jnp.mean(out[h] * out[h], axis=-1, keepdims=True) + EPS)
            o = o * normw * _silu(z[rows, h * DN_DV:(h + 1) * DN_DV])
            o_ref[0, rows, h * DN_DV:(h + 1) * DN_DV] = o.astype(BF16)
    for h in heads:
        s_scr[h] = state[h]


def _deltanet(u3, ab, abt, conv_w, hpr, hpc, norm_w):
    b, t, _ = u3.shape
    tt = min(TT_MIX, t)
    nt = t // tt
    return pl.pallas_call(
        functools.partial(_dn_kernel, tt=tt),
        grid=(b, nt),
        in_specs=[
            pl.BlockSpec((1, tt, DN_QKV), lambda i, j: (i, j, OFF_DN // DN_QKV)),
            pl.BlockSpec((1, tt, DN_VW), lambda i, j: (i, j, OFF_Z // DN_VW)),
            pl.BlockSpec((tt, LANES), lambda i, j: (i * nt + j, 0)),
            pl.BlockSpec((AB_ROWS, tt), lambda i, j: (0, i * nt + j)),
            pl.BlockSpec((DN_CONV, DN_QKV), lambda i, j: (0, 0)),
            pl.BlockSpec((SUBLANES, LANES), lambda i, j: (0, 0)),
            pl.BlockSpec((AB_ROWS, LANES), lambda i, j: (0, 0)),
            pl.BlockSpec((1, DN_DV), lambda i, j: (0, 0)),
        ],
        out_specs=pl.BlockSpec((1, tt, DN_VW), lambda i, j: (i, j, 0)),
        out_shape=jax.ShapeDtypeStruct((b, t, DN_VW), BF16),
        scratch_shapes=[
            pltpu.VMEM((tt + SUBLANES, DN_QKV), F32),
            pltpu.VMEM((DN_HEADS, DN_DK, DN_DV), F32),
        ],
        compiler_params=_params(("parallel", "arbitrary")),
        name="deltanet",
    )(u3, u3, ab, abt, conv_w, hpr, hpc, norm_w)


def _ret_kernel(r_ref, pos_ref, inv_ref, dmask_ref, xi_ref, zeta_ref, gch_ref, o_ref, s_scr, *, tt):
    @pl.when(pl.program_id(1) == 0)
    def _():
        s_scr[...] = jnp.zeros(s_scr.shape, F32)

    r = r_ref[0].astype(F32)
    ang = pos_ref[0].astype(F32) * inv_ref[...]
    cos = jnp.cos(ang)
    sin = jnp.sin(ang)
    lane = _iota2((tt, LANES), 1)
    first_half = (lane % RET_DK) < (RET_DK // 2)
    sin_signed = jnp.where(first_half, -sin, sin)

    def rotary(t):
        swapped = jnp.where(first_half, pltpu.roll(t, LANES - RET_DK // 2, 1),
                            pltpu.roll(t, RET_DK // 2, 1))
        return t * cos + swapped * sin_signed

    q = jnp.concatenate([rotary(r[:, i * LANES:(i + 1) * LANES]) for i in range(RET_QK // LANES)], axis=1)
    k = jnp.concatenate([rotary(r[:, RET_QK + i * LANES:RET_QK + (i + 1) * LANES])
                         for i in range(RET_QK // LANES)], axis=1) * (RET_DK ** -0.5)
    v = r[:, 2 * RET_QK:2 * RET_QK + RET_VW]
    gate = r[:, 2 * RET_QK + RET_VW:]

    heads = range(RET_HEADS)
    qh = [q[:, h * RET_DK:(h + 1) * RET_DK] for h in heads]
    kh = [k[:, h * RET_DK:(h + 1) * RET_DK] for h in heads]
    vh = [v[:, h * RET_DV:(h + 1) * RET_DV] for h in heads]
    state = [s_scr[h] for h in heads]
    scores = [_dot_nt(qh[h], kh[h]) * dmask_ref[h] for h in heads]
    cross = [_dot(qh[h], state[h]) * xi_ref[:, h:h + 1] for h in heads]
    inner = [_dot(scores[h], vh[h]) for h in heads]
    for h in heads:
        s_scr[h] = state[h] * gch_ref[h:h + 1, :] + _dot_tn(kh[h] * zeta_ref[:, h:h + 1], vh[h])
    for h in heads:
        out = cross[h] + inner[h]
        mu = jnp.mean(out, axis=-1, keepdims=True)
        cen = out - mu
        var = jnp.mean(cen * cen, axis=-1, keepdims=True)
        out = cen * lax.rsqrt(var + EPS) * _silu(gate[:, h * RET_DV:(h + 1) * RET_DV])
        o_ref[0, :, h * RET_DV:(h + 1) * RET_DV] = out.astype(BF16)


def _retention(u3, pos3, inv_row, tables):
    b, t, _ = u3.shape
    tt = min(TT_MIX, t)
    dmask, xi, zeta, gch = tables
    return pl.pallas_call(
        functools.partial(_ret_kernel, tt=tt),
        grid=(b, t // tt),
        in_specs=[
            pl.BlockSpec((1, tt, RET_W), lambda i, j: (i, j, OFF_RET // RET_W)),
            pl.BlockSpec((1, tt, 1), lambda i, j: (i, j, 0)),
            pl.BlockSpec((1, LANES), lambda i, j: (0, 0)),
            pl.BlockSpec((RET_HEADS, tt, tt), lambda i, j: (0, 0, 0)),
            pl.BlockSpec((tt, LANES), lambda i, j: (0, 0)),
            pl.BlockSpec((tt, LANES), lambda i, j: (0, 0)),
            pl.BlockSpec((SUBLANES, LANES), lambda i, j: (0, 0)),
        ],
        out_specs=pl.BlockSpec((1, tt, RET_VW), lambda i, j: (i, j, 0)),
        out_shape=jax.ShapeDtypeStruct((b, t, RET_VW), BF16),
        scratch_shapes=[pltpu.VMEM((RET_HEADS, RET_DK, RET_DV), F32)],
        compiler_params=_params(("parallel", "arbitrary")),
        name="retention",
    )(u3, pos3, inv_row, dmask, xi, zeta, gch)


def _lru_kernel(l_ref, convw_ref, convb_ref, wa_ref, ba_ref, wx_ref, bx_ref, lam_ref, o_ref,
                ext_scr, h_scr, *, tt):
    first = pl.program_id(1) == 0

    @pl.when(first)
    def _():
        h_scr[...] = jnp.zeros(h_scr.shape, F32)

    l = l_ref[0]
    xc = _causal_conv(ext_scr, l[:, 0:LRU_WIDTH].astype(F32), convw_ref[...], LRU_CONV, tt, first)
    xc = xc + convb_ref[...]
    rs, gs = [], []
    for n in range(LRU_BLOCKS):
        xb = xc[:, n * LRU_BLOCK:(n + 1) * LRU_BLOCK].astype(BF16)
        rs.append(jnp.dot(xb, wa_ref[n], preferred_element_type=F32))
        gs.append(jnp.dot(xb, wx_ref[n], preferred_element_type=F32))
    rgate = _sigmoid(jnp.concatenate(rs, axis=1) + ba_ref[...])
    igate = _sigmoid(jnp.concatenate(gs, axis=1) + bx_ref[...])
    log_a = (-LRU_C) * rgate * _softplus(-lam_ref[...])
    a = jnp.exp(log_a)
    one_minus = -jnp.tanh(log_a) * (a * a + 1.0)
    bv = jnp.sqrt(one_minus) * (igate * xc)

    row = _iota2((tt, LRU_WIDTH), 0)
    s = 1
    while s < tt:
        keep = row >= s
        a_prev = jnp.where(keep, pltpu.roll(a, s, 0), 1.0)
        b_prev = jnp.where(keep, pltpu.roll(bv, s, 0), 0.0)
        bv = a * b_prev + bv
        a = a * a_prev
        s *= 2
    hseq = a * h_scr[...] + bv
    h_scr[...] = hseq[tt - 1:tt, :]
    g = l[:, LRU_WIDTH:].astype(F32)
    gelu = 0.5 * g * (1.0 + jnp.tanh(math.sqrt(2.0 / math.pi) * (g + 0.044715 * (g * g * g))))
    o_ref[0] = (hseq * gelu).astype(BF16)


def _rglru(u3, conv_w, conv_b, wa, ba, wx, bx, lam):
    b, t, _ = u3.shape
    tt = min(TT_MIX, t)
    full2 = lambda i, j: (0, 0)
    return pl.pallas_call(
        functools.partial(_lru_kernel, tt=tt),
        grid=(b, t // tt),
        in_specs=[
            pl.BlockSpec((1, tt, LRU_W), lambda i, j: (i, j, OFF_LRU // LRU_W)),
            pl.BlockSpec((LRU_CONV, LRU_WIDTH), full2),
            pl.BlockSpec((1, LRU_WIDTH), full2),
            pl.BlockSpec((LRU_BLOCKS, LRU_BLOCK, LRU_BLOCK), lambda i, j: (0, 0, 0)),
            pl.BlockSpec((1, LRU_WIDTH), full2),
            pl.BlockSpec((LRU_BLOCKS, LRU_BLOCK, LRU_BLOCK), lambda i, j: (0, 0, 0)),
            pl.BlockSpec((1, LRU_WIDTH), full2),
            pl.BlockSpec((1, LRU_WIDTH), full2),
        ],
        out_specs=pl.BlockSpec((1, tt, LRU_WIDTH), lambda i, j: (i, j, 0)),
        out_shape=jax.ShapeDtypeStruct((b, t, LRU_WIDTH), BF16),
        scratch_shapes=[
            pltpu.VMEM((tt + SUBLANES, LRU_WIDTH), F32),
            pltpu.VMEM((1, LRU_WIDTH), F32),
        ],
        compiler_params=_params(("parallel", "arbitrary")),
        name="rglru",
    )(u3, conv_w, conv_b, wa, ba, wx, bx, lam)


def _merge_kernel(ya_ref, yb_ref, yc_ref, g0_ref, g1_ref, g2_ref, x_ref, wb_ref, wo_ref, o_ref):
    merged = _sigmoid(g0_ref[...].astype(F32)) * jnp.dot(ya_ref[...], wb_ref[0], preferred_element_type=F32)
    merged = merged + _sigmoid(g1_ref[...].astype(F32)) * jnp.dot(yb_ref[...], wb_ref[1],
                                                                   preferred_element_type=F32)
    merged = merged + _sigmoid(g2_ref[...].astype(F32)) * jnp.dot(yc_ref[...], wb_ref[2],
                                                                   preferred_element_type=F32)
    o_ref[...] = x_ref[...] + jnp.dot(merged.astype(BF16), wo_ref[...], preferred_element_type=F32)


def _merge(ya, yb, yc, u2, x2, wb, wo):
    m = x2.shape[0]
    tm = min(TM_MERGE, m)
    gate_blk = OFF_GATE // D_MODEL
    yspec = pl.BlockSpec((tm, BRANCH_WIDTH), lambda i: (i, 0))
    return pl.pallas_call(
        _merge_kernel,
        grid=(m // tm,),
        in_specs=[
            yspec, yspec, yspec,
            pl.BlockSpec((tm, D_MODEL), lambda i: (i, gate_blk)),
            pl.BlockSpec((tm, D_MODEL), lambda i: (i, gate_blk + 1)),
            pl.BlockSpec((tm, D_MODEL), lambda i: (i, gate_blk + 2)),
            pl.BlockSpec((tm, D_MODEL), lambda i: (i, 0)),
            pl.BlockSpec((N_BRANCH, BRANCH_WIDTH, D_MODEL), lambda i: (0, 0, 0)),
            pl.BlockSpec((D_MODEL, D_MODEL), lambda i: (0, 0)),
        ],
        out_specs=pl.BlockSpec((tm, D_MODEL), lambda i: (i, 0)),
        out_shape=jax.ShapeDtypeStruct((m, D_MODEL), F32),
        compiler_params=_params(("parallel",)),
        name="merge",
    )(ya, yb, yc, u2, u2, u2, x2, wb, wo)


def _ffn_kernel(x_ref, g_ref, wup_ref, cw_ref, cb_ref, wdn_ref, fg_ref, o_ref,
                tail_scr, extg_scr, extv_scr, act_scr, *, tt, final):
    first = pl.program_id(1) == 0

    @pl.when(first)
    def _():
        tail_scr[...] = jnp.zeros(tail_scr.shape, F32)

    x = x_ref[0]
    hb = (x * lax.rsqrt(jnp.mean(x * x, axis=-1, keepdims=True) + EPS) * g_ref[...]).astype(BF16)

    def conv(ext_scr, pre, col0):
        ext_scr[0:SUBLANES, :] = tail_scr[:, col0:col0 + FFN_COLS]
        ext_scr[SUBLANES:SUBLANES + tt, :] = pre
        y = cb_ref[:, col0:col0 + FFN_COLS]
        for j in range(FFN_CONV):
            off = SUBLANES - (FFN_CONV - 1) + j
            y = y + ext_scr[off:off + tt, :] * cw_ref[j:j + 1, col0:col0 + FFN_COLS]
        tail_scr[:, col0:col0 + FFN_COLS] = ext_scr[tt:tt + SUBLANES, :]
        return y

    for c in range(D_FF // FFN_COLS):
        c0 = c * FFN_COLS
        gate = conv(extg_scr, jnp.dot(hb, wup_ref[:, c0:c0 + FFN_COLS], preferred_element_type=F32), c0)
        val = conv(extv_scr, jnp.dot(hb, wup_ref[:, D_FF + c0:D_FF + c0 + FFN_COLS],
                                     preferred_element_type=F32), D_FF + c0)
        act_scr[:, c0:c0 + FFN_COLS] = (_silu(gate) * val).astype(BF16)

    y = x + jnp.dot(act_scr[...], wdn_ref[...], preferred_element_type=F32)
    if final:
        y = y * lax.rsqrt(jnp.mean(y * y, axis=-1, keepdims=True) + EPS) * fg_ref[...]
    o_ref[0] = y


def _conv_ffn(x3, g, w_up, conv_w, conv_b, w_down, final_g, final):
    b, t, _ = x3.shape
    tt = min(TT_FFN, t)
    full2 = lambda i, j: (0, 0)
    return pl.pallas_call(
        functools.partial(_ffn_kernel, tt=tt, final=final),
        grid=(b, t // tt),
        in_specs=[
            pl.BlockSpec((1, tt, D_MODEL), lambda i, j: (i, j, 0)),
            pl.BlockSpec((1, D_MODEL), full2),
            pl.BlockSpec((D_MODEL, 2 * D_FF), full2, pipeline_mode=pl.Buffered(1)),
            pl.BlockSpec((FFN_CONV, 2 * D_FF), full2),
            pl.BlockSpec((1, 2 * D_FF), full2),
            pl.BlockSpec((D_FF, D_MODEL), full2, pipeline_mode=pl.Buffered(1)),
            pl.BlockSpec((1, D_MODEL), full2),
        ],
        out_specs=pl.BlockSpec((1, tt, D_MODEL), lambda i, j: (i, j, 0)),
        out_shape=jax.ShapeDtypeStruct((b, t, D_MODEL), F32),
        scratch_shapes=[
            pltpu.VMEM((SUBLANES, 2 * D_FF), F32),
            pltpu.VMEM((tt + SUBLANES, FFN_COLS), F32),
            pltpu.VMEM((tt + SUBLANES, FFN_COLS), F32),
            pltpu.VMEM((tt, D_FF), BF16),
        ],
        compiler_params=_params(("parallel", "arbitrary")),
        name="conv_ffn",
    )(x3, g, w_up, conv_w, conv_b, w_down, final_g)


def _retention_tables(chunk):
    log_gamma = np.log(1.0 - 2.0 ** (-5.0 - np.arange(RET_HEADS, dtype=np.float64)))
    idx = np.arange(chunk, dtype=np.float64)
    dist = idx[:, None] - idx[None, :]
    causal = dist >= 0
    dmask = np.where(causal, np.exp(np.where(causal, dist, 0.0) * log_gamma[:, None, None]), 0.0)
    xi = np.zeros((chunk, LANES), np.float64)
    zeta = np.zeros((chunk, LANES), np.float64)
    xi[:, :RET_HEADS] = np.exp((idx[:, None] + 1.0) * log_gamma[None, :])
    zeta[:, :RET_HEADS] = np.exp((chunk - 1.0 - idx[:, None]) * log_gamma[None, :])
    gch = np.zeros((SUBLANES, LANES), np.float64)
    gch[:RET_HEADS, :] = np.exp(chunk * log_gamma)[:, None]
    return tuple(jnp.asarray(t, F32) for t in (dmask, xi, zeta, gch))


def _rotary_inv_row():
    half = RET_DK // 2
    inv = (ROPE_BASE ** (-np.arange(half, dtype=np.float32) / half)).astype(np.float32)
    return jnp.asarray(np.tile(inv, LANES // half)[None, :], F32)


def _split_w_in(w_in):
    sizes = (DN_QKV, DN_HEADS, DN_HEADS, DN_VW, RET_QK, RET_QK, RET_VW, RET_VW, LRU_WIDTH, LRU_WIDTH,
             N_BRANCH * D_MODEL)
    offs = np.concatenate([[0], np.cumsum(sizes)])
    part = lambda n: w_in[..., int(offs[n]):int(offs[n + 1])]
    main = jnp.concatenate([part(0), part(4), part(5), part(6), part(7), part(8), part(9), part(10), part(3)],
                           axis=-1).astype(BF16)
    ab = jnp.concatenate([part(1), part(2)], axis=-1)
    w_ab = jnp.pad(ab, ((0, 0), (0, 0), (0, LANES - 2 * DN_HEADS))).astype(BF16)
    w_abt = jnp.pad(jnp.swapaxes(ab, -1, -2), ((0, 0), (0, AB_ROWS - 2 * DN_HEADS), (0, 0))).astype(BF16)
    return main, w_ab, w_abt


def kernel(x, positions, attn_norm, w_in, dn_conv_w, dn_a_log, dn_dt_bias, dn_norm_w, lru_conv_w, lru_conv_b,
           lru_wa, lru_ba, lru_wx, lru_bx, lru_lambda, w_branch, w_out, ffn_norm, w_up, ffn_conv_w, ffn_conv_b,
           w_down, final_norm):
    bsz, seq, _ = x.shape
    m = bsz * seq
    depth = w_in.shape[0]
    ret_tables = _retention_tables(min(TT_MIX, seq))
    inv_row = _rotary_inv_row()
    w_main, w_ab, w_abt = _split_w_in(w_in)
    pos3 = positions[:, :, None]
    hp = jnp.stack([dn_a_log, dn_dt_bias], axis=1).astype(F32)
    hpr = jnp.pad(hp, ((0, 0), (0, SUBLANES - 2), (0, LANES - DN_HEADS)))
    hpc = jnp.pad(jnp.swapaxes(hp, 1, 2), ((0, 0), (0, AB_ROWS - DN_HEADS), (0, LANES - 2)))
    wb = w_branch.astype(BF16)
    wo = w_out.astype(BF16)
    wup = w_up.astype(BF16)
    wdn = w_down.astype(BF16)
    wa = lru_wa.astype(BF16)
    wx = lru_wx.astype(BF16)

    x2 = x.reshape(m, D_MODEL)
    for l in range(depth):
        u2, ab, abt = _in_proj(x2, attn_norm[l][None, :], w_main[l], w_ab[l], w_abt[l])
        u3 = u2.reshape(bsz, seq, N_MAIN)
        ya = _deltanet(u3, ab, abt, dn_conv_w[l], hpr[l], hpc[l], dn_norm_w[l][None, :])
        yb = _retention(u3, pos3, inv_row, ret_tables)
        yc = _rglru(u3, lru_conv_w[l], lru_conv_b[l][None, :], wa[l], lru_ba[l].reshape(1, LRU_WIDTH),
                    wx[l], lru_bx[l].reshape(1, LRU_WIDTH), lru_lambda[l][None, :])
        x2 = _merge(ya.reshape(m, DN_VW), yb.reshape(m, RET_VW), yc.reshape(m, LRU_WIDTH), u2, x2, wb[l], wo[l])
        x3 = _conv_ffn(x2.reshape(bsz, seq, D_MODEL), ffn_norm[l][None, :], wup[l], ffn_conv_w[l],
                       ffn_conv_b[l][None, :], wdn[l], final_norm[None, :], final=(l == depth - 1))
        x2 = x3.reshape(m, D_MODEL)
    return x2.reshape(bsz, seq, D_MODEL)
```

```python
import functools
import math

import numpy as np
import jax
import jax.numpy as jnp
from jax import lax
from jax.experimental import pallas as pl
from jax.experimental.pallas import tpu as pltpu

F32 = jnp.float32
BF16 = jnp.bfloat16
HIGHEST = lax.Precision.HIGHEST
F32_MIN_NORMAL = float(np.finfo(np.float32).tiny)

D_MODEL = 1024
DEPTH = 4
CHUNK = 64
EPS = 1e-6
N_BRANCH = 3
BRANCH_WIDTH = 512

DN_HEADS = 4
DN_DK = 128
DN_DV = 128
DN_CONV = 4
DN_QK = DN_HEADS * DN_DK
DN_VW = DN_HEADS * DN_DV
DN_QKV = 2 * DN_QK + DN_VW

RET_HEADS = 4
RET_DK = 64
RET_DV = 128
RET_QK = RET_HEADS * RET_DK
RET_VW = RET_HEADS * RET_DV
ROPE_BASE = 10000.0

LRU_WIDTH = 512
LRU_BLOCKS = 4
LRU_BLOCK = LRU_WIDTH // LRU_BLOCKS
LRU_CONV = 4
LRU_C = 8.0

D_FF = 2816
FFN_CONV = 3

SUBLANES = 8
LANES = 128
VMEM_LIMIT_BYTES = 56 * 1024 * 1024

RET_W = 2 * RET_QK + 2 * RET_VW
LRU_W = 2 * LRU_WIDTH
OFF_DN = 0
OFF_RET = OFF_DN + DN_QKV
OFF_LRU = OFF_RET + RET_W
OFF_GATE = OFF_LRU + LRU_W
OFF_Z = OFF_GATE + N_BRANCH * D_MODEL
N_MAIN = OFF_Z + DN_VW
AB_ROWS = 16

TM_IN = 1024
TN_IN = 1536
TT_MIX = 256
DN_BLK = 2 * CHUNK
TM_MERGE = 512
TT_FFN = 512
FFN_COLS = 256


def _params(sem):
    return pltpu.CompilerParams(dimension_semantics=sem, vmem_limit_bytes=VMEM_LIMIT_BYTES)


def _sigmoid(x):
    return 1.0 / (1.0 + jnp.exp(-x))


def _silu(x):
    return x * _sigmoid(x)


def _softplus(x):
    return jnp.maximum(x, 0.0) + jnp.log1p(jnp.exp(-jnp.abs(x)))


def _dot(a, b):
    return jnp.dot(a.astype(BF16), b.astype(BF16), preferred_element_type=F32)


def _dot_nt(a, b):
    return lax.dot_general(a.astype(BF16), b.astype(BF16), (((1,), (1,)), ((), ())),
                           preferred_element_type=F32)


def _dot_tn(a, b):
    return lax.dot_general(a.astype(BF16), b.astype(BF16), (((0,), (0,)), ((), ())),
                           preferred_element_type=F32)


def _dot_f32(a, b):
    return jnp.dot(a, b, preferred_element_type=F32, precision=HIGHEST)


def _iota2(shape, dim):
    return lax.broadcasted_iota(jnp.int32, shape, dim)


def _in_proj_kernel(x_ref, g_ref, w_ref, wab_ref, wabt_ref, u_ref, ab_ref, abt_ref, h_scr):
    @pl.when(pl.program_id(1) == 0)
    def _():
        x = x_ref[...]
        h = x * lax.rsqrt(jnp.mean(x * x, axis=-1, keepdims=True) + EPS) * g_ref[...]
        hb = h.astype(BF16)
        h_scr[...] = hb
        ab_ref[...] = jnp.dot(hb, wab_ref[...], preferred_element_type=F32)
        abt_ref[...] = lax.dot_general(wabt_ref[...], hb, (((1,), (1,)), ((), ())),
                                       preferred_element_type=F32)

    u_ref[...] = jnp.dot(h_scr[...], w_ref[...], preferred_element_type=F32).astype(BF16)


def _in_proj(x2, g, w_main, w_ab, w_abt):
    m = x2.shape[0]
    tm, tn = min(TM_IN, m), TN_IN
    return pl.pallas_call(
        _in_proj_kernel,
        grid=(m // tm, N_MAIN // tn),
        in_specs=[
            pl.BlockSpec((tm, D_MODEL), lambda i, j: (i, 0)),
            pl.BlockSpec((1, D_MODEL), lambda i, j: (0, 0)),
            pl.BlockSpec((D_MODEL, tn), lambda i, j: (0, j)),
            pl.BlockSpec((D_MODEL, LANES), lambda i, j: (0, 0)),
            pl.BlockSpec((AB_ROWS, D_MODEL), lambda i, j: (0, 0)),
        ],
        out_specs=[
            pl.BlockSpec((tm, tn), lambda i, j: (i, j)),
            pl.BlockSpec((tm, LANES), lambda i, j: (i, 0)),
            pl.BlockSpec((AB_ROWS, tm), lambda i, j: (0, i)),
        ],
        out_shape=[
            jax.ShapeDtypeStruct((m, N_MAIN), BF16),
            jax.ShapeDtypeStruct((m, LANES), F32),
            jax.ShapeDtypeStruct((AB_ROWS, m), F32),
        ],
        scratch_shapes=[pltpu.VMEM((tm, D_MODEL), BF16)],
        compiler_params=_params(("parallel", "arbitrary")),
        name="in_proj",
    )(x2, g, w_main, w_ab, w_abt)


def _causal_conv(ext_scr, x_f32, w, width, tt, first):
    @pl.when(first)
    def _():
        ext_scr[0:SUBLANES, :] = jnp.zeros((SUBLANES, ext_scr.shape[1]), F32)

    ext_scr[SUBLANES:SUBLANES + tt, :] = x_f32
    y = None
    for j in range(width):
        off = SUBLANES - (width - 1) + j
        term = ext_scr[off:off + tt, :] * w[j:j + 1, :]
        y = term if y is None else y + term
    ext_scr[0:SUBLANES, :] = ext_scr[tt:tt + SUBLANES, :]
    return y


def _causal_conv_bf16(tail_scr, x, shift_ref, w, width):
    tt = x.shape[0]
    xf = x.astype(F32)
    y = xf * w[width - 1:width, :]
    for s in range(1, width):
        y = y + jnp.dot(shift_ref[s - 1], x, preferred_element_type=F32) * w[width - 1 - s:width - s, :]
    corr = None
    for j in range(width - 1):
        off = SUBLANES - (width - 1) + j
        term = tail_scr[off:off + SUBLANES, :] * w[j:j + 1, :]
        corr = term if corr is None else corr + term
    tail_scr[0:SUBLANES, :] = xf[tt - SUBLANES:tt, :]
    return jnp.concatenate([y[0:SUBLANES, :] + corr, y[SUBLANES:, :]], axis=0)


def _shift_matrices(tt, count):
    i = np.arange(tt)
    d = i[:, None] - i[None, :]
    return jnp.asarray(np.stack([(d == s) for s in range(1, count + 1)]).astype(np.float32), BF16)


def _dn_kernel(qkv_ref, z_ref, ab_ref, abt_ref, shift_ref, convw_ref, hpr_ref, hpc_ref, normw_ref, o_ref,
               tail_scr, s_scr, *, tt):
    a = _silu(_causal_conv_bf16(tail_scr, qkv_ref[0], shift_ref, convw_ref[...], DN_CONV))

    ab = ab_ref[...]
    abt = abt_ref[...]
    g_cols = -jnp.exp(hpr_ref[0:1, :]) * _softplus(ab + hpr_ref[1:2, :])
    beta_cols = _sigmoid(ab)
    g_rows = -jnp.exp(hpc_ref[:, 0:1]) * _softplus(abt + hpc_ref[:, 1:2])

    rt = _iota2((tt, tt), 0)
    ct = _iota2((tt, tt), 1)
    same_chunk = (rt ^ ct) < CHUNK
    gc_cols = _dot_f32(jnp.where(same_chunk & (rt >= ct), 1.0, 0.0).astype(F32), g_cols)
    gc_rows = _dot_f32(g_rows, jnp.where(same_chunk & (rt <= ct), 1.0, 0.0).astype(F32))
    normw = normw_ref[...]
    z = z_ref[0].astype(F32)

    ri = _iota2((DN_BLK, DN_BLK), 0)
    ci = _iota2((DN_BLK, DN_BLK), 1)
    xr = ri ^ ci
    lower = ri > ci
    causal = (xr < CHUNK) & (ri >= ci)
    eye = jnp.where(ri == ci, 1.0, 0.0).astype(F32)

    heads = range(DN_HEADS)
    q, k, v, gcc, beta, eg, k_beta = [], [], [], [], [], [], []
    for h in heads:
        qh = a[:, h * DN_DK:(h + 1) * DN_DK]
        kh = a[:, DN_QK + h * DN_DK:DN_QK + (h + 1) * DN_DK]
        q.append(qh * lax.rsqrt(jnp.sum(qh * qh, axis=-1, keepdims=True) + EPS) * (DN_DK ** -0.5))
        k.append(kh * lax.rsqrt(jnp.sum(kh * kh, axis=-1, keepdims=True) + EPS))
        v.append(a[:, 2 * DN_QK + h * DN_DV:2 * DN_QK + (h + 1) * DN_DV])
        gcc.append(gc_cols[:, h:h + 1])
        beta.append(beta_cols[:, DN_HEADS + h:DN_HEADS + h + 1])
        eg.append(jnp.exp(gcc[h]))
        k_beta.append(k[h] * beta[h])
    rhs = [jnp.concatenate([v[h] * beta[h], k_beta[h] * eg[h]], axis=1) for h in heads]
    q_dec = [q[h] * eg[h] for h in heads]

    probs = [(h, b) for h in heads for b in range(tt // DN_BLK)]
    rows_of = lambda b: slice(b * DN_BLK, (b + 1) * DN_BLK)
    decay = [jnp.where(causal, jnp.exp(gcc[h][rows_of(b)] - gc_rows[h:h + 1, rows_of(b)]), 0.0)
             for h, b in probs]
    low = [jnp.where(lower, _dot_nt(k_beta[h][rows_of(b)], k[h][rows_of(b)]) * decay[i], 0.0)
           for i, (h, b) in enumerate(probs)]
    attn = [_dot_nt(q[h][rows_of(b)], k[h][rows_of(b)]) * decay[i] for i, (h, b) in enumerate(probs)]
    n = range(len(probs))
    d4 = [jnp.where(xr < 4, low[i], 0.0) for i in n]
    d4sq = [_dot(d4[i], d4[i]) for i in n]
    t_inv = [(eye - d4[i]) + _dot(eye - d4[i], d4sq[i]) for i in n]
    for lv in range(2, 6):
        off_t = [_dot(jnp.where((xr >> lv) == 1, low[i], 0.0), t_inv[i]) for i in n]
        t_inv = [t_inv[i] - _dot(t_inv[i], off_t[i]) for i in n]
    uw = {hb: _dot(t_inv[i], rhs[hb[0]][rows_of(hb[1])]) for i, hb in enumerate(probs)}
    attn = {hb: attn[i] for i, hb in enumerate(probs)}

    state = [s_scr[h] for h in heads]
    per_blk = DN_BLK // CHUNK
    for c in range(tt // CHUNK):
        rows = slice(c * CHUNK, (c + 1) * CHUNK)
        b = c // per_blk
        loc = slice((c % per_blk) * CHUNK, (c % per_blk + 1) * CHUNK)
        g_last = [gcc[h][(c + 1) * CHUNK - 1:(c + 1) * CHUNK, :] for h in heads]
        v_new = [uw[h, b][loc, :DN_DV] - _dot(uw[h, b][loc, DN_DV:], state[h]) for h in heads]
        out = [_dot(q_dec[h][rows], state[h]) + _dot(attn[h, b][loc, loc], v_new[h]) for h in heads]
        k_dec = [k[h][rows] * jnp.exp(g_last[h] - gcc[h][rows]) for h in heads]
        state = [state[h] * jnp.exp(g_last[h]) + _dot_tn(k_dec[h], v_new[h]) for h in heads]
        for h in heads:
            o = out[h] * lax.rsqrt(jnp.mean(out[h] * out[h], axis=-1, keepdims=True) + EPS)
            o = o * normw * _silu(z[rows, h * DN_DV:(h + 1) * DN_DV])
            o_ref[0, rows, h * DN_DV:(h + 1) * DN_DV] = o.astype(BF16)
    for h in heads:
        s_scr[h] = state[h]


def _ret_kernel(r_ref, pos_ref, inv_ref, dmask_ref, xi_ref, zeta_ref, gch_ref, o_ref, s_scr, *, tt):
    r = r_ref[0].astype(F32)
    ang = pos_ref[0].astype(F32) * inv_ref[...]
    cos = jnp.cos(ang)
    sin = jnp.sin(ang)
    lane = _iota2((tt, LANES), 1)
    first_half = (lane % RET_DK) < (RET_DK // 2)
    sin_signed = jnp.where(first_half, -sin, sin)

    def rotary(t):
        swapped = jnp.where(first_half, pltpu.roll(t, LANES - RET_DK // 2, 1),
                            pltpu.roll(t, RET_DK // 2, 1))
        return t * cos + swapped * sin_signed

    q = jnp.concatenate([rotary(r[:, i * LANES:(i + 1) * LANES]) for i in range(RET_QK // LANES)], axis=1)
    k = jnp.concatenate([rotary(r[:, RET_QK + i * LANES:RET_QK + (i + 1) * LANES])
                         for i in range(RET_QK // LANES)], axis=1) * (RET_DK ** -0.5)
    v = r[:, 2 * RET_QK:2 * RET_QK + RET_VW]
    gate = r[:, 2 * RET_QK + RET_VW:]

    heads = range(RET_HEADS)
    qh = [q[:, h * RET_DK:(h + 1) * RET_DK] for h in heads]
    kh = [k[:, h * RET_DK:(h + 1) * RET_DK] for h in heads]
    vh = [v[:, h * RET_DV:(h + 1) * RET_DV] for h in heads]
    state = [s_scr[h] for h in heads]
    scores = [_dot_nt(qh[h], kh[h]) * dmask_ref[h] for h in heads]
    cross = [_dot(qh[h], state[h]) * xi_ref[:, h:h + 1] for h in heads]
    inner = [_dot(scores[h], vh[h]) for h in heads]
    for h in heads:
        s_scr[h] = state[h] * gch_ref[h:h + 1, :] + _dot_tn(kh[h] * zeta_ref[:, h:h + 1], vh[h])
    for h in heads:
        out = cross[h] + inner[h]
        mu = jnp.mean(out, axis=-1, keepdims=True)
        cen = out - mu
        var = jnp.mean(cen * cen, axis=-1, keepdims=True)
        out = cen * lax.rsqrt(var + EPS) * _silu(gate[:, h * RET_DV:(h + 1) * RET_DV])
        o_ref[0, :, h * RET_DV:(h + 1) * RET_DV] = out.astype(BF16)


def _lru_kernel(l_ref, shift_ref, convw_ref, convb_ref, wa_ref, ba_ref, wx_ref, bx_ref, lam_ref, o_ref,
                tail_scr, h_scr, *, tt):
    l = l_ref[0]
    xc = _causal_conv_bf16(tail_scr, l[:, 0:LRU_WIDTH], shift_ref, convw_ref[...], LRU_CONV)
    xc = xc + convb_ref[...]
    rs, gs = [], []
    for n in range(LRU_BLOCKS):
        xb = xc[:, n * LRU_BLOCK:(n + 1) * LRU_BLOCK].astype(BF16)
        rs.append(jnp.dot(xb, wa_ref[n], preferred_element_type=F32))
        gs.append(jnp.dot(xb, wx_ref[n], preferred_element_type=F32))
    rgate = _sigmoid(jnp.concatenate(rs, axis=1) + ba_ref[...])
    igate = _sigmoid(jnp.concatenate(gs, axis=1) + bx_ref[...])
    log_a = (-LRU_C) * rgate * _softplus(-lam_ref[...])
    a = jnp.exp(log_a)
    one_minus = -jnp.tanh(log_a) * (a * a + 1.0)
    bv = one_minus * lax.rsqrt(jnp.maximum(one_minus, F32_MIN_NORMAL)) * (igate * xc)

    sub = _iota2((tt, LRU_WIDTH), 0) % SUBLANES
    s = 1
    while s < SUBLANES:
        keep = sub >= s
        a_prev = jnp.where(keep, pltpu.roll(a, s, 0), 1.0)
        b_prev = jnp.where(keep, pltpu.roll(bv, s, 0), 0.0)
        bv = a * b_prev + bv
        a = a * a_prev
        s *= 2
    carry = h_scr[...]
    groups = []
    for r0 in range(0, tt, SUBLANES):
        hg = a[r0:r0 + SUBLANES, :] * carry + bv[r0:r0 + SUBLANES, :]
        groups.append(hg)
        carry = hg[SUBLANES - 1:SUBLANES, :]
    hseq = jnp.concatenate(groups, axis=0)
    h_scr[...] = carry
    g = l[:, LRU_WIDTH:].astype(F32)
    gelu = 0.5 * g * (1.0 + jnp.tanh(math.sqrt(2.0 / math.pi) * (g + 0.044715 * (g * g * g))))
    o_ref[0] = (hseq * gelu).astype(BF16)


N_DN_IN, N_RET_IN, N_LRU_IN = 9, 7, 9


def _mixer_kernel(*refs, tt):
    dn_in = refs[:N_DN_IN]
    ret_in = refs[N_DN_IN:N_DN_IN + N_RET_IN]
    lru_in = refs[N_DN_IN + N_RET_IN:N_DN_IN + N_RET_IN + N_LRU_IN]
    ya_ref, yb_ref, yc_ref, dn_tail, dn_state, ret_state, lru_tail, lru_state = refs[N_DN_IN + N_RET_IN + N_LRU_IN:]

    @pl.when(pl.program_id(1) == 0)
    def _():
        for scr in (dn_tail, dn_state, ret_state, lru_tail, lru_state):
            scr[...] = jnp.zeros(scr.shape, F32)

    _ret_kernel(*ret_in, yb_ref, ret_state, tt=tt)
    _dn_kernel(*dn_in, ya_ref, dn_tail, dn_state, tt=tt)
    _lru_kernel(*lru_in, yc_ref, lru_tail, lru_state, tt=tt)


def _mixer(u3, ab, abt, pos3, shifts, inv_row, ret_tables, dn_p, lru_p):
    b, t, _ = u3.shape
    tt = min(TT_MIX, t)
    nt = t // tt
    full2 = lambda i, j: (0, 0)
    full3 = lambda i, j: (0, 0, 0)
    shift_spec = pl.BlockSpec((DN_CONV - 1, tt, tt), full3)
    dn_specs = [
        pl.BlockSpec((1, tt, DN_QKV), lambda i, j: (i, j, OFF_DN // DN_QKV)),
        pl.BlockSpec((1, tt, DN_VW), lambda i, j: (i, j, OFF_Z // DN_VW)),
        pl.BlockSpec((tt, LANES), lambda i, j: (i * nt + j, 0)),
        pl.BlockSpec((AB_ROWS, tt), lambda i, j: (0, i * nt + j)),
        shift_spec,
        pl.BlockSpec((DN_CONV, DN_QKV), full2),
        pl.BlockSpec((SUBLANES, LANES), full2),
        pl.BlockSpec((AB_ROWS, LANES), full2),
        pl.BlockSpec((1, DN_DV), full2),
    ]
    ret_specs = [
        pl.BlockSpec((1, tt, RET_W), lambda i, j: (i, j, OFF_RET // RET_W)),
        pl.BlockSpec((1, tt, 1), lambda i, j: (i, j, 0)),
        pl.BlockSpec((1, LANES), full2),
        pl.BlockSpec((RET_HEADS, tt, tt), full3),
        pl.BlockSpec((tt, LANES), full2),
        pl.BlockSpec((tt, LANES), full2),
        pl.BlockSpec((SUBLANES, LANES), full2),
    ]
    lru_specs = [
        pl.BlockSpec((1, tt, LRU_W), lambda i, j: (i, j, OFF_LRU // LRU_W)),
        shift_spec,
        pl.BlockSpec((LRU_CONV, LRU_WIDTH), full2),
        pl.BlockSpec((1, LRU_WIDTH), full2),
        pl.BlockSpec((LRU_BLOCKS, LRU_BLOCK, LRU_BLOCK), full3),
        pl.BlockSpec((1, LRU_WIDTH), full2),
        pl.BlockSpec((LRU_BLOCKS, LRU_BLOCK, LRU_BLOCK), full3),
        pl.BlockSpec((1, LRU_WIDTH), full2),
        pl.BlockSpec((1, LRU_WIDTH), full2),
    ]
    assert (len(dn_specs), len(ret_specs), len(lru_specs)) == (N_DN_IN, N_RET_IN, N_LRU_IN)
    out_spec = pl.BlockSpec((1, tt, BRANCH_WIDTH), lambda i, j: (i, j, 0))
    out_shape = jax.ShapeDtypeStruct((b, t, BRANCH_WIDTH), BF16)
    return pl.pallas_call(
        functools.partial(_mixer_kernel, tt=tt),
        grid=(b, nt),
        in_specs=dn_specs + ret_specs + lru_specs,
        out_specs=[out_spec] * N_BRANCH,
        out_shape=[out_shape] * N_BRANCH,
        scratch_shapes=[
            pltpu.VMEM((2 * SUBLANES, DN_QKV), F32),
            pltpu.VMEM((DN_HEADS, DN_DK, DN_DV), F32),
            pltpu.VMEM((RET_HEADS, RET_DK, RET_DV), F32),
            pltpu.VMEM((2 * SUBLANES, LRU_WIDTH), F32),
            pltpu.VMEM((1, LRU_WIDTH), F32),
        ],
        compiler_params=_params(("parallel", "arbitrary")),
        name="mixer",
    )(u3, u3, ab, abt, shifts, *dn_p, u3, pos3, inv_row, *ret_tables, u3, shifts, *lru_p)


def _merge_kernel(ya_ref, yb_ref, yc_ref, g0_ref, g1_ref, g2_ref, x_ref, wb_ref, wo_ref, o_ref):
    merged = _sigmoid(g0_ref[...].astype(F32)) * jnp.dot(ya_ref[...], wb_ref[0], preferred_element_type=F32)
    merged = merged + _sigmoid(g1_ref[...].astype(F32)) * jnp.dot(yb_ref[...], wb_ref[1],
                                                                   preferred_element_type=F32)
    merged = merged + _sigmoid(g2_ref[...].astype(F32)) * jnp.dot(yc_ref[...], wb_ref[2],
                                                                   preferred_element_type=F32)
    o_ref[...] = x_ref[...] + jnp.dot(merged.astype(BF16), wo_ref[...], preferred_element_type=F32)


def _merge(ya, yb, yc, u2, x2, wb, wo):
    m = x2.shape[0]
    tm = min(TM_MERGE, m)
    gate_blk = OFF_GATE // D_MODEL
    yspec = pl.BlockSpec((tm, BRANCH_WIDTH), lambda i: (i, 0))
    return pl.pallas_call(
        _merge_kernel,
        grid=(m // tm,),
        in_specs=[
            yspec, yspec, yspec,
            pl.BlockSpec((tm, D_MODEL), lambda i: (i, gate_blk)),
            pl.BlockSpec((tm, D_MODEL), lambda i: (i, gate_blk + 1)),
            pl.BlockSpec((tm, D_MODEL), lambda i: (i, gate_blk + 2)),
            pl.BlockSpec((tm, D_MODEL), lambda i: (i, 0)),
            pl.BlockSpec((N_BRANCH, BRANCH_WIDTH, D_MODEL), lambda i: (0, 0, 0)),
            pl.BlockSpec((D_MODEL, D_MODEL), lambda i: (0, 0)),
        ],
        out_specs=pl.BlockSpec((tm, D_MODEL), lambda i: (i, 0)),
        out_shape=jax.ShapeDtypeStruct((m, D_MODEL), F32),
        compiler_params=_params(("parallel",)),
        name="merge",
    )(ya, yb, yc, u2, u2, u2, x2, wb, wo)


def _ffn_kernel(x_ref, g_ref, wup_ref, cw_ref, cb_ref, wdn_ref, fg_ref, o_ref,
                tail_scr, extg_scr, extv_scr, act_scr, *, tt, final):
    first = pl.program_id(1) == 0

    @pl.when(first)
    def _():
        tail_scr[...] = jnp.zeros(tail_scr.shape, F32)

    x = x_ref[0]
    hb = (x * lax.rsqrt(jnp.mean(x * x, axis=-1, keepdims=True) + EPS) * g_ref[...]).astype(BF16)

    def conv(ext_scr, pre, col0):
        ext_scr[0:SUBLANES, :] = tail_scr[:, col0:col0 + FFN_COLS]
        ext_scr[SUBLANES:SUBLANES + tt, :] = pre
        y = cb_ref[:, col0:col0 + FFN_COLS]
        for j in range(FFN_CONV):
            off = SUBLANES - (FFN_CONV - 1) + j
            y = y + ext_scr[off:off + tt, :] * cw_ref[j:j + 1, col0:col0 + FFN_COLS]
        tail_scr[:, col0:col0 + FFN_COLS] = ext_scr[tt:tt + SUBLANES, :]
        return y

    for c in range(D_FF // FFN_COLS):
        c0 = c * FFN_COLS
        gate = conv(extg_scr, jnp.dot(hb, wup_ref[:, c0:c0 + FFN_COLS], preferred_element_type=F32), c0)
        val = conv(extv_scr, jnp.dot(hb, wup_ref[:, D_FF + c0:D_FF + c0 + FFN_COLS],
                                     preferred_element_type=F32), D_FF + c0)
        act_scr[:, c0:c0 + FFN_COLS] = (_silu(gate) * val).astype(BF16)

    y = x + jnp.dot(act_scr[...], wdn_ref[...], preferred_element_type=F32)
    if final:
        y = y * lax.rsqrt(jnp.mean(y * y, axis=-1, keepdims=True) + EPS) * fg_ref[...]
    o_ref[0] = y


def _conv_ffn(x3, g, w_up, conv_w, conv_b, w_down, final_g, final):
    b, t, _ = x3.shape
    tt = min(TT_FFN, t)
    full2 = lambda i, j: (0, 0)
    return pl.pallas_call(
        functools.partial(_ffn_kernel, tt=tt, final=final),
        grid=(b, t // tt),
        in_specs=[
            pl.BlockSpec((1, tt, D_MODEL), lambda i, j: (i, j, 0)),
            pl.BlockSpec((1, D_MODEL), full2),
            pl.BlockSpec((D_MODEL, 2 * D_FF), full2, pipeline_mode=pl.Buffered(1)),
            pl.BlockSpec((FFN_CONV, 2 * D_FF), full2),
            pl.BlockSpec((1, 2 * D_FF), full2),
            pl.BlockSpec((D_FF, D_MODEL), full2, pipeline_mode=pl.Buffered(1)),
            pl.BlockSpec((1, D_MODEL), full2),
        ],
        out_specs=pl.BlockSpec((1, tt, D_MODEL), lambda i, j: (i, j, 0)),
        out_shape=jax.ShapeDtypeStruct((b, t, D_MODEL), F32),
        scratch_shapes=[
            pltpu.VMEM((SUBLANES, 2 * D_FF), F32),
            pltpu.VMEM((tt + SUBLANES, FFN_COLS), F32),
            pltpu.VMEM((tt + SUBLANES, FFN_COLS), F32),
            pltpu.VMEM((tt, D_FF), BF16),
        ],
        compiler_params=_params(("parallel", "arbitrary")),
        name="conv_ffn",
    )(x3, g, w_up, conv_w, conv_b, w_down, final_g)


def _retention_tables(chunk):
    log_gamma = np.log(1.0 - 2.0 ** (-5.0 - np.arange(RET_HEADS, dtype=np.float64)))
    idx = np.arange(chunk, dtype=np.float64)
    dist = idx[:, None] - idx[None, :]
    causal = dist >= 0
    dmask = np.where(causal, np.exp(np.where(causal, dist, 0.0) * log_gamma[:, None, None]), 0.0)
    xi = np.zeros((chunk, LANES), np.float64)
    zeta = np.zeros((chunk, LANES), np.float64)
    xi[:, :RET_HEADS] = np.exp((idx[:, None] + 1.0) * log_gamma[None, :])
    zeta[:, :RET_HEADS] = np.exp((chunk - 1.0 - idx[:, None]) * log_gamma[None, :])
    gch = np.zeros((SUBLANES, LANES), np.float64)
    gch[:RET_HEADS, :] = np.exp(chunk * log_gamma)[:, None]
    return tuple(jnp.asarray(t, F32) for t in (dmask, xi, zeta, gch))


def _rotary_inv_row():
    half = RET_DK // 2
    inv = (ROPE_BASE ** (-np.arange(half, dtype=np.float32) / half)).astype(np.float32)
    return jnp.asarray(np.tile(inv, LANES // half)[None, :], F32)


def _split_w_in(w_in):
    sizes = (DN_QKV, DN_HEADS, DN_HEADS, DN_VW, RET_QK, RET_QK, RET_VW, RET_VW, LRU_WIDTH, LRU_WIDTH,
             N_BRANCH * D_MODEL)
    offs = np.concatenate([[0], np.cumsum(sizes)])
    part = lambda n: w_in[..., int(offs[n]):int(offs[n + 1])]
    main = jnp.concatenate([part(0), part(4), part(5), part(6), part(7), part(8), part(9), part(10), part(3)],
                           axis=-1).astype(BF16)
    ab = jnp.concatenate([part(1), part(2)], axis=-1)
    w_ab = jnp.pad(ab, ((0, 0), (0, 0), (0, LANES - 2 * DN_HEADS))).astype(BF16)
    w_abt = jnp.pad(jnp.swapaxes(ab, -1, -2), ((0, 0), (0, AB_ROWS - 2 * DN_HEADS), (0, 0))).astype(BF16)
    return main, w_ab, w_abt


def kernel(x, positions, attn_norm, w_in, dn_conv_w, dn_a_log, dn_dt_bias, dn_norm_w, lru_conv_w, lru_conv_b,
           lru_wa, lru_ba, lru_wx, lru_bx, lru_lambda, w_branch, w_out, ffn_norm, w_up, ffn_conv_w, ffn_conv_b,
           w_down, final_norm):
    bsz, seq, _ = x.shape
    m = bsz * seq
    depth = w_in.shape[0]
    ret_tables = _retention_tables(min(TT_MIX, seq))
    inv_row = _rotary_inv_row()
    assert DN_CONV == LRU_CONV
    shifts = _shift_matrices(min(TT_MIX, seq), DN_CONV - 1)
    w_main, w_ab, w_abt = _split_w_in(w_in)
    pos3 = positions[:, :, None]
    hp = jnp.stack([dn_a_log, dn_dt_bias], axis=1).astype(F32)
    hpr = jnp.pad(hp, ((0, 0), (0, SUBLANES - 2), (0, LANES - DN_HEADS)))
    hpc = jnp.pad(jnp.swapaxes(hp, 1, 2), ((0, 0), (0, AB_ROWS - DN_HEADS), (0, LANES - 2)))
    wb = w_branch.astype(BF16)
    wo = w_out.astype(BF16)
    wup = w_up.astype(BF16)
    wdn = w_down.astype(BF16)
    wa = lru_wa.astype(BF16)
    wx = lru_wx.astype(BF16)

    x2 = x.reshape(m, D_MODEL)
    for l in range(depth):
        u2, ab, abt = _in_proj(x2, attn_norm[l][None, :], w_main[l], w_ab[l], w_abt[l])
        u3 = u2.reshape(bsz, seq, N_MAIN)
        dn_p = (dn_conv_w[l], hpr[l], hpc[l], dn_norm_w[l][None, :])
        lru_p = (lru_conv_w[l], lru_conv_b[l][None, :], wa[l], lru_ba[l].reshape(1, LRU_WIDTH),
                 wx[l], lru_bx[l].reshape(1, LRU_WIDTH), lru_lambda[l][None, :])
        ya, yb, yc = _mixer(u3, ab, abt, pos3, shifts, inv_row, ret_tables, dn_p, lru_p)
        x2 = _merge(ya.reshape(m, DN_VW), yb.reshape(m, RET_VW), yc.reshape(m, LRU_WIDTH), u2, x2, wb[l], wo[l])
        x3 = _conv_ffn(x2.reshape(bsz, seq, D_MODEL), ffn_norm[l][None, :], wup[l], ffn_conv_w[l],
                       ffn_conv_b[l][None, :], wdn[l], final_norm[None, :], final=(l == depth - 1))
        x2 = x3.reshape(m, D_MODEL)
    return x2.reshape(bsz, seq, D_MODEL)
```

```python
import functools
import math

import numpy as np
import jax
import jax.numpy as jnp
from jax import lax
from jax.experimental import pallas as pl
from jax.experimental.pallas import tpu as pltpu

F32 = jnp.float32
BF16 = jnp.bfloat16
HIGHEST = lax.Precision.HIGHEST
F32_MIN_NORMAL = float(np.finfo(np.float32).tiny)

D_MODEL = 1024
DEPTH = 4
CHUNK = 64
EPS = 1e-6
N_BRANCH = 3
BRANCH_WIDTH = 512

DN_HEADS = 4
DN_DK = 128
DN_DV = 128
DN_CONV = 4
DN_QK = DN_HEADS * DN_DK
DN_VW = DN_HEADS * DN_DV
DN_QKV = 2 * DN_QK + DN_VW

RET_HEADS = 4
RET_DK = 64
RET_DV = 128
RET_QK = RET_HEADS * RET_DK
RET_VW = RET_HEADS * RET_DV
ROPE_BASE = 10000.0

LRU_WIDTH = 512
LRU_BLOCKS = 4
LRU_BLOCK = LRU_WIDTH // LRU_BLOCKS
LRU_CONV = 4
LRU_C = 8.0

D_FF = 2816
FFN_CONV = 3

SUBLANES = 8
LANES = 128
VMEM_LIMIT_BYTES = 56 * 1024 * 1024

RET_W = 2 * RET_QK + 2 * RET_VW
LRU_W = 2 * LRU_WIDTH
OFF_DN = 0
OFF_RET = OFF_DN + DN_QKV
OFF_LRU = OFF_RET + RET_W
OFF_GATE = OFF_LRU + LRU_W
OFF_Z = OFF_GATE + N_BRANCH * D_MODEL
N_MAIN = OFF_Z + DN_VW
AB_ROWS = 16

TM_IN = 512
TN_IN = 1536
TT_MIX = 256
DN_BLK = 2 * CHUNK
TM_MERGE = 512
TT_FFN = 512
FFN_COLS = 256


def _params(sem):
    return pltpu.CompilerParams(dimension_semantics=sem, vmem_limit_bytes=VMEM_LIMIT_BYTES)


def _sigmoid(x):
    return 1.0 / (1.0 + jnp.exp(-x))


def _silu(x):
    return x * _sigmoid(x)


def _softplus(x):
    return jnp.maximum(x, 0.0) + jnp.log1p(jnp.exp(-jnp.abs(x)))


def _dot(a, b):
    return jnp.dot(a.astype(BF16), b.astype(BF16), preferred_element_type=F32)


def _dot_nt(a, b):
    return lax.dot_general(a.astype(BF16), b.astype(BF16), (((1,), (1,)), ((), ())),
                           preferred_element_type=F32)


def _dot_tn(a, b):
    return lax.dot_general(a.astype(BF16), b.astype(BF16), (((0,), (0,)), ((), ())),
                           preferred_element_type=F32)


def _dot_f32(a, b):
    return jnp.dot(a, b, preferred_element_type=F32, precision=HIGHEST)


def _iota2(shape, dim):
    return lax.broadcasted_iota(jnp.int32, shape, dim)


def _in_proj_kernel(x_ref, g_ref, w_ref, wab_ref, wabt_ref, u_ref, ab_ref, abt_ref):
    x = x_ref[...]
    hb = (x * lax.rsqrt(jnp.mean(x * x, axis=-1, keepdims=True) + EPS) * g_ref[...]).astype(BF16)
    for c0 in range(0, N_MAIN, TN_IN):
        u_ref[:, c0:c0 + TN_IN] = jnp.dot(hb, w_ref[:, c0:c0 + TN_IN],
                                          preferred_element_type=F32).astype(BF16)
    ab_ref[...] = jnp.dot(hb, wab_ref[...], preferred_element_type=F32)
    abt_ref[...] = lax.dot_general(wabt_ref[...], hb, (((1,), (1,)), ((), ())), preferred_element_type=F32)


def _in_proj(x2, g, w_main, w_ab, w_abt, layer):
    m = x2.shape[0]
    tm = min(TM_IN, m)
    return pl.pallas_call(
        _in_proj_kernel,
        grid=(m // tm,),
        in_specs=[
            pl.BlockSpec((tm, D_MODEL), lambda i: (i, 0)),
            pl.BlockSpec((1, D_MODEL), lambda i: (0, 0)),
            pl.BlockSpec((None, D_MODEL, N_MAIN), lambda i: (layer, 0, 0), pipeline_mode=pl.Buffered(1)),
            pl.BlockSpec((None, D_MODEL, LANES), lambda i: (layer, 0, 0)),
            pl.BlockSpec((None, AB_ROWS, D_MODEL), lambda i: (layer, 0, 0)),
        ],
        out_specs=[
            pl.BlockSpec((tm, N_MAIN), lambda i: (i, 0)),
            pl.BlockSpec((tm, LANES), lambda i: (i, 0)),
            pl.BlockSpec((AB_ROWS, tm), lambda i: (0, i)),
        ],
        out_shape=[
            jax.ShapeDtypeStruct((m, N_MAIN), BF16),
            jax.ShapeDtypeStruct((m, LANES), F32),
            jax.ShapeDtypeStruct((AB_ROWS, m), F32),
        ],
        compiler_params=_params(("parallel",)),
        name="in_proj",
    )(x2, g, w_main, w_ab, w_abt)


def _causal_conv(ext_scr, x_f32, w, width, tt, first):
    @pl.when(first)
    def _():
        ext_scr[0:SUBLANES, :] = jnp.zeros((SUBLANES, ext_scr.shape[1]), F32)

    ext_scr[SUBLANES:SUBLANES + tt, :] = x_f32
    y = None
    for j in range(width):
        off = SUBLANES - (width - 1) + j
        term = ext_scr[off:off + tt, :] * w[j:j + 1, :]
        y = term if y is None else y + term
    ext_scr[0:SUBLANES, :] = ext_scr[tt:tt + SUBLANES, :]
    return y


def _causal_conv_bf16(tail_scr, x, shift_ref, w, width):
    tt = x.shape[0]
    xf = x.astype(F32)
    y = xf * w[width - 1:width, :]
    for s in range(1, width):
        y = y + jnp.dot(shift_ref[s - 1], x, preferred_element_type=F32) * w[width - 1 - s:width - s, :]
    corr = None
    for j in range(width - 1):
        off = SUBLANES - (width - 1) + j
        term = tail_scr[off:off + SUBLANES, :] * w[j:j + 1, :]
        corr = term if corr is None else corr + term
    tail_scr[0:SUBLANES, :] = xf[tt - SUBLANES:tt, :]
    return jnp.concatenate([y[0:SUBLANES, :] + corr, y[SUBLANES:, :]], axis=0)


def _shift_matrices(tt, count):
    i = np.arange(tt)
    d = i[:, None] - i[None, :]
    return jnp.asarray(np.stack([(d == s) for s in range(1, count + 1)]).astype(np.float32), BF16)


def _dn_kernel(qkv_ref, z_ref, ab_ref, abt_ref, shift_ref, convw_ref, hpr_ref, hpc_ref, normw_ref, o_ref,
               tail_scr, s_scr, *, tt):
    a = _silu(_causal_conv_bf16(tail_scr, qkv_ref[0], shift_ref, convw_ref[...], DN_CONV))

    ab = ab_ref[...]
    abt = abt_ref[...]
    g_cols = -jnp.exp(hpr_ref[0:1, :]) * _softplus(ab + hpr_ref[1:2, :])
    beta_cols = _sigmoid(ab)
    g_rows = -jnp.exp(hpc_ref[:, 0:1]) * _softplus(abt + hpc_ref[:, 1:2])

    rt = _iota2((tt, tt), 0)
    ct = _iota2((tt, tt), 1)
    same_chunk = (rt ^ ct) < CHUNK
    gc_cols = _dot_f32(jnp.where(same_chunk & (rt >= ct), 1.0, 0.0).astype(F32), g_cols)
    gc_rows = _dot_f32(g_rows, jnp.where(same_chunk & (rt <= ct), 1.0, 0.0).astype(F32))
    normw = normw_ref[...]
    z = z_ref[0].astype(F32)

    ri = _iota2((DN_BLK, DN_BLK), 0)
    ci = _iota2((DN_BLK, DN_BLK), 1)
    xr = ri ^ ci
    lower = ri > ci
    causal = (xr < CHUNK) & (ri >= ci)
    eye = jnp.where(ri == ci, 1.0, 0.0).astype(F32)

    heads = range(DN_HEADS)
    q, k, v, gcc, beta, eg, k_beta = [], [], [], [], [], [], []
    for h in heads:
        qh = a[:, h * DN_DK:(h + 1) * DN_DK]
        kh = a[:, DN_QK + h * DN_DK:DN_QK + (h + 1) * DN_DK]
        q.append(qh * lax.rsqrt(jnp.sum(qh * qh, axis=-1, keepdims=True) + EPS) * (DN_DK ** -0.5))
        k.append(kh * lax.rsqrt(jnp.sum(kh * kh, axis=-1, keepdims=True) + EPS))
        v.append(a[:, 2 * DN_QK + h * DN_DV:2 * DN_QK + (h + 1) * DN_DV])
        gcc.append(gc_cols[:, h:h + 1])
        beta.append(beta_cols[:, DN_HEADS + h:DN_HEADS + h + 1])
        eg.append(jnp.exp(gcc[h]))
        k_beta.append(k[h] * beta[h])
    rhs = [jnp.concatenate([v[h] * beta[h], k_beta[h] * eg[h]], axis=1) for h in heads]
    q_dec = [q[h] * eg[h] for h in heads]

    probs = [(h, b) for h in heads for b in range(tt // DN_BLK)]
    rows_of = lambda b: slice(b * DN_BLK, (b + 1) * DN_BLK)
    decay = [jnp.where(causal, jnp.exp(gcc[h][rows_of(b)] - gc_rows[h:h + 1, rows_of(b)]), 0.0)
             for h, b in probs]
    low = [jnp.where(lower, _dot_nt(k_beta[h][rows_of(b)], k[h][rows_of(b)]) * decay[i], 0.0)
           for i, (h, b) in enumerate(probs)]
    attn = [_dot_nt(q[h][rows_of(b)], k[h][rows_of(b)]) * decay[i] for i, (h, b) in enumerate(probs)]
    n = range(len(probs))
    d4 = [jnp.where(xr < 4, low[i], 0.0) for i in n]
    d4sq = [_dot(d4[i], d4[i]) for i in n]
    t_inv = [(eye - d4[i]) + _dot(eye - d4[i], d4sq[i]) for i in n]
    for lv in range(2, 6):
        off_t = [_dot(jnp.where((xr >> lv) == 1, low[i], 0.0), t_inv[i]) for i in n]
        t_inv = [t_inv[i] - _dot(t_inv[i], off_t[i]) for i in n]
    uw = {hb: _dot(t_inv[i], rhs[hb[0]][rows_of(hb[1])]) for i, hb in enumerate(probs)}
    attn = {hb: attn[i] for i, hb in enumerate(probs)}

    state = [s_scr[h] for h in heads]
    per_blk = DN_BLK // CHUNK
    for c in range(tt // CHUNK):
        rows = slice(c * CHUNK, (c + 1) * CHUNK)
        b = c // per_blk
        loc = slice((c % per_blk) * CHUNK, (c % per_blk + 1) * CHUNK)
        g_last = [gcc[h][(c + 1) * CHUNK - 1:(c + 1) * CHUNK, :] for h in heads]
        v_new = [uw[h, b][loc, :DN_DV] - _dot(uw[h, b][loc, DN_DV:], state[h]) for h in heads]
        out = [_dot(q_dec[h][rows], state[h]) + _dot(attn[h, b][loc, loc], v_new[h]) for h in heads]
        k_dec = [k[h][rows] * jnp.exp(g_last[h] - gcc[h][rows]) for h in heads]
        state = [state[h] * jnp.exp(g_last[h]) + _dot_tn(k_dec[h], v_new[h]) for h in heads]
        for h in heads:
            o = out[h] * lax.rsqrt(jnp.mean(out[h] * out[h], axis=-1, keepdims=True) + EPS)
            o = o * normw * _silu(z[rows, h * DN_DV:(h + 1) * DN_DV])
            o_ref[0, rows, h * DN_DV:(h + 1) * DN_DV] = o.astype(BF16)
    for h in heads:
        s_scr[h] = state[h]


def _ret_kernel(r_ref, pos_ref, inv_ref, dmask_ref, xi_ref, zeta_ref, gch_ref, o_ref, s_scr, *, tt):
    r = r_ref[0].astype(F32)
    ang = pos_ref[0].astype(F32) * inv_ref[...]
    cos = jnp.cos(ang)
    sin = jnp.sin(ang)
    lane = _iota2((tt, LANES), 1)
    first_half = (lane % RET_DK) < (RET_DK // 2)
    sin_signed = jnp.where(first_half, -sin, sin)

    def rotary(t):
        swapped = jnp.where(first_half, pltpu.roll(t, LANES - RET_DK // 2, 1),
                            pltpu.roll(t, RET_DK // 2, 1))
        return t * cos + swapped * sin_signed

    q = jnp.concatenate([rotary(r[:, i * LANES:(i + 1) * LANES]) for i in range(RET_QK // LANES)], axis=1)
    k = jnp.concatenate([rotary(r[:, RET_QK + i * LANES:RET_QK + (i + 1) * LANES])
                         for i in range(RET_QK // LANES)], axis=1) * (RET_DK ** -0.5)
    v = r[:, 2 * RET_QK:2 * RET_QK + RET_VW]
    gate = r[:, 2 * RET_QK + RET_VW:]

    heads = range(RET_HEADS)
    qh = [q[:, h * RET_DK:(h + 1) * RET_DK] for h in heads]
    kh = [k[:, h * RET_DK:(h + 1) * RET_DK] for h in heads]
    vh = [v[:, h * RET_DV:(h + 1) * RET_DV] for h in heads]
    state = [s_scr[h] for h in heads]
    scores = [_dot_nt(qh[h], kh[h]) * dmask_ref[h] for h in heads]
    cross = [_dot(qh[h], state[h]) * xi_ref[:, h:h + 1] for h in heads]
    inner = [_dot(scores[h], vh[h]) for h in heads]
    for h in heads:
        s_scr[h] = state[h] * gch_ref[h:h + 1, :] + _dot_tn(kh[h] * zeta_ref[:, h:h + 1], vh[h])
    for h in heads:
        out = cross[h] + inner[h]
        mu = jnp.mean(out, axis=-1, keepdims=True)
        cen = out - mu
        var = jnp.mean(cen * cen, axis=-1, keepdims=True)
        out = cen * lax.rsqrt(var + EPS) * _silu(gate[:, h * RET_DV:(h + 1) * RET_DV])
        o_ref[0, :, h * RET_DV:(h + 1) * RET_DV] = out.astype(BF16)


def _lru_kernel(l_ref, shift_ref, convw_ref, convb_ref, wa_ref, ba_ref, wx_ref, bx_ref, lam_ref, o_ref,
                tail_scr, h_scr, *, tt):
    l = l_ref[0]
    xc = _causal_conv_bf16(tail_scr, l[:, 0:LRU_WIDTH], shift_ref, convw_ref[...], LRU_CONV)
    xc = xc + convb_ref[...]
    rs, gs = [], []
    for n in range(LRU_BLOCKS):
        xb = xc[:, n * LRU_BLOCK:(n + 1) * LRU_BLOCK].astype(BF16)
        rs.append(jnp.dot(xb, wa_ref[n], preferred_element_type=F32))
        gs.append(jnp.dot(xb, wx_ref[n], preferred_element_type=F32))
    rgate = _sigmoid(jnp.concatenate(rs, axis=1) + ba_ref[...])
    igate = _sigmoid(jnp.concatenate(gs, axis=1) + bx_ref[...])
    log_a = (-LRU_C) * rgate * _softplus(-lam_ref[...])
    a = jnp.exp(log_a)
    one_minus = -jnp.tanh(log_a) * (a * a + 1.0)
    bv = one_minus * lax.rsqrt(jnp.maximum(one_minus, F32_MIN_NORMAL)) * (igate * xc)

    sub = _iota2((tt, LRU_WIDTH), 0) % SUBLANES
    s = 1
    while s < SUBLANES:
        keep = sub >= s
        a_prev = jnp.where(keep, pltpu.roll(a, s, 0), 1.0)
        b_prev = jnp.where(keep, pltpu.roll(bv, s, 0), 0.0)
        bv = a * b_prev + bv
        a = a * a_prev
        s *= 2
    carry = h_scr[...]
    groups = []
    for r0 in range(0, tt, SUBLANES):
        hg = a[r0:r0 + SUBLANES, :] * carry + bv[r0:r0 + SUBLANES, :]
        groups.append(hg)
        carry = hg[SUBLANES - 1:SUBLANES, :]
    hseq = jnp.concatenate(groups, axis=0)
    h_scr[...] = carry
    g = l[:, LRU_WIDTH:].astype(F32)
    gelu = 0.5 * g * (1.0 + jnp.tanh(math.sqrt(2.0 / math.pi) * (g + 0.044715 * (g * g * g))))
    o_ref[0] = (hseq * gelu).astype(BF16)


N_DN_IN, N_RET_IN, N_LRU_IN = 9, 7, 9


def _mixer_kernel(*refs, tt):
    dn_in = refs[:N_DN_IN]
    ret_in = refs[N_DN_IN:N_DN_IN + N_RET_IN]
    lru_in = refs[N_DN_IN + N_RET_IN:N_DN_IN + N_RET_IN + N_LRU_IN]
    ya_ref, yb_ref, yc_ref, dn_tail, dn_state, ret_state, lru_tail, lru_state = refs[N_DN_IN + N_RET_IN + N_LRU_IN:]

    @pl.when(pl.program_id(1) == 0)
    def _():
        for scr in (dn_tail, dn_state, ret_state, lru_tail, lru_state):
            scr[...] = jnp.zeros(scr.shape, F32)

    _ret_kernel(*ret_in, yb_ref, ret_state, tt=tt)
    _dn_kernel(*dn_in, ya_ref, dn_tail, dn_state, tt=tt)
    _lru_kernel(*lru_in, yc_ref, lru_tail, lru_state, tt=tt)


def _mixer(u3, ab, abt, pos3, shifts, inv_row, ret_tables, dn_p, lru_p):
    b, t, _ = u3.shape
    tt = min(TT_MIX, t)
    nt = t // tt
    full2 = lambda i, j: (0, 0)
    full3 = lambda i, j: (0, 0, 0)
    shift_spec = pl.BlockSpec((DN_CONV - 1, tt, tt), full3)
    dn_specs = [
        pl.BlockSpec((1, tt, DN_QKV), lambda i, j: (i, j, OFF_DN // DN_QKV)),
        pl.BlockSpec((1, tt, DN_VW), lambda i, j: (i, j, OFF_Z // DN_VW)),
        pl.BlockSpec((tt, LANES), lambda i, j: (i * nt + j, 0)),
        pl.BlockSpec((AB_ROWS, tt), lambda i, j: (0, i * nt + j)),
        shift_spec,
        pl.BlockSpec((DN_CONV, DN_QKV), full2),
        pl.BlockSpec((SUBLANES, LANES), full2),
        pl.BlockSpec((AB_ROWS, LANES), full2),
        pl.BlockSpec((1, DN_DV), full2),
    ]
    ret_specs = [
        pl.BlockSpec((1, tt, RET_W), lambda i, j: (i, j, OFF_RET // RET_W)),
        pl.BlockSpec((1, tt, 1), lambda i, j: (i, j, 0)),
        pl.BlockSpec((1, LANES), full2),
        pl.BlockSpec((RET_HEADS, tt, tt), full3),
        pl.BlockSpec((tt, LANES), full2),
        pl.BlockSpec((tt, LANES), full2),
        pl.BlockSpec((SUBLANES, LANES), full2),
    ]
    lru_specs = [
        pl.BlockSpec((1, tt, LRU_W), lambda i, j: (i, j, OFF_LRU // LRU_W)),
        shift_spec,
        pl.BlockSpec((LRU_CONV, LRU_WIDTH), full2),
        pl.BlockSpec((1, LRU_WIDTH), full2),
        pl.BlockSpec((LRU_BLOCKS, LRU_BLOCK, LRU_BLOCK), full3),
        pl.BlockSpec((1, LRU_WIDTH), full2),
        pl.BlockSpec((LRU_BLOCKS, LRU_BLOCK, LRU_BLOCK), full3),
        pl.BlockSpec((1, LRU_WIDTH), full2),
        pl.BlockSpec((1, LRU_WIDTH), full2),
    ]
    assert (len(dn_specs), len(ret_specs), len(lru_specs)) == (N_DN_IN, N_RET_IN, N_LRU_IN)
    out_spec = pl.BlockSpec((1, tt, BRANCH_WIDTH), lambda i, j: (i, j, 0))
    out_shape = jax.ShapeDtypeStruct((b, t, BRANCH_WIDTH), BF16)
    return pl.pallas_call(
        functools.partial(_mixer_kernel, tt=tt),
        grid=(b, nt),
        in_specs=dn_specs + ret_specs + lru_specs,
        out_specs=[out_spec] * N_BRANCH,
        out_shape=[out_shape] * N_BRANCH,
        scratch_shapes=[
            pltpu.VMEM((2 * SUBLANES, DN_QKV), F32),
            pltpu.VMEM((DN_HEADS, DN_DK, DN_DV), F32),
            pltpu.VMEM((RET_HEADS, RET_DK, RET_DV), F32),
            pltpu.VMEM((2 * SUBLANES, LRU_WIDTH), F32),
            pltpu.VMEM((1, LRU_WIDTH), F32),
        ],
        compiler_params=_params(("parallel", "arbitrary")),
        name="mixer",
    )(u3, u3, ab, abt, shifts, *dn_p, u3, pos3, inv_row, *ret_tables, u3, shifts, *lru_p)


def _merge_kernel(ya_ref, yb_ref, yc_ref, g0_ref, g1_ref, g2_ref, x_ref, wb_ref, wo_ref, o_ref):
    merged = _sigmoid(g0_ref[...].astype(F32)) * jnp.dot(ya_ref[...], wb_ref[0], preferred_element_type=F32)
    merged = merged + _sigmoid(g1_ref[...].astype(F32)) * jnp.dot(yb_ref[...], wb_ref[1],
                                                                   preferred_element_type=F32)
    merged = merged + _sigmoid(g2_ref[...].astype(F32)) * jnp.dot(yc_ref[...], wb_ref[2],
                                                                   preferred_element_type=F32)
    o_ref[...] = x_ref[...] + jnp.dot(merged.astype(BF16), wo_ref[...], preferred_element_type=F32)


def _merge(ya, yb, yc, u2, x2, wb, wo, layer):
    m = x2.shape[0]
    tm = min(TM_MERGE, m)
    gate_blk = OFF_GATE // D_MODEL
    yspec = pl.BlockSpec((tm, BRANCH_WIDTH), lambda i: (i, 0))
    return pl.pallas_call(
        _merge_kernel,
        grid=(m // tm,),
        in_specs=[
            yspec, yspec, yspec,
            pl.BlockSpec((tm, D_MODEL), lambda i: (i, gate_blk)),
            pl.BlockSpec((tm, D_MODEL), lambda i: (i, gate_blk + 1)),
            pl.BlockSpec((tm, D_MODEL), lambda i: (i, gate_blk + 2)),
            pl.BlockSpec((tm, D_MODEL), lambda i: (i, 0)),
            pl.BlockSpec((None, N_BRANCH, BRANCH_WIDTH, D_MODEL), lambda i: (layer, 0, 0, 0)),
            pl.BlockSpec((None, D_MODEL, D_MODEL), lambda i: (layer, 0, 0)),
        ],
        out_specs=pl.BlockSpec((tm, D_MODEL), lambda i: (i, 0)),
        out_shape=jax.ShapeDtypeStruct((m, D_MODEL), F32),
        compiler_params=_params(("parallel",)),
        name="merge",
    )(ya, yb, yc, u2, u2, u2, x2, wb, wo)


def _ffn_kernel(x_ref, g_ref, wup_ref, cw_ref, cb_ref, wdn_ref, fg_ref, o_ref,
                tail_scr, act_scr, perm_scr, *, tt, final):
    @pl.when(pl.program_id(1) == 0)
    def _():
        tail_scr[...] = jnp.zeros(tail_scr.shape, F32)

    ns = tt // SUBLANES
    panels = D_MODEL // LANES
    for p in range(panels):
        perm_scr[p] = x_ref[0, :, p * LANES:(p + 1) * LANES]
    x = jnp.concatenate(
        [jnp.concatenate([perm_scr[p, pl.ds(i, ns, stride=SUBLANES), :] for p in range(panels)], axis=1)
         for i in range(SUBLANES)], axis=0)
    hb = (x * lax.rsqrt(jnp.mean(x * x, axis=-1, keepdims=True) + EPS) * g_ref[...]).astype(BF16)
    group0 = _iota2((ns, FFN_COLS), 0) == 0
    assert FFN_CONV == 3

    def conv(pre, col0):
        cols = slice(col0, col0 + FFN_COLS)
        slabs = [pre[i * ns:(i + 1) * ns, :] for i in range(SUBLANES)]
        back2 = jnp.where(group0, tail_scr[0:1, cols], pltpu.roll(slabs[SUBLANES - 2], 1, 0))
        back1 = jnp.where(group0, tail_scr[1:2, cols], pltpu.roll(slabs[SUBLANES - 1], 1, 0))
        tail_scr[0:1, cols] = slabs[SUBLANES - 2][ns - 1:ns, :]
        tail_scr[1:2, cols] = slabs[SUBLANES - 1][ns - 1:ns, :]
        ext = [back2, back1] + slabs
        w0, w1, w2, bias = cw_ref[0:1, cols], cw_ref[1:2, cols], cw_ref[2:3, cols], cb_ref[:, cols]
        return jnp.concatenate([bias + ext[i] * w0 + ext[i + 1] * w1 + ext[i + 2] * w2
                                for i in range(SUBLANES)], axis=0)

    for c in range(D_FF // FFN_COLS):
        c0 = c * FFN_COLS
        gate = conv(jnp.dot(hb, wup_ref[:, c0:c0 + FFN_COLS], preferred_element_type=F32), c0)
        val = conv(jnp.dot(hb, wup_ref[:, D_FF + c0:D_FF + c0 + FFN_COLS], preferred_element_type=F32),
                   D_FF + c0)
        act_scr[:, c0:c0 + FFN_COLS] = (_silu(gate) * val).astype(BF16)

    y = x + jnp.dot(act_scr[...], wdn_ref[...], preferred_element_type=F32)
    if final:
        y = y * lax.rsqrt(jnp.mean(y * y, axis=-1, keepdims=True) + EPS) * fg_ref[...]
    for p in range(panels):
        for i in range(SUBLANES):
            perm_scr[p, pl.ds(i, ns, stride=SUBLANES), :] = y[i * ns:(i + 1) * ns, p * LANES:(p + 1) * LANES]
    o_ref[0] = jnp.concatenate([perm_scr[p] for p in range(panels)], axis=1)


def _conv_ffn(x3, g, w_up, conv_w, conv_b, w_down, final_g, layer, final):
    b, t, _ = x3.shape
    tt = min(TT_FFN, t)
    full2 = lambda i, j: (0, 0)
    of_layer = lambda i, j: (layer, 0, 0)
    return pl.pallas_call(
        functools.partial(_ffn_kernel, tt=tt, final=final),
        grid=(b, t // tt),
        in_specs=[
            pl.BlockSpec((1, tt, D_MODEL), lambda i, j: (i, j, 0)),
            pl.BlockSpec((1, D_MODEL), full2),
            pl.BlockSpec((None, D_MODEL, 2 * D_FF), of_layer, pipeline_mode=pl.Buffered(1)),
            pl.BlockSpec((FFN_CONV, 2 * D_FF), full2),
            pl.BlockSpec((1, 2 * D_FF), full2),
            pl.BlockSpec((None, D_FF, D_MODEL), of_layer, pipeline_mode=pl.Buffered(1)),
            pl.BlockSpec((1, D_MODEL), full2),
        ],
        out_specs=pl.BlockSpec((1, tt, D_MODEL), lambda i, j: (i, j, 0)),
        out_shape=jax.ShapeDtypeStruct((b, t, D_MODEL), F32),
        scratch_shapes=[
            pltpu.VMEM((SUBLANES, 2 * D_FF), F32),
            pltpu.VMEM((tt, D_FF), BF16),
            pltpu.VMEM((D_MODEL // LANES, tt, LANES), F32),
        ],
        compiler_params=_params(("parallel", "arbitrary")),
        name="conv_ffn",
    )(x3, g, w_up, conv_w, conv_b, w_down, final_g)


def _retention_tables(chunk):
    log_gamma = np.log(1.0 - 2.0 ** (-5.0 - np.arange(RET_HEADS, dtype=np.float64)))
    idx = np.arange(chunk, dtype=np.float64)
    dist = idx[:, None] - idx[None, :]
    causal = dist >= 0
    dmask = np.where(causal, np.exp(np.where(causal, dist, 0.0) * log_gamma[:, None, None]), 0.0)
    xi = np.zeros((chunk, LANES), np.float64)
    zeta = np.zeros((chunk, LANES), np.float64)
    xi[:, :RET_HEADS] = np.exp((idx[:, None] + 1.0) * log_gamma[None, :])
    zeta[:, :RET_HEADS] = np.exp((chunk - 1.0 - idx[:, None]) * log_gamma[None, :])
    gch = np.zeros((SUBLANES, LANES), np.float64)
    gch[:RET_HEADS, :] = np.exp(chunk * log_gamma)[:, None]
    return tuple(jnp.asarray(t, F32) for t in (dmask, xi, zeta, gch))


def _rotary_inv_row():
    half = RET_DK // 2
    inv = (ROPE_BASE ** (-np.arange(half, dtype=np.float32) / half)).astype(np.float32)
    return jnp.asarray(np.tile(inv, LANES // half)[None, :], F32)


def _split_w_in(w_in):
    sizes = (DN_QKV, DN_HEADS, DN_HEADS, DN_VW, RET_QK, RET_QK, RET_VW, RET_VW, LRU_WIDTH, LRU_WIDTH,
             N_BRANCH * D_MODEL)
    offs = np.concatenate([[0], np.cumsum(sizes)])
    part = lambda n: w_in[..., int(offs[n]):int(offs[n + 1])]
    main = jnp.concatenate([part(0), part(4), part(5), part(6), part(7), part(8), part(9), part(10), part(3)],
                           axis=-1).astype(BF16)
    ab = jnp.concatenate([part(1), part(2)], axis=-1)
    w_ab = jnp.pad(ab, ((0, 0), (0, 0), (0, LANES - 2 * DN_HEADS))).astype(BF16)
    w_abt = jnp.pad(jnp.swapaxes(ab, -1, -2), ((0, 0), (0, AB_ROWS - 2 * DN_HEADS), (0, 0))).astype(BF16)
    return main, w_ab, w_abt


def kernel(x, positions, attn_norm, w_in, dn_conv_w, dn_a_log, dn_dt_bias, dn_norm_w, lru_conv_w, lru_conv_b,
           lru_wa, lru_ba, lru_wx, lru_bx, lru_lambda, w_branch, w_out, ffn_norm, w_up, ffn_conv_w, ffn_conv_b,
           w_down, final_norm):
    bsz, seq, _ = x.shape
    m = bsz * seq
    depth = w_in.shape[0]
    ret_tables = _retention_tables(min(TT_MIX, seq))
    inv_row = _rotary_inv_row()
    assert DN_CONV == LRU_CONV
    shifts = _shift_matrices(min(TT_MIX, seq), DN_CONV - 1)
    w_main, w_ab, w_abt = _split_w_in(w_in)
    pos3 = positions[:, :, None]
    hp = jnp.stack([dn_a_log, dn_dt_bias], axis=1).astype(F32)
    hpr = jnp.pad(hp, ((0, 0), (0, SUBLANES - 2), (0, LANES - DN_HEADS)))
    hpc = jnp.pad(jnp.swapaxes(hp, 1, 2), ((0, 0), (0, AB_ROWS - DN_HEADS), (0, LANES - 2)))
    wb = w_branch.astype(BF16)
    wo = w_out.astype(BF16)
    wup = w_up.astype(BF16)
    wdn = w_down.astype(BF16)
    wa = lru_wa.astype(BF16)
    wx = lru_wx.astype(BF16)

    x2 = x.reshape(m, D_MODEL)
    for l in range(depth):
        u2, ab, abt = _in_proj(x2, attn_norm[l][None, :], w_main, w_ab, w_abt, l)
        u3 = u2.reshape(bsz, seq, N_MAIN)
        dn_p = (dn_conv_w[l], hpr[l], hpc[l], dn_norm_w[l][None, :])
        lru_p = (lru_conv_w[l], lru_conv_b[l][None, :], wa[l], lru_ba[l].reshape(1, LRU_WIDTH),
                 wx[l], lru_bx[l].reshape(1, LRU_WIDTH), lru_lambda[l][None, :])
        ya, yb, yc = _mixer(u3, ab, abt, pos3, shifts, inv_row, ret_tables, dn_p, lru_p)
        x2 = _merge(ya.reshape(m, DN_VW), yb.reshape(m, RET_VW), yc.reshape(m, LRU_WIDTH), u2, x2, wb, wo, l)
        x3 = _conv_ffn(x2.reshape(bsz, seq, D_MODEL), ffn_norm[l][None, :], wup, ffn_conv_w[l],
                       ffn_conv_b[l][None, :], wdn, final_norm[None, :], l, final=(l == depth - 1))
        x2 = x3.reshape(m, D_MODEL)
    return x2.reshape(bsz, seq, D_MODEL)
```

```python
import functools
import math

import numpy as np
import jax
import jax.numpy as jnp
from jax import lax
from jax.experimental import pallas as pl
from jax.experimental.pallas import tpu as pltpu

F32 = jnp.float32
BF16 = jnp.bfloat16
HIGHEST = lax.Precision.HIGHEST
F32_MIN_NORMAL = float(np.finfo(np.float32).tiny)

D_MODEL = 1024
DEPTH = 4
CHUNK = 64
EPS = 1e-6
N_BRANCH = 3
BRANCH_WIDTH = 512

DN_HEADS = 4
DN_DK = 128
DN_DV = 128
DN_CONV = 4
DN_QK = DN_HEADS * DN_DK
DN_VW = DN_HEADS * DN_DV
DN_QKV = 2 * DN_QK + DN_VW

RET_HEADS = 4
RET_DK = 64
RET_DV = 128
RET_QK = RET_HEADS * RET_DK
RET_VW = RET_HEADS * RET_DV
ROPE_BASE = 10000.0

LRU_WIDTH = 512
LRU_BLOCKS = 4
LRU_BLOCK = LRU_WIDTH // LRU_BLOCKS
LRU_CONV = 4
LRU_C = 8.0

D_FF = 2816
FFN_CONV = 3

SUBLANES = 8
LANES = 128
VMEM_LIMIT_BYTES = 56 * 1024 * 1024

RET_W = 2 * RET_QK + 2 * RET_VW
LRU_W = 2 * LRU_WIDTH
OFF_DN = 0
OFF_RET = OFF_DN + DN_QKV
OFF_LRU = OFF_RET + RET_W
OFF_GATE = OFF_LRU + LRU_W
OFF_Z = OFF_GATE + N_BRANCH * D_MODEL
N_MAIN = OFF_Z + DN_VW
AB_ROWS = 16

TM_IN = 512
IN_COLS = 256
TT_MIX = 256
DN_BLK = 2 * CHUNK
TM_MERGE = 512
TT_FFN = 512
FFN_COLS = 256


def _params(sem):
    return pltpu.CompilerParams(dimension_semantics=sem, vmem_limit_bytes=VMEM_LIMIT_BYTES)


def _sigmoid(x):
    return 1.0 / (1.0 + jnp.exp(-x))


def _silu(x):
    return x * _sigmoid(x)


def _softplus(x):
    return jnp.maximum(x, 0.0) + jnp.log1p(jnp.exp(-jnp.abs(x)))


def _dot(a, b):
    return jnp.dot(a.astype(BF16), b.astype(BF16), preferred_element_type=F32)


def _dot_nt(a, b):
    return lax.dot_general(a.astype(BF16), b.astype(BF16), (((1,), (1,)), ((), ())),
                           preferred_element_type=F32)


def _dot_tn(a, b):
    return lax.dot_general(a.astype(BF16), b.astype(BF16), (((0,), (0,)), ((), ())),
                           preferred_element_type=F32)


def _dot_f32(a, b):
    return jnp.dot(a, b, preferred_element_type=F32, precision=HIGHEST)


def _iota2(shape, dim):
    return lax.broadcasted_iota(jnp.int32, shape, dim)


def _rows_to_slabs(scr, x):
    tt = x.shape[0]
    ns = tt // SUBLANES
    panels = x.shape[1] // LANES
    for p in range(panels):
        scr[p] = x[:, p * LANES:(p + 1) * LANES]
    return jnp.concatenate(
        [jnp.concatenate([scr[p, pl.ds(i, ns, stride=SUBLANES), :] for p in range(panels)], axis=1)
         for i in range(SUBLANES)], axis=0)


def _slabs_to_rows(scr, y):
    tt = y.shape[0]
    ns = tt // SUBLANES
    panels = y.shape[1] // LANES
    for p in range(panels):
        for i in range(SUBLANES):
            scr[p, pl.ds(i, ns, stride=SUBLANES), :] = y[i * ns:(i + 1) * ns, p * LANES:(p + 1) * LANES]
    return jnp.concatenate([scr[p] for p in range(panels)], axis=1)


def _causal_conv_slabs(tail_scr, w_ref, cols, xs, width):
    ns = xs.shape[0] // SUBLANES
    group0 = _iota2((ns, xs.shape[1]), 0) == 0
    slabs = [xs[i * ns:(i + 1) * ns, :] for i in range(SUBLANES)]
    back = [jnp.where(group0, tail_scr[i:i + 1, cols], pltpu.roll(slabs[i], 1, 0))
            for i in range(SUBLANES - (width - 1), SUBLANES)]
    for i in range(SUBLANES - (width - 1), SUBLANES):
        tail_scr[i:i + 1, cols] = slabs[i][ns - 1:ns, :]
    ext = back + slabs
    out = []
    for i in range(SUBLANES):
        y = ext[i] * w_ref[0:1, cols]
        for j in range(1, width):
            y = y + ext[i + j] * w_ref[j:j + 1, cols]
        out.append(y)
    return jnp.concatenate(out, axis=0)


def _in_proj_chunk_order():
    starts = list(range(0, N_MAIN, IN_COLS))
    heavy = [c for c in starts if c < OFF_DN + DN_QKV or OFF_LRU <= c < OFF_LRU + LRU_WIDTH]
    plain = [c for c in starts if c not in heavy]
    order, per = [], len(plain) // len(heavy)
    for i, c in enumerate(heavy):
        order.append(c)
        order.extend(plain[i * per:(i + 1) * per])
    order.extend(plain[len(heavy) * per:])
    return [(c, IN_COLS) for c in order]


def _in_proj_kernel(x_ref, g_ref, w_ref, wab_ref, wabt_ref, dncw_ref, lrucw_ref, lrucb_ref,
                    u_ref, ab_ref, abt_ref, dn_tail, lru_tail, perm_scr, *, tiles_per_seq):
    @pl.when(pl.program_id(0) % tiles_per_seq == 0)
    def _():
        dn_tail[...] = jnp.zeros(dn_tail.shape, F32)
        lru_tail[...] = jnp.zeros(lru_tail.shape, F32)

    x = x_ref[...]
    hb = (x * lax.rsqrt(jnp.mean(x * x, axis=-1, keepdims=True) + EPS) * g_ref[...]).astype(BF16)

    n_conv = 0
    for c0, width in _in_proj_chunk_order():
        p = jnp.dot(hb, w_ref[:, c0:c0 + width], preferred_element_type=F32)
        is_dn = c0 < OFF_DN + DN_QKV
        is_lru = OFF_LRU <= c0 < OFF_LRU + LRU_WIDTH
        if is_dn or is_lru:
            ps = _rows_to_slabs(perm_scr.at[2 * n_conv], p)
            if is_dn:
                cols = slice(c0 - OFF_DN, c0 - OFF_DN + IN_COLS)
                ps = _silu(_causal_conv_slabs(dn_tail, dncw_ref, cols, ps, DN_CONV))
                if c0 < OFF_DN + 2 * DN_QK:
                    scale = DN_DK ** -0.5 if c0 < OFF_DN + DN_QK else 1.0
                    heads = [ps[:, h0:h0 + DN_DK] for h0 in range(0, IN_COLS, DN_DK)]
                    ps = jnp.concatenate(
                        [t * (lax.rsqrt(jnp.sum(t * t, axis=-1, keepdims=True) + EPS) * scale) for t in heads],
                        axis=1)
            else:
                cols = slice(c0 - OFF_LRU, c0 - OFF_LRU + IN_COLS)
                ps = _causal_conv_slabs(lru_tail, lrucw_ref, cols, ps, LRU_CONV) + lrucb_ref[:, cols]
            p = _slabs_to_rows(perm_scr.at[2 * n_conv + 1], ps)
            n_conv += 1
        u_ref[:, c0:c0 + width] = p.astype(BF16)
    ab_ref[...] = jnp.dot(hb, wab_ref[...], preferred_element_type=F32)
    abt_ref[...] = lax.dot_general(wabt_ref[...], hb, (((1,), (1,)), ((), ())), preferred_element_type=F32)


def _in_proj(x2, g, w_main, w_ab, w_abt, dn_conv_w, lru_conv_w, lru_conv_b, layer, seq):
    m = x2.shape[0]
    tm = min(TM_IN, seq)
    return pl.pallas_call(
        functools.partial(_in_proj_kernel, tiles_per_seq=seq // tm),
        grid=(m // tm,),
        in_specs=[
            pl.BlockSpec((tm, D_MODEL), lambda i: (i, 0)),
            pl.BlockSpec((1, D_MODEL), lambda i: (0, 0)),
            pl.BlockSpec((None, D_MODEL, N_MAIN), lambda i: (layer, 0, 0), pipeline_mode=pl.Buffered(1)),
            pl.BlockSpec((None, D_MODEL, LANES), lambda i: (layer, 0, 0)),
            pl.BlockSpec((None, AB_ROWS, D_MODEL), lambda i: (layer, 0, 0)),
            pl.BlockSpec((DN_CONV, DN_QKV), lambda i: (0, 0)),
            pl.BlockSpec((LRU_CONV, LRU_WIDTH), lambda i: (0, 0)),
            pl.BlockSpec((1, LRU_WIDTH), lambda i: (0, 0)),
        ],
        out_specs=[
            pl.BlockSpec((tm, N_MAIN), lambda i: (i, 0)),
            pl.BlockSpec((tm, LANES), lambda i: (i, 0)),
            pl.BlockSpec((AB_ROWS, tm), lambda i: (0, i)),
        ],
        out_shape=[
            jax.ShapeDtypeStruct((m, N_MAIN), BF16),
            jax.ShapeDtypeStruct((m, LANES), F32),
            jax.ShapeDtypeStruct((AB_ROWS, m), F32),
        ],
        scratch_shapes=[
            pltpu.VMEM((SUBLANES, DN_QKV), F32),
            pltpu.VMEM((SUBLANES, LRU_WIDTH), F32),
            pltpu.VMEM((2 * (DN_QKV + LRU_WIDTH) // IN_COLS, IN_COLS // LANES, tm, LANES), F32),
        ],
        compiler_params=_params(("arbitrary",)),
        name="in_proj",
    )(x2, g, w_main, w_ab, w_abt, dn_conv_w, lru_conv_w, lru_conv_b)


def _dn_kernel(qkv_ref, z_ref, ab_ref, abt_ref, hpr_ref, hpc_ref, normw_ref, o_ref, s_scr, *, tt):
    a = qkv_ref[0].astype(F32)

    ab = ab_ref[...]
    abt = abt_ref[...]
    g_cols = -jnp.exp(hpr_ref[0:1, :]) * _softplus(ab + hpr_ref[1:2, :])
    beta_cols = _sigmoid(ab)
    g_rows = -jnp.exp(hpc_ref[:, 0:1]) * _softplus(abt + hpc_ref[:, 1:2])

    rt = _iota2((tt, tt), 0)
    ct = _iota2((tt, tt), 1)
    same_chunk = (rt ^ ct) < CHUNK
    gc_cols = _dot_f32(jnp.where(same_chunk & (rt >= ct), 1.0, 0.0).astype(F32), g_cols)
    gc_rows = _dot_f32(g_rows, jnp.where(same_chunk & (rt <= ct), 1.0, 0.0).astype(F32))
    normw = normw_ref[...]
    z = z_ref[0].astype(F32)

    ri = _iota2((DN_BLK, DN_BLK), 0)
    ci = _iota2((DN_BLK, DN_BLK), 1)
    xr = ri ^ ci
    lower = ri > ci
    causal = (xr < CHUNK) & (ri >= ci)
    eye = jnp.where(ri == ci, 1.0, 0.0).astype(F32)

    heads = range(DN_HEADS)
    q, k, v, gcc, beta, eg, k_beta = [], [], [], [], [], [], []
    for h in heads:
        q.append(a[:, h * DN_DK:(h + 1) * DN_DK])
        k.append(a[:, DN_QK + h * DN_DK:DN_QK + (h + 1) * DN_DK])
        v.append(a[:, 2 * DN_QK + h * DN_DV:2 * DN_QK + (h + 1) * DN_DV])
        gcc.append(gc_cols[:, h:h + 1])
        beta.append(beta_cols[:, DN_HEADS + h:DN_HEADS + h + 1])
        eg.append(jnp.exp(gcc[h]))
        k_beta.append(k[h] * beta[h])
    rhs = [jnp.concatenate([v[h] * beta[h], k_beta[h] * eg[h]], axis=1) for h in heads]
    q_dec = [q[h] * eg[h] for h in heads]

    probs = [(h, b) for h in heads for b in range(tt // DN_BLK)]
    rows_of = lambda b: slice(b * DN_BLK, (b + 1) * DN_BLK)
    decay = [jnp.where(causal, jnp.exp(gcc[h][rows_of(b)] - gc_rows[h:h + 1, rows_of(b)]), 0.0)
             for h, b in probs]
    low = [jnp.where(lower, _dot_nt(k_beta[h][rows_of(b)], k[h][rows_of(b)]) * decay[i], 0.0)
           for i, (h, b) in enumerate(probs)]
    attn = [_dot_nt(q[h][rows_of(b)], k[h][rows_of(b)]) * decay[i] for i, (h, b) in enumerate(probs)]
    n = range(len(probs))
    d4 = [jnp.where(xr < 4, low[i], 0.0) for i in n]
    d4sq = [_dot(d4[i], d4[i]) for i in n]
    t_inv = [(eye - d4[i]) + _dot(eye - d4[i], d4sq[i]) for i in n]
    for lv in range(2, 6):
        off_t = [_dot(jnp.where((xr >> lv) == 1, low[i], 0.0), t_inv[i]) for i in n]
        t_inv = [t_inv[i] - _dot(t_inv[i], off_t[i]) for i in n]
    uw = {hb: _dot(t_inv[i], rhs[hb[0]][rows_of(hb[1])]) for i, hb in enumerate(probs)}
    attn = {hb: attn[i] for i, hb in enumerate(probs)}

    state = [s_scr[h] for h in heads]
    per_blk = DN_BLK // CHUNK
    for c in range(tt // CHUNK):
        rows = slice(c * CHUNK, (c + 1) * CHUNK)
        b = c // per_blk
        loc = slice((c % per_blk) * CHUNK, (c % per_blk + 1) * CHUNK)
        g_last = [gcc[h][(c + 1) * CHUNK - 1:(c + 1) * CHUNK, :] for h in heads]
        v_new = [uw[h, b][loc, :DN_DV] - _dot(uw[h, b][loc, DN_DV:], state[h]) for h in heads]
        out = [_dot(q_dec[h][rows], state[h]) + _dot(attn[h, b][loc, loc], v_new[h]) for h in heads]
        k_dec = [k[h][rows] * jnp.exp(g_last[h] - gcc[h][rows]) for h in heads]
        state = [state[h] * jnp.exp(g_last[h]) + _dot_tn(k_dec[h], v_new[h]) for h in heads]
        for h in heads:
            o = out[h] * lax.rsqrt(jnp.mean(out[h] * out[h], axis=-1, keepdims=True) + EPS)
            o = o * normw * _silu(z[rows, h * DN_DV:(h + 1) * DN_DV])
            o_ref[0, rows, h * DN_DV:(h + 1) * DN_DV] = o.astype(BF16)
    for h in heads:
        s_scr[h] = state[h]


def _ret_kernel(r_ref, pos_ref, inv_ref, dmask_ref, xi_ref, zeta_ref, gch_ref, o_ref, s_scr, *, tt):
    r = r_ref[0].astype(F32)
    ang = pos_ref[0].astype(F32) * inv_ref[...]
    cos = jnp.cos(ang)
    sin = jnp.sin(ang)
    lane = _iota2((tt, LANES), 1)
    first_half = (lane % RET_DK) < (RET_DK // 2)
    sin_signed = jnp.where(first_half, -sin, sin)

    def rotary(t):
        swapped = jnp.where(first_half, pltpu.roll(t, LANES - RET_DK // 2, 1),
                            pltpu.roll(t, RET_DK // 2, 1))
        return t * cos + swapped * sin_signed

    q = jnp.concatenate([rotary(r[:, i * LANES:(i + 1) * LANES]) for i in range(RET_QK // LANES)], axis=1)
    k = jnp.concatenate([rotary(r[:, RET_QK + i * LANES:RET_QK + (i + 1) * LANES])
                         for i in range(RET_QK // LANES)], axis=1) * (RET_DK ** -0.5)
    v = r[:, 2 * RET_QK:2 * RET_QK + RET_VW]
    gate = r[:, 2 * RET_QK + RET_VW:]

    heads = range(RET_HEADS)
    qh = [q[:, h * RET_DK:(h + 1) * RET_DK] for h in heads]
    kh = [k[:, h * RET_DK:(h + 1) * RET_DK] for h in heads]
    vh = [v[:, h * RET_DV:(h + 1) * RET_DV] for h in heads]
    state = [s_scr[h] for h in heads]
    scores = [_dot_nt(qh[h], kh[h]) * dmask_ref[h] for h in heads]
    cross = [_dot(qh[h], state[h]) * xi_ref[:, h:h + 1] for h in heads]
    inner = [_dot(scores[h], vh[h]) for h in heads]
    for h in heads:
        s_scr[h] = state[h] * gch_ref[h:h + 1, :] + _dot_tn(kh[h] * zeta_ref[:, h:h + 1], vh[h])
    for h in heads:
        out = cross[h] + inner[h]
        mu = jnp.mean(out, axis=-1, keepdims=True)
        cen = out - mu
        var = jnp.mean(cen * cen, axis=-1, keepdims=True)
        out = cen * lax.rsqrt(var + EPS) * _silu(gate[:, h * RET_DV:(h + 1) * RET_DV])
        o_ref[0, :, h * RET_DV:(h + 1) * RET_DV] = out.astype(BF16)


def _lru_kernel(l_ref, wa_ref, ba_ref, wx_ref, bx_ref, lam_ref, o_ref, h_scr, *, tt):
    l = l_ref[0]
    xc = l[:, 0:LRU_WIDTH].astype(F32)
    rs, gs = [], []
    for n in range(LRU_BLOCKS):
        xb = xc[:, n * LRU_BLOCK:(n + 1) * LRU_BLOCK].astype(BF16)
        rs.append(jnp.dot(xb, wa_ref[n], preferred_element_type=F32))
        gs.append(jnp.dot(xb, wx_ref[n], preferred_element_type=F32))
    rgate = _sigmoid(jnp.concatenate(rs, axis=1) + ba_ref[...])
    igate = _sigmoid(jnp.concatenate(gs, axis=1) + bx_ref[...])
    log_a = (-LRU_C) * rgate * _softplus(-lam_ref[...])
    a = jnp.exp(log_a)
    one_minus = -jnp.tanh(log_a) * (a * a + 1.0)
    bv = one_minus * lax.rsqrt(jnp.maximum(one_minus, F32_MIN_NORMAL)) * (igate * xc)

    sub = _iota2((tt, LRU_WIDTH), 0) % SUBLANES
    s = 1
    while s < SUBLANES:
        keep = sub >= s
        a_prev = jnp.where(keep, pltpu.roll(a, s, 0), 1.0)
        b_prev = jnp.where(keep, pltpu.roll(bv, s, 0), 0.0)
        bv = a * b_prev + bv
        a = a * a_prev
        s *= 2
    carry = h_scr[...]
    groups = []
    for r0 in range(0, tt, SUBLANES):
        hg = a[r0:r0 + SUBLANES, :] * carry + bv[r0:r0 + SUBLANES, :]
        groups.append(hg)
        carry = hg[SUBLANES - 1:SUBLANES, :]
    hseq = jnp.concatenate(groups, axis=0)
    h_scr[...] = carry
    g = l[:, LRU_WIDTH:].astype(F32)
    gelu = 0.5 * g * (1.0 + jnp.tanh(math.sqrt(2.0 / math.pi) * (g + 0.044715 * (g * g * g))))
    o_ref[0] = (hseq * gelu).astype(BF16)


N_DN_IN, N_RET_IN, N_LRU_IN = 7, 7, 6


def _mixer_kernel(*refs, tt):
    dn_in = refs[:N_DN_IN]
    ret_in = refs[N_DN_IN:N_DN_IN + N_RET_IN]
    lru_in = refs[N_DN_IN + N_RET_IN:N_DN_IN + N_RET_IN + N_LRU_IN]
    ya_ref, yb_ref, yc_ref, dn_state, ret_state, lru_state = refs[N_DN_IN + N_RET_IN + N_LRU_IN:]

    @pl.when(pl.program_id(1) == 0)
    def _():
        for scr in (dn_state, ret_state, lru_state):
            scr[...] = jnp.zeros(scr.shape, F32)

    _ret_kernel(*ret_in, yb_ref, ret_state, tt=tt)
    _dn_kernel(*dn_in, ya_ref, dn_state, tt=tt)
    _lru_kernel(*lru_in, yc_ref, lru_state, tt=tt)


def _mixer(u3, ab, abt, pos3, inv_row, ret_tables, dn_p, lru_p):
    b, t, _ = u3.shape
    tt = min(TT_MIX, t)
    nt = t // tt
    full2 = lambda i, j: (0, 0)
    full3 = lambda i, j: (0, 0, 0)
    dn_specs = [
        pl.BlockSpec((1, tt, DN_QKV), lambda i, j: (i, j, OFF_DN // DN_QKV)),
        pl.BlockSpec((1, tt, DN_VW), lambda i, j: (i, j, OFF_Z // DN_VW)),
        pl.BlockSpec((tt, LANES), lambda i, j: (i * nt + j, 0)),
        pl.BlockSpec((AB_ROWS, tt), lambda i, j: (0, i * nt + j)),
        pl.BlockSpec((SUBLANES, LANES), full2),
        pl.BlockSpec((AB_ROWS, LANES), full2),
        pl.BlockSpec((1, DN_DV), full2),
    ]
    ret_specs = [
        pl.BlockSpec((1, tt, RET_W), lambda i, j: (i, j, OFF_RET // RET_W)),
        pl.BlockSpec((1, tt, 1), lambda i, j: (i, j, 0)),
        pl.BlockSpec((1, LANES), full2),
        pl.BlockSpec((RET_HEADS, tt, tt), full3),
        pl.BlockSpec((tt, LANES), full2),
        pl.BlockSpec((tt, LANES), full2),
        pl.BlockSpec((SUBLANES, LANES), full2),
    ]
    lru_specs = [
        pl.BlockSpec((1, tt, LRU_W), lambda i, j: (i, j, OFF_LRU // LRU_W)),
        pl.BlockSpec((LRU_BLOCKS, LRU_BLOCK, LRU_BLOCK), full3),
        pl.BlockSpec((1, LRU_WIDTH), full2),
        pl.BlockSpec((LRU_BLOCKS, LRU_BLOCK, LRU_BLOCK), full3),
        pl.BlockSpec((1, LRU_WIDTH), full2),
        pl.BlockSpec((1, LRU_WIDTH), full2),
    ]
    assert (len(dn_specs), len(ret_specs), len(lru_specs)) == (N_DN_IN, N_RET_IN, N_LRU_IN)
    out_spec = pl.BlockSpec((1, tt, BRANCH_WIDTH), lambda i, j: (i, j, 0))
    out_shape = jax.ShapeDtypeStruct((b, t, BRANCH_WIDTH), BF16)
    return pl.pallas_call(
        functools.partial(_mixer_kernel, tt=tt),
        grid=(b, nt),
        in_specs=dn_specs + ret_specs + lru_specs,
        out_specs=[out_spec] * N_BRANCH,
        out_shape=[out_shape] * N_BRANCH,
        scratch_shapes=[
            pltpu.VMEM((DN_HEADS, DN_DK, DN_DV), F32),
            pltpu.VMEM((RET_HEADS, RET_DK, RET_DV), F32),
            pltpu.VMEM((1, LRU_WIDTH), F32),
        ],
        compiler_params=_params(("parallel", "arbitrary")),
        name="mixer",
    )(u3, u3, ab, abt, *dn_p, u3, pos3, inv_row, *ret_tables, u3, *lru_p)


def _merge_kernel(ya_ref, yb_ref, yc_ref, g0_ref, g1_ref, g2_ref, x_ref, wb_ref, wo_ref, o_ref):
    merged = _sigmoid(g0_ref[...].astype(F32)) * jnp.dot(ya_ref[...], wb_ref[0], preferred_element_type=F32)
    merged = merged + _sigmoid(g1_ref[...].astype(F32)) * jnp.dot(yb_ref[...], wb_ref[1],
                                                                   preferred_element_type=F32)
    merged = merged + _sigmoid(g2_ref[...].astype(F32)) * jnp.dot(yc_ref[...], wb_ref[2],
                                                                   preferred_element_type=F32)
    o_ref[...] = x_ref[...] + jnp.dot(merged.astype(BF16), wo_ref[...], preferred_element_type=F32)


def _merge(ya, yb, yc, u2, x2, wb, wo, layer):
    m = x2.shape[0]
    tm = min(TM_MERGE, m)
    gate_blk = OFF_GATE // D_MODEL
    yspec = pl.BlockSpec((tm, BRANCH_WIDTH), lambda i: (i, 0))
    return pl.pallas_call(
        _merge_kernel,
        grid=(m // tm,),
        in_specs=[
            yspec, yspec, yspec,
            pl.BlockSpec((tm, D_MODEL), lambda i: (i, gate_blk)),
            pl.BlockSpec((tm, D_MODEL), lambda i: (i, gate_blk + 1)),
            pl.BlockSpec((tm, D_MODEL), lambda i: (i, gate_blk + 2)),
            pl.BlockSpec((tm, D_MODEL), lambda i: (i, 0)),
            pl.BlockSpec((None, N_BRANCH, BRANCH_WIDTH, D_MODEL), lambda i: (layer, 0, 0, 0)),
            pl.BlockSpec((None, D_MODEL, D_MODEL), lambda i: (layer, 0, 0)),
        ],
        out_specs=pl.BlockSpec((tm, D_MODEL), lambda i: (i, 0)),
        out_shape=jax.ShapeDtypeStruct((m, D_MODEL), F32),
        compiler_params=_params(("parallel",)),
        name="merge",
    )(ya, yb, yc, u2, u2, u2, x2, wb, wo)


def _ffn_kernel(x_ref, g_ref, wup_ref, cw_ref, cb_ref, wdn_ref, fg_ref, o_ref,
                tail_scr, act_scr, perm_scr, *, tt, final):
    @pl.when(pl.program_id(1) == 0)
    def _():
        tail_scr[...] = jnp.zeros(tail_scr.shape, F32)

    x = _rows_to_slabs(perm_scr, x_ref[0])
    hb = (x * lax.rsqrt(jnp.mean(x * x, axis=-1, keepdims=True) + EPS) * g_ref[...]).astype(BF16)

    def conv(pre, col0):
        cols = slice(col0, col0 + FFN_COLS)
        return _causal_conv_slabs(tail_scr, cw_ref, cols, pre, FFN_CONV) + cb_ref[:, cols]

    for c in range(D_FF // FFN_COLS):
        c0 = c * FFN_COLS
        gate = conv(jnp.dot(hb, wup_ref[:, c0:c0 + FFN_COLS], preferred_element_type=F32), c0)
        val = conv(jnp.dot(hb, wup_ref[:, D_FF + c0:D_FF + c0 + FFN_COLS], preferred_element_type=F32),
                   D_FF + c0)
        act_scr[:, c0:c0 + FFN_COLS] = (_silu(gate) * val).astype(BF16)

    y = x + jnp.dot(act_scr[...], wdn_ref[...], preferred_element_type=F32)
    if final:
        y = y * lax.rsqrt(jnp.mean(y * y, axis=-1, keepdims=True) + EPS) * fg_ref[...]
    o_ref[0] = _slabs_to_rows(perm_scr, y)


def _conv_ffn(x3, g, w_up, conv_w, conv_b, w_down, final_g, layer, final):
    b, t, _ = x3.shape
    tt = min(TT_FFN, t)
    full2 = lambda i, j: (0, 0)
    of_layer = lambda i, j: (layer, 0, 0)
    return pl.pallas_call(
        functools.partial(_ffn_kernel, tt=tt, final=final),
        grid=(b, t // tt),
        in_specs=[
            pl.BlockSpec((1, tt, D_MODEL), lambda i, j: (i, j, 0)),
            pl.BlockSpec((1, D_MODEL), full2),
            pl.BlockSpec((None, D_MODEL, 2 * D_FF), of_layer, pipeline_mode=pl.Buffered(1)),
            pl.BlockSpec((FFN_CONV, 2 * D_FF), full2),
            pl.BlockSpec((1, 2 * D_FF), full2),
            pl.BlockSpec((None, D_FF, D_MODEL), of_layer, pipeline_mode=pl.Buffered(1)),
            pl.BlockSpec((1, D_MODEL), full2),
        ],
        out_specs=pl.BlockSpec((1, tt, D_MODEL), lambda i, j: (i, j, 0)),
        out_shape=jax.ShapeDtypeStruct((b, t, D_MODEL), F32),
        scratch_shapes=[
            pltpu.VMEM((SUBLANES, 2 * D_FF), F32),
            pltpu.VMEM((tt, D_FF), BF16),
            pltpu.VMEM((D_MODEL // LANES, tt, LANES), F32),
        ],
        compiler_params=_params(("parallel", "arbitrary")),
        name="conv_ffn",
    )(x3, g, w_up, conv_w, conv_b, w_down, final_g)


def _retention_tables(chunk):
    log_gamma = np.log(1.0 - 2.0 ** (-5.0 - np.arange(RET_HEADS, dtype=np.float64)))
    idx = np.arange(chunk, dtype=np.float64)
    dist = idx[:, None] - idx[None, :]
    causal = dist >= 0
    dmask = np.where(causal, np.exp(np.where(causal, dist, 0.0) * log_gamma[:, None, None]), 0.0)
    xi = np.zeros((chunk, LANES), np.float64)
    zeta = np.zeros((chunk, LANES), np.float64)
    xi[:, :RET_HEADS] = np.exp((idx[:, None] + 1.0) * log_gamma[None, :])
    zeta[:, :RET_HEADS] = np.exp((chunk - 1.0 - idx[:, None]) * log_gamma[None, :])
    gch = np.zeros((SUBLANES, LANES), np.float64)
    gch[:RET_HEADS, :] = np.exp(chunk * log_gamma)[:, None]
    return tuple(jnp.asarray(t, F32) for t in (dmask, xi, zeta, gch))


def _rotary_inv_row():
    half = RET_DK // 2
    inv = (ROPE_BASE ** (-np.arange(half, dtype=np.float32) / half)).astype(np.float32)
    return jnp.asarray(np.tile(inv, LANES // half)[None, :], F32)


def _split_w_in(w_in):
    sizes = (DN_QKV, DN_HEADS, DN_HEADS, DN_VW, RET_QK, RET_QK, RET_VW, RET_VW, LRU_WIDTH, LRU_WIDTH,
             N_BRANCH * D_MODEL)
    offs = np.concatenate([[0], np.cumsum(sizes)])
    part = lambda n: w_in[..., int(offs[n]):int(offs[n + 1])]
    main = jnp.concatenate([part(0), part(4), part(5), part(6), part(7), part(8), part(9), part(10), part(3)],
                           axis=-1).astype(BF16)
    ab = jnp.concatenate([part(1), part(2)], axis=-1)
    w_ab = jnp.pad(ab, ((0, 0), (0, 0), (0, LANES - 2 * DN_HEADS))).astype(BF16)
    w_abt = jnp.pad(jnp.swapaxes(ab, -1, -2), ((0, 0), (0, AB_ROWS - 2 * DN_HEADS), (0, 0))).astype(BF16)
    return main, w_ab, w_abt


def kernel(x, positions, attn_norm, w_in, dn_conv_w, dn_a_log, dn_dt_bias, dn_norm_w, lru_conv_w, lru_conv_b,
           lru_wa, lru_ba, lru_wx, lru_bx, lru_lambda, w_branch, w_out, ffn_norm, w_up, ffn_conv_w, ffn_conv_b,
           w_down, final_norm):
    bsz, seq, _ = x.shape
    m = bsz * seq
    depth = w_in.shape[0]
    ret_tables = _retention_tables(min(TT_MIX, seq))
    inv_row = _rotary_inv_row()
    w_main, w_ab, w_abt = _split_w_in(w_in)
    pos3 = positions[:, :, None]
    hp = jnp.stack([dn_a_log, dn_dt_bias], axis=1).astype(F32)
    hpr = jnp.pad(hp, ((0, 0), (0, SUBLANES - 2), (0, LANES - DN_HEADS)))
    hpc = jnp.pad(jnp.swapaxes(hp, 1, 2), ((0, 0), (0, AB_ROWS - DN_HEADS), (0, LANES - 2)))
    wb = w_branch.astype(BF16)
    wo = w_out.astype(BF16)
    wup = w_up.astype(BF16)
    wdn = w_down.astype(BF16)
    wa = lru_wa.astype(BF16)
    wx = lru_wx.astype(BF16)

    x2 = x.reshape(m, D_MODEL)
    for l in range(depth):
        u2, ab, abt = _in_proj(x2, attn_norm[l][None, :], w_main, w_ab, w_abt, dn_conv_w[l], lru_conv_w[l],
                               lru_conv_b[l][None, :], l, seq)
        u3 = u2.reshape(bsz, seq, N_MAIN)
        dn_p = (hpr[l], hpc[l], dn_norm_w[l][None, :])
        lru_p = (wa[l], lru_ba[l].reshape(1, LRU_WIDTH), wx[l], lru_bx[l].reshape(1, LRU_WIDTH),
                 lru_lambda[l][None, :])
        ya, yb, yc = _mixer(u3, ab, abt, pos3, inv_row, ret_tables, dn_p, lru_p)
        x2 = _merge(ya.reshape(m, DN_VW), yb.reshape(m, RET_VW), yc.reshape(m, LRU_WIDTH), u2, x2, wb, wo, l)
        x3 = _conv_ffn(x2.reshape(bsz, seq, D_MODEL), ffn_norm[l][None, :], wup, ffn_conv_w[l],
                       ffn_conv_b[l][None, :], wdn, final_norm[None, :], l, final=(l == depth - 1))
        x2 = x3.reshape(m, D_MODEL)
    return x2.reshape(bsz, seq, D_MODEL)
```

```python
import functools
import math

import numpy as np
import jax
import jax.numpy as jnp
from jax import lax
from jax.experimental import pallas as pl
from jax.experimental.pallas import tpu as pltpu

F32 = jnp.float32
BF16 = jnp.bfloat16
HIGHEST = lax.Precision.HIGHEST
F32_MIN_NORMAL = float(np.finfo(np.float32).tiny)

D_MODEL = 1024
DEPTH = 4
CHUNK = 64
EPS = 1e-6
N_BRANCH = 3
BRANCH_WIDTH = 512

DN_HEADS = 4
DN_DK = 128
DN_DV = 128
DN_CONV = 4
DN_QK = DN_HEADS * DN_DK
DN_VW = DN_HEADS * DN_DV
DN_QKV = 2 * DN_QK + DN_VW

RET_HEADS = 4
RET_DK = 64
RET_DV = 128
RET_QK = RET_HEADS * RET_DK
RET_VW = RET_HEADS * RET_DV
ROPE_BASE = 10000.0

LRU_WIDTH = 512
LRU_BLOCKS = 4
LRU_BLOCK = LRU_WIDTH // LRU_BLOCKS
LRU_CONV = 4
LRU_C = 8.0

D_FF = 2816
FFN_CONV = 3

SUBLANES = 8
LANES = 128
VMEM_LIMIT_BYTES = 56 * 1024 * 1024

RET_W = 2 * RET_QK + 2 * RET_VW
LRU_W = 2 * LRU_WIDTH
OFF_DN = 0
OFF_RET = OFF_DN + DN_QKV
OFF_LRU = OFF_RET + RET_W
OFF_GATE = OFF_LRU + LRU_W
OFF_Z = OFF_GATE + N_BRANCH * D_MODEL
N_MAIN = OFF_Z + DN_VW
AB_ROWS = 16

TM_IN = 512
IN_COLS = 256
TT_MIX = 256
DN_BLK = 2 * CHUNK
TT_FFN = 512
FFN_COLS = 256


def _params(sem):
    return pltpu.CompilerParams(dimension_semantics=sem, vmem_limit_bytes=VMEM_LIMIT_BYTES)


def _sigmoid(x):
    return 1.0 / (1.0 + jnp.exp(-x))


def _silu(x):
    return x * _sigmoid(x)


def _softplus(x):
    return jnp.maximum(x, 0.0) + jnp.log1p(jnp.exp(-jnp.abs(x)))


def _dot(a, b):
    return jnp.dot(a.astype(BF16), b.astype(BF16), preferred_element_type=F32)


def _dot_nt(a, b):
    return lax.dot_general(a.astype(BF16), b.astype(BF16), (((1,), (1,)), ((), ())),
                           preferred_element_type=F32)


def _dot_tn(a, b):
    return lax.dot_general(a.astype(BF16), b.astype(BF16), (((0,), (0,)), ((), ())),
                           preferred_element_type=F32)


def _dot_f32(a, b):
    return jnp.dot(a, b, preferred_element_type=F32, precision=HIGHEST)


def _iota2(shape, dim):
    return lax.broadcasted_iota(jnp.int32, shape, dim)


def _rows_to_slabs(scr, x):
    tt = x.shape[0]
    ns = tt // SUBLANES
    panels = x.shape[1] // LANES
    for p in range(panels):
        scr[p] = x[:, p * LANES:(p + 1) * LANES]
    return jnp.concatenate(
        [jnp.concatenate([scr[p, pl.ds(i, ns, stride=SUBLANES), :] for p in range(panels)], axis=1)
         for i in range(SUBLANES)], axis=0)


def _slabs_to_rows(scr, y):
    tt = y.shape[0]
    ns = tt // SUBLANES
    panels = y.shape[1] // LANES
    for p in range(panels):
        for i in range(SUBLANES):
            scr[p, pl.ds(i, ns, stride=SUBLANES), :] = y[i * ns:(i + 1) * ns, p * LANES:(p + 1) * LANES]
    return jnp.concatenate([scr[p] for p in range(panels)], axis=1)


def _causal_conv_slabs(tail_scr, w_ref, cols, xs, width):
    ns = xs.shape[0] // SUBLANES
    group0 = _iota2((ns, xs.shape[1]), 0) == 0
    slabs = [xs[i * ns:(i + 1) * ns, :] for i in range(SUBLANES)]
    back = [jnp.where(group0, tail_scr[i:i + 1, cols], pltpu.roll(slabs[i], 1, 0))
            for i in range(SUBLANES - (width - 1), SUBLANES)]
    for i in range(SUBLANES - (width - 1), SUBLANES):
        tail_scr[i:i + 1, cols] = slabs[i][ns - 1:ns, :]
    ext = back + slabs
    out = []
    for i in range(SUBLANES):
        y = ext[i] * w_ref[0:1, cols]
        for j in range(1, width):
            y = y + ext[i + j] * w_ref[j:j + 1, cols]
        out.append(y)
    return jnp.concatenate(out, axis=0)


def _in_proj_chunk_order():
    starts = list(range(0, N_MAIN, IN_COLS))
    heavy = [c for c in starts if c < OFF_DN + DN_QKV or OFF_LRU <= c < OFF_LRU + LRU_WIDTH]
    plain = [c for c in starts if c not in heavy]
    order, per = [], len(plain) // len(heavy)
    for i, c in enumerate(heavy):
        order.append(c)
        order.extend(plain[i * per:(i + 1) * per])
    order.extend(plain[len(heavy) * per:])
    return [(c, IN_COLS) for c in order]


def _in_proj_kernel(x_ref, g_ref, w_ref, wab_ref, wabt_ref, dncw_ref, lrucw_ref, lrucb_ref,
                    u_ref, ab_ref, abt_ref, dn_tail, lru_tail, perm_scr, *, tiles_per_seq):
    @pl.when(pl.program_id(0) % tiles_per_seq == 0)
    def _():
        dn_tail[...] = jnp.zeros(dn_tail.shape, F32)
        lru_tail[...] = jnp.zeros(lru_tail.shape, F32)

    x = x_ref[...]
    hb = (x * lax.rsqrt(jnp.mean(x * x, axis=-1, keepdims=True) + EPS) * g_ref[...]).astype(BF16)

    n_conv = 0
    for c0, width in _in_proj_chunk_order():
        p = jnp.dot(hb, w_ref[:, c0:c0 + width], preferred_element_type=F32)
        is_dn = c0 < OFF_DN + DN_QKV
        is_lru = OFF_LRU <= c0 < OFF_LRU + LRU_WIDTH
        if is_dn or is_lru:
            ps = _rows_to_slabs(perm_scr.at[2 * n_conv], p)
            if is_dn:
                cols = slice(c0 - OFF_DN, c0 - OFF_DN + IN_COLS)
                ps = _silu(_causal_conv_slabs(dn_tail, dncw_ref, cols, ps, DN_CONV))
                if c0 < OFF_DN + 2 * DN_QK:
                    scale = DN_DK ** -0.5 if c0 < OFF_DN + DN_QK else 1.0
                    heads = [ps[:, h0:h0 + DN_DK] for h0 in range(0, IN_COLS, DN_DK)]
                    ps = jnp.concatenate(
                        [t * (lax.rsqrt(jnp.sum(t * t, axis=-1, keepdims=True) + EPS) * scale) for t in heads],
                        axis=1)
            else:
                cols = slice(c0 - OFF_LRU, c0 - OFF_LRU + IN_COLS)
                ps = _causal_conv_slabs(lru_tail, lrucw_ref, cols, ps, LRU_CONV) + lrucb_ref[:, cols]
            p = _slabs_to_rows(perm_scr.at[2 * n_conv + 1], ps)
            n_conv += 1
        u_ref[:, c0:c0 + width] = p.astype(BF16)
    ab_ref[...] = jnp.dot(hb, wab_ref[...], preferred_element_type=F32)
    abt_ref[...] = lax.dot_general(wabt_ref[...], hb, (((1,), (1,)), ((), ())), preferred_element_type=F32)


def _in_proj(x2, g, w_main, w_ab, w_abt, dn_conv_w, lru_conv_w, lru_conv_b, layer, seq):
    m = x2.shape[0]
    tm = min(TM_IN, seq)
    return pl.pallas_call(
        functools.partial(_in_proj_kernel, tiles_per_seq=seq // tm),
        grid=(m // tm,),
        in_specs=[
            pl.BlockSpec((tm, D_MODEL), lambda i: (i, 0)),
            pl.BlockSpec((1, D_MODEL), lambda i: (0, 0)),
            pl.BlockSpec((None, D_MODEL, N_MAIN), lambda i: (layer, 0, 0), pipeline_mode=pl.Buffered(1)),
            pl.BlockSpec((None, D_MODEL, LANES), lambda i: (layer, 0, 0)),
            pl.BlockSpec((None, AB_ROWS, D_MODEL), lambda i: (layer, 0, 0)),
            pl.BlockSpec((DN_CONV, DN_QKV), lambda i: (0, 0)),
            pl.BlockSpec((LRU_CONV, LRU_WIDTH), lambda i: (0, 0)),
            pl.BlockSpec((1, LRU_WIDTH), lambda i: (0, 0)),
        ],
        out_specs=[
            pl.BlockSpec((tm, N_MAIN), lambda i: (i, 0)),
            pl.BlockSpec((tm, LANES), lambda i: (i, 0)),
            pl.BlockSpec((AB_ROWS, tm), lambda i: (0, i)),
        ],
        out_shape=[
            jax.ShapeDtypeStruct((m, N_MAIN), BF16),
            jax.ShapeDtypeStruct((m, LANES), F32),
            jax.ShapeDtypeStruct((AB_ROWS, m), F32),
        ],
        scratch_shapes=[
            pltpu.VMEM((SUBLANES, DN_QKV), F32),
            pltpu.VMEM((SUBLANES, LRU_WIDTH), F32),
            pltpu.VMEM((2 * (DN_QKV + LRU_WIDTH) // IN_COLS, IN_COLS // LANES, tm, LANES), F32),
        ],
        compiler_params=_params(("arbitrary",)),
        name="in_proj",
    )(x2, g, w_main, w_ab, w_abt, dn_conv_w, lru_conv_w, lru_conv_b)


def _dn_kernel(qkv_ref, z_ref, ab_ref, abt_ref, hpr_ref, hpc_ref, normw_ref, o_ref, s_scr, *, tt):
    a = qkv_ref[0].astype(F32)

    ab = ab_ref[...]
    abt = abt_ref[...]
    g_cols = -jnp.exp(hpr_ref[0:1, :]) * _softplus(ab + hpr_ref[1:2, :])
    beta_cols = _sigmoid(ab)
    g_rows = -jnp.exp(hpc_ref[:, 0:1]) * _softplus(abt + hpc_ref[:, 1:2])

    rt = _iota2((tt, tt), 0)
    ct = _iota2((tt, tt), 1)
    same_chunk = (rt ^ ct) < CHUNK
    gc_cols = _dot_f32(jnp.where(same_chunk & (rt >= ct), 1.0, 0.0).astype(F32), g_cols)
    gc_rows = _dot_f32(g_rows, jnp.where(same_chunk & (rt <= ct), 1.0, 0.0).astype(F32))
    normw = normw_ref[...]
    z = z_ref[0].astype(F32)

    ri = _iota2((DN_BLK, DN_BLK), 0)
    ci = _iota2((DN_BLK, DN_BLK), 1)
    xr = ri ^ ci
    lower = ri > ci
    causal = (xr < CHUNK) & (ri >= ci)
    eye = jnp.where(ri == ci, 1.0, 0.0).astype(F32)

    heads = range(DN_HEADS)
    q, k, v, gcc, beta, eg, k_beta = [], [], [], [], [], [], []
    for h in heads:
        q.append(a[:, h * DN_DK:(h + 1) * DN_DK])
        k.append(a[:, DN_QK + h * DN_DK:DN_QK + (h + 1) * DN_DK])
        v.append(a[:, 2 * DN_QK + h * DN_DV:2 * DN_QK + (h + 1) * DN_DV])
        gcc.append(gc_cols[:, h:h + 1])
        beta.append(beta_cols[:, DN_HEADS + h:DN_HEADS + h + 1])
        eg.append(jnp.exp(gcc[h]))
        k_beta.append(k[h] * beta[h])
    rhs = [jnp.concatenate([v[h] * beta[h], k_beta[h] * eg[h]], axis=1) for h in heads]
    q_dec = [q[h] * eg[h] for h in heads]

    probs = [(h, b) for h in heads for b in range(tt // DN_BLK)]
    rows_of = lambda b: slice(b * DN_BLK, (b + 1) * DN_BLK)
    decay = [jnp.where(causal, jnp.exp(gcc[h][rows_of(b)] - gc_rows[h:h + 1, rows_of(b)]), 0.0)
             for h, b in probs]
    low = [jnp.where(lower, _dot_nt(k_beta[h][rows_of(b)], k[h][rows_of(b)]) * decay[i], 0.0)
           for i, (h, b) in enumerate(probs)]
    attn = [_dot_nt(q[h][rows_of(b)], k[h][rows_of(b)]) * decay[i] for i, (h, b) in enumerate(probs)]
    n = range(len(probs))
    d4 = [jnp.where(xr < 4, low[i], 0.0) for i in n]
    d4sq = [_dot(d4[i], d4[i]) for i in n]
    t_inv = [(eye - d4[i]) + _dot(eye - d4[i], d4sq[i]) for i in n]
    for lv in range(2, 6):
        off_t = [_dot(jnp.where((xr >> lv) == 1, low[i], 0.0), t_inv[i]) for i in n]
        t_inv = [t_inv[i] - _dot(t_inv[i], off_t[i]) for i in n]
    uw = {hb: _dot(t_inv[i], rhs[hb[0]][rows_of(hb[1])]) for i, hb in enumerate(probs)}
    attn = {hb: attn[i] for i, hb in enumerate(probs)}

    state = [s_scr[h] for h in heads]
    per_blk = DN_BLK // CHUNK
    for c in range(tt // CHUNK):
        rows = slice(c * CHUNK, (c + 1) * CHUNK)
        b = c // per_blk
        loc = slice((c % per_blk) * CHUNK, (c % per_blk + 1) * CHUNK)
        g_last = [gcc[h][(c + 1) * CHUNK - 1:(c + 1) * CHUNK, :] for h in heads]
        v_new = [uw[h, b][loc, :DN_DV] - _dot(uw[h, b][loc, DN_DV:], state[h]) for h in heads]
        out = [_dot(q_dec[h][rows], state[h]) + _dot(attn[h, b][loc, loc], v_new[h]) for h in heads]
        k_dec = [k[h][rows] * jnp.exp(g_last[h] - gcc[h][rows]) for h in heads]
        state = [state[h] * jnp.exp(g_last[h]) + _dot_tn(k_dec[h], v_new[h]) for h in heads]
        for h in heads:
            o = out[h] * lax.rsqrt(jnp.mean(out[h] * out[h], axis=-1, keepdims=True) + EPS)
            o = o * normw * _silu(z[rows, h * DN_DV:(h + 1) * DN_DV])
            o_ref[0, rows, h * DN_DV:(h + 1) * DN_DV] = o.astype(BF16)
    for h in heads:
        s_scr[h] = state[h]


def _ret_kernel(r_ref, pos_ref, inv_ref, dmask_ref, xi_ref, zeta_ref, gch_ref, o_ref, s_scr, *, tt):
    r = r_ref[0].astype(F32)
    ang = pos_ref[0].astype(F32) * inv_ref[...]
    cos = jnp.cos(ang)
    sin = jnp.sin(ang)
    lane = _iota2((tt, LANES), 1)
    first_half = (lane % RET_DK) < (RET_DK // 2)
    sin_signed = jnp.where(first_half, -sin, sin)

    def rotary(t):
        swapped = jnp.where(first_half, pltpu.roll(t, LANES - RET_DK // 2, 1),
                            pltpu.roll(t, RET_DK // 2, 1))
        return t * cos + swapped * sin_signed

    q = jnp.concatenate([rotary(r[:, i * LANES:(i + 1) * LANES]) for i in range(RET_QK // LANES)], axis=1)
    k = jnp.concatenate([rotary(r[:, RET_QK + i * LANES:RET_QK + (i + 1) * LANES])
                         for i in range(RET_QK // LANES)], axis=1) * (RET_DK ** -0.5)
    v = r[:, 2 * RET_QK:2 * RET_QK + RET_VW]
    gate = r[:, 2 * RET_QK + RET_VW:]

    heads = range(RET_HEADS)
    qh = [q[:, h * RET_DK:(h + 1) * RET_DK] for h in heads]
    kh = [k[:, h * RET_DK:(h + 1) * RET_DK] for h in heads]
    vh = [v[:, h * RET_DV:(h + 1) * RET_DV] for h in heads]
    state = [s_scr[h] for h in heads]
    scores = [_dot_nt(qh[h], kh[h]) * dmask_ref[h] for h in heads]
    cross = [_dot(qh[h], state[h]) * xi_ref[:, h:h + 1] for h in heads]
    inner = [_dot(scores[h], vh[h]) for h in heads]
    for h in heads:
        s_scr[h] = state[h] * gch_ref[h:h + 1, :] + _dot_tn(kh[h] * zeta_ref[:, h:h + 1], vh[h])
    for h in heads:
        out = cross[h] + inner[h]
        mu = jnp.mean(out, axis=-1, keepdims=True)
        cen = out - mu
        var = jnp.mean(cen * cen, axis=-1, keepdims=True)
        out = cen * lax.rsqrt(var + EPS) * _silu(gate[:, h * RET_DV:(h + 1) * RET_DV])
        o_ref[0, :, h * RET_DV:(h + 1) * RET_DV] = out.astype(BF16)


def _lru_kernel(l_ref, wa_ref, ba_ref, wx_ref, bx_ref, lam_ref, o_ref, h_scr, *, tt):
    l = l_ref[0]
    xc = l[:, 0:LRU_WIDTH].astype(F32)
    rs, gs = [], []
    for n in range(LRU_BLOCKS):
        xb = xc[:, n * LRU_BLOCK:(n + 1) * LRU_BLOCK].astype(BF16)
        rs.append(jnp.dot(xb, wa_ref[n], preferred_element_type=F32))
        gs.append(jnp.dot(xb, wx_ref[n], preferred_element_type=F32))
    rgate = _sigmoid(jnp.concatenate(rs, axis=1) + ba_ref[...])
    igate = _sigmoid(jnp.concatenate(gs, axis=1) + bx_ref[...])
    log_a = (-LRU_C) * rgate * _softplus(-lam_ref[...])
    a = jnp.exp(log_a)
    one_minus = -jnp.tanh(log_a) * (a * a + 1.0)
    bv = one_minus * lax.rsqrt(jnp.maximum(one_minus, F32_MIN_NORMAL)) * (igate * xc)

    sub = _iota2((tt, LRU_WIDTH), 0) % SUBLANES
    s = 1
    while s < SUBLANES:
        keep = sub >= s
        a_prev = jnp.where(keep, pltpu.roll(a, s, 0), 1.0)
        b_prev = jnp.where(keep, pltpu.roll(bv, s, 0), 0.0)
        bv = a * b_prev + bv
        a = a * a_prev
        s *= 2
    carry = h_scr[...]
    groups = []
    for r0 in range(0, tt, SUBLANES):
        hg = a[r0:r0 + SUBLANES, :] * carry + bv[r0:r0 + SUBLANES, :]
        groups.append(hg)
        carry = hg[SUBLANES - 1:SUBLANES, :]
    hseq = jnp.concatenate(groups, axis=0)
    h_scr[...] = carry
    g = l[:, LRU_WIDTH:].astype(F32)
    gelu = 0.5 * g * (1.0 + jnp.tanh(math.sqrt(2.0 / math.pi) * (g + 0.044715 * (g * g * g))))
    o_ref[0] = (hseq * gelu).astype(BF16)


N_DN_IN, N_RET_IN, N_LRU_IN, N_MERGE_IN = 7, 7, 6, 6


def _merge_tile(y_ref, g0_ref, g1_ref, g2_ref, x_ref, wb_ref, wo_ref, o_ref):
    merged = None
    for n, g_ref in enumerate((g0_ref, g1_ref, g2_ref)):
        term = _sigmoid(g_ref[...].astype(F32)) * jnp.dot(y_ref[n, 0], wb_ref[n], preferred_element_type=F32)
        merged = term if merged is None else merged + term
    o_ref[...] = x_ref[...] + jnp.dot(merged.astype(BF16), wo_ref[...], preferred_element_type=F32)


def _mixer_kernel(*refs, tt, tiles_per_seq):
    n_in = N_DN_IN + N_RET_IN + N_LRU_IN
    dn_in = refs[:N_DN_IN]
    ret_in = refs[N_DN_IN:N_DN_IN + N_RET_IN]
    lru_in = refs[N_DN_IN + N_RET_IN:n_in]
    merge_in = refs[n_in:n_in + N_MERGE_IN]
    o_ref, dn_state, ret_state, lru_state, y_scr = refs[n_in + N_MERGE_IN:]
    step = pl.program_id(0)

    @pl.when(step % tiles_per_seq == 0)
    def _():
        for scr in (dn_state, ret_state, lru_state):
            scr[...] = jnp.zeros(scr.shape, F32)

    @pl.when(step == 0)
    def _():
        y_scr[...] = jnp.zeros(y_scr.shape, BF16)

    _merge_tile(y_scr, *merge_in, o_ref)
    _ret_kernel(*ret_in, y_scr.at[1], ret_state, tt=tt)
    _dn_kernel(*dn_in, y_scr.at[0], dn_state, tt=tt)
    _lru_kernel(*lru_in, y_scr.at[2], lru_state, tt=tt)


def _mixer_merge(u2, ab, abt, pos, x2, inv_row, ret_tables, dn_p, lru_p, wb, wo, layer, seq):
    m = u2.shape[0]
    tt = min(TT_MIX, seq)
    n_tiles = m // tt
    u3 = u2.reshape(n_tiles, tt, N_MAIN)
    pos3 = pos.reshape(n_tiles, tt, 1)
    cur = lambda s: jnp.minimum(s, n_tiles - 1)
    prev = lambda s: jnp.maximum(s - 1, 0)
    full2 = lambda s: (0, 0)
    full3 = lambda s: (0, 0, 0)
    dn_specs = [
        pl.BlockSpec((1, tt, DN_QKV), lambda s: (cur(s), 0, OFF_DN // DN_QKV)),
        pl.BlockSpec((1, tt, DN_VW), lambda s: (cur(s), 0, OFF_Z // DN_VW)),
        pl.BlockSpec((tt, LANES), lambda s: (cur(s), 0)),
        pl.BlockSpec((AB_ROWS, tt), lambda s: (0, cur(s))),
        pl.BlockSpec((SUBLANES, LANES), full2),
        pl.BlockSpec((AB_ROWS, LANES), full2),
        pl.BlockSpec((1, DN_DV), full2),
    ]
    ret_specs = [
        pl.BlockSpec((1, tt, RET_W), lambda s: (cur(s), 0, OFF_RET // RET_W)),
        pl.BlockSpec((1, tt, 1), lambda s: (cur(s), 0, 0)),
        pl.BlockSpec((1, LANES), full2),
        pl.BlockSpec((RET_HEADS, tt, tt), full3),
        pl.BlockSpec((tt, LANES), full2),
        pl.BlockSpec((tt, LANES), full2),
        pl.BlockSpec((SUBLANES, LANES), full2),
    ]
    lru_specs = [
        pl.BlockSpec((1, tt, LRU_W), lambda s: (cur(s), 0, OFF_LRU // LRU_W)),
        pl.BlockSpec((LRU_BLOCKS, LRU_BLOCK, LRU_BLOCK), full3),
        pl.BlockSpec((1, LRU_WIDTH), full2),
        pl.BlockSpec((LRU_BLOCKS, LRU_BLOCK, LRU_BLOCK), full3),
        pl.BlockSpec((1, LRU_WIDTH), full2),
        pl.BlockSpec((1, LRU_WIDTH), full2),
    ]
    gate_blk = OFF_GATE // D_MODEL
    merge_specs = [
        pl.BlockSpec((tt, D_MODEL), lambda s: (prev(s), gate_blk)),
        pl.BlockSpec((tt, D_MODEL), lambda s: (prev(s), gate_blk + 1)),
        pl.BlockSpec((tt, D_MODEL), lambda s: (prev(s), gate_blk + 2)),
        pl.BlockSpec((tt, D_MODEL), lambda s: (prev(s), 0)),
        pl.BlockSpec((None, N_BRANCH, BRANCH_WIDTH, D_MODEL), lambda s: (layer, 0, 0, 0)),
        pl.BlockSpec((None, D_MODEL, D_MODEL), lambda s: (layer, 0, 0)),
    ]
    assert (len(dn_specs), len(ret_specs), len(lru_specs), len(merge_specs)) == (
        N_DN_IN, N_RET_IN, N_LRU_IN, N_MERGE_IN)
    return pl.pallas_call(
        functools.partial(_mixer_kernel, tt=tt, tiles_per_seq=seq // tt),
        grid=(n_tiles + 1,),
        in_specs=dn_specs + ret_specs + lru_specs + merge_specs,
        out_specs=pl.BlockSpec((tt, D_MODEL), lambda s: (prev(s), 0)),
        out_shape=jax.ShapeDtypeStruct((m, D_MODEL), F32),
        scratch_shapes=[
            pltpu.VMEM((DN_HEADS, DN_DK, DN_DV), F32),
            pltpu.VMEM((RET_HEADS, RET_DK, RET_DV), F32),
            pltpu.VMEM((1, LRU_WIDTH), F32),
            pltpu.VMEM((N_BRANCH, 1, tt, BRANCH_WIDTH), BF16),
        ],
        compiler_params=_params(("arbitrary",)),
        name="mixer",
    )(u3, u3, ab, abt, *dn_p, u3, pos3, inv_row, *ret_tables, u3, *lru_p, u2, u2, u2, x2, wb, wo)


def _ffn_kernel(x_ref, g_ref, wup_ref, cw_ref, cb_ref, wdn_ref, fg_ref, o_ref,
                tail_scr, act_scr, perm_scr, *, tt, final):
    @pl.when(pl.program_id(1) == 0)
    def _():
        tail_scr[...] = jnp.zeros(tail_scr.shape, F32)

    x = _rows_to_slabs(perm_scr, x_ref[0])
    hb = (x * lax.rsqrt(jnp.mean(x * x, axis=-1, keepdims=True) + EPS) * g_ref[...]).astype(BF16)

    def conv(pre, col0):
        cols = slice(col0, col0 + FFN_COLS)
        return _causal_conv_slabs(tail_scr, cw_ref, cols, pre, FFN_CONV) + cb_ref[:, cols]

    for c in range(D_FF // FFN_COLS):
        c0 = c * FFN_COLS
        gate = conv(jnp.dot(hb, wup_ref[:, c0:c0 + FFN_COLS], preferred_element_type=F32), c0)
        val = conv(jnp.dot(hb, wup_ref[:, D_FF + c0:D_FF + c0 + FFN_COLS], preferred_element_type=F32),
                   D_FF + c0)
        act_scr[:, c0:c0 + FFN_COLS] = (_silu(gate) * val).astype(BF16)

    y = x + jnp.dot(act_scr[...], wdn_ref[...], preferred_element_type=F32)
    if final:
        y = y * lax.rsqrt(jnp.mean(y * y, axis=-1, keepdims=True) + EPS) * fg_ref[...]
    o_ref[0] = _slabs_to_rows(perm_scr, y)


def _conv_ffn(x3, g, w_up, conv_w, conv_b, w_down, final_g, layer, final):
    b, t, _ = x3.shape
    tt = min(TT_FFN, t)
    full2 = lambda i, j: (0, 0)
    of_layer = lambda i, j: (layer, 0, 0)
    return pl.pallas_call(
        functools.partial(_ffn_kernel, tt=tt, final=final),
        grid=(b, t // tt),
        in_specs=[
            pl.BlockSpec((1, tt, D_MODEL), lambda i, j: (i, j, 0)),
            pl.BlockSpec((1, D_MODEL), full2),
            pl.BlockSpec((None, D_MODEL, 2 * D_FF), of_layer, pipeline_mode=pl.Buffered(1)),
            pl.BlockSpec((FFN_CONV, 2 * D_FF), full2),
            pl.BlockSpec((1, 2 * D_FF), full2),
            pl.BlockSpec((None, D_FF, D_MODEL), of_layer, pipeline_mode=pl.Buffered(1)),
            pl.BlockSpec((1, D_MODEL), full2),
        ],
        out_specs=pl.BlockSpec((1, tt, D_MODEL), lambda i, j: (i, j, 0)),
        out_shape=jax.ShapeDtypeStruct((b, t, D_MODEL), F32),
        scratch_shapes=[
            pltpu.VMEM((SUBLANES, 2 * D_FF), F32),
            pltpu.VMEM((tt, D_FF), BF16),
            pltpu.VMEM((D_MODEL // LANES, tt, LANES), F32),
        ],
        compiler_params=_params(("parallel", "arbitrary")),
        name="conv_ffn",
    )(x3, g, w_up, conv_w, conv_b, w_down, final_g)


def _retention_tables(chunk):
    log_gamma = np.log(1.0 - 2.0 ** (-5.0 - np.arange(RET_HEADS, dtype=np.float64)))
    idx = np.arange(chunk, dtype=np.float64)
    dist = idx[:, None] - idx[None, :]
    causal = dist >= 0
    dmask = np.where(causal, np.exp(np.where(causal, dist, 0.0) * log_gamma[:, None, None]), 0.0)
    xi = np.zeros((chunk, LANES), np.float64)
    zeta = np.zeros((chunk, LANES), np.float64)
    xi[:, :RET_HEADS] = np.exp((idx[:, None] + 1.0) * log_gamma[None, :])
    zeta[:, :RET_HEADS] = np.exp((chunk - 1.0 - idx[:, None]) * log_gamma[None, :])
    gch = np.zeros((SUBLANES, LANES), np.float64)
    gch[:RET_HEADS, :] = np.exp(chunk * log_gamma)[:, None]
    return tuple(jnp.asarray(t, F32) for t in (dmask, xi, zeta, gch))


def _rotary_inv_row():
    half = RET_DK // 2
    inv = (ROPE_BASE ** (-np.arange(half, dtype=np.float32) / half)).astype(np.float32)
    return jnp.asarray(np.tile(inv, LANES // half)[None, :], F32)


def _split_w_in(w_in):
    sizes = (DN_QKV, DN_HEADS, DN_HEADS, DN_VW, RET_QK, RET_QK, RET_VW, RET_VW, LRU_WIDTH, LRU_WIDTH,
             N_BRANCH * D_MODEL)
    offs = np.concatenate([[0], np.cumsum(sizes)])
    part = lambda n: w_in[..., int(offs[n]):int(offs[n + 1])]
    main = jnp.concatenate([part(0), part(4), part(5), part(6), part(7), part(8), part(9), part(10), part(3)],
                           axis=-1).astype(BF16)
    ab = jnp.concatenate([part(1), part(2)], axis=-1)
    w_ab = jnp.pad(ab, ((0, 0), (0, 0), (0, LANES - 2 * DN_HEADS))).astype(BF16)
    w_abt = jnp.pad(jnp.swapaxes(ab, -1, -2), ((0, 0), (0, AB_ROWS - 2 * DN_HEADS), (0, 0))).astype(BF16)
    return main, w_ab, w_abt


def kernel(x, positions, attn_norm, w_in, dn_conv_w, dn_a_log, dn_dt_bias, dn_norm_w, lru_conv_w, lru_conv_b,
           lru_wa, lru_ba, lru_wx, lru_bx, lru_lambda, w_branch, w_out, ffn_norm, w_up, ffn_conv_w, ffn_conv_b,
           w_down, final_norm):
    bsz, seq, _ = x.shape
    m = bsz * seq
    depth = w_in.shape[0]
    ret_tables = _retention_tables(min(TT_MIX, seq))
    inv_row = _rotary_inv_row()
    w_main, w_ab, w_abt = _split_w_in(w_in)
    hp = jnp.stack([dn_a_log, dn_dt_bias], axis=1).astype(F32)
    hpr = jnp.pad(hp, ((0, 0), (0, SUBLANES - 2), (0, LANES - DN_HEADS)))
    hpc = jnp.pad(jnp.swapaxes(hp, 1, 2), ((0, 0), (0, AB_ROWS - DN_HEADS), (0, LANES - 2)))
    wb = w_branch.astype(BF16)
    wo = w_out.astype(BF16)
    wup = w_up.astype(BF16)
    wdn = w_down.astype(BF16)
    wa = lru_wa.astype(BF16)
    wx = lru_wx.astype(BF16)

    x2 = x.reshape(m, D_MODEL)
    for l in range(depth):
        u2, ab, abt = _in_proj(x2, attn_norm[l][None, :], w_main, w_ab, w_abt, dn_conv_w[l], lru_conv_w[l],
                               lru_conv_b[l][None, :], l, seq)
        dn_p = (hpr[l], hpc[l], dn_norm_w[l][None, :])
        lru_p = (wa[l], lru_ba[l].reshape(1, LRU_WIDTH), wx[l], lru_bx[l].reshape(1, LRU_WIDTH),
                 lru_lambda[l][None, :])
        x2 = _mixer_merge(u2, ab, abt, positions, x2, inv_row, ret_tables, dn_p, lru_p, wb, wo, l, seq)
        x3 = _conv_ffn(x2.reshape(bsz, seq, D_MODEL), ffn_norm[l][None, :], wup, ffn_conv_w[l],
                       ffn_conv_b[l][None, :], wdn, final_norm[None, :], l, final=(l == depth - 1))
        x2 = x3.reshape(m, D_MODEL)
    return x2.reshape(bsz, seq, D_MODEL)
```

```python
import functools
import math

import numpy as np
import jax
import jax.numpy as jnp
from jax import lax
from jax.experimental import pallas as pl
from jax.experimental.pallas import tpu as pltpu

F32 = jnp.float32
BF16 = jnp.bfloat16
HIGHEST = lax.Precision.HIGHEST
F32_MIN_NORMAL = float(np.finfo(np.float32).tiny)

D_MODEL = 1024
DEPTH = 4
CHUNK = 64
EPS = 1e-6
N_BRANCH = 3
BRANCH_WIDTH = 512

DN_HEADS = 4
DN_DK = 128
DN_DV = 128
DN_CONV = 4
DN_QK = DN_HEADS * DN_DK
DN_VW = DN_HEADS * DN_DV
DN_QKV = 2 * DN_QK + DN_VW

RET_HEADS = 4
RET_DK = 64
RET_DV = 128
RET_QK = RET_HEADS * RET_DK
RET_VW = RET_HEADS * RET_DV
ROPE_BASE = 10000.0

LRU_WIDTH = 512
LRU_BLOCKS = 4
LRU_BLOCK = LRU_WIDTH // LRU_BLOCKS
LRU_CONV = 4
LRU_C = 8.0

D_FF = 2816
FFN_CONV = 3

SUBLANES = 8
LANES = 128
VMEM_LIMIT_BYTES = 56 * 1024 * 1024

RET_W = 2 * RET_QK + 2 * RET_VW
LRU_W = 2 * LRU_WIDTH
OFF_DN = 0
OFF_RET = OFF_DN + DN_QKV
OFF_LRU = OFF_RET + RET_W
OFF_GATE = OFF_LRU + LRU_W
OFF_Z = OFF_GATE + N_BRANCH * D_MODEL
N_MAIN = OFF_Z + DN_VW
AB_ROWS = 16

TM_IN = 512
IN_COLS = 256
TT_MIX = 256
DN_BLK = 2 * CHUNK
TM_MERGE = 512
TT_FFN = 512
FFN_COLS = 256


def _params(sem):
    return pltpu.CompilerParams(dimension_semantics=sem, vmem_limit_bytes=VMEM_LIMIT_BYTES)


def _sigmoid(x):
    return 1.0 / (1.0 + jnp.exp(-x))


def _silu(x):
    return x * _sigmoid(x)


def _softplus(x):
    return jnp.maximum(x, 0.0) + jnp.log1p(jnp.exp(-jnp.abs(x)))


def _dot(a, b):
    return jnp.dot(a.astype(BF16), b.astype(BF16), preferred_element_type=F32)


def _dot_nt(a, b):
    return lax.dot_general(a.astype(BF16), b.astype(BF16), (((1,), (1,)), ((), ())),
                           preferred_element_type=F32)


def _dot_tn(a, b):
    return lax.dot_general(a.astype(BF16), b.astype(BF16), (((0,), (0,)), ((), ())),
                           preferred_element_type=F32)


def _dot_f32(a, b):
    return jnp.dot(a, b, preferred_element_type=F32, precision=HIGHEST)


def _iota2(shape, dim):
    return lax.broadcasted_iota(jnp.int32, shape, dim)


def _rows_to_slabs(scr, x):
    tt = x.shape[0]
    ns = tt // SUBLANES
    panels = x.shape[1] // LANES
    for p in range(panels):
        scr[p] = x[:, p * LANES:(p + 1) * LANES]
    return jnp.concatenate(
        [jnp.concatenate([scr[p, pl.ds(i, ns, stride=SUBLANES), :] for p in range(panels)], axis=1)
         for i in range(SUBLANES)], axis=0)


def _slabs_to_rows(scr, y):
    tt = y.shape[0]
    ns = tt // SUBLANES
    panels = y.shape[1] // LANES
    for p in range(panels):
        for i in range(SUBLANES):
            scr[p, pl.ds(i, ns, stride=SUBLANES), :] = y[i * ns:(i + 1) * ns, p * LANES:(p + 1) * LANES]
    return jnp.concatenate([scr[p] for p in range(panels)], axis=1)


def _causal_conv_slabs(tail_scr, w_ref, cols, xs, width):
    ns = xs.shape[0] // SUBLANES
    group0 = _iota2((ns, xs.shape[1]), 0) == 0
    slabs = [xs[i * ns:(i + 1) * ns, :] for i in range(SUBLANES)]
    back = [jnp.where(group0, tail_scr[i:i + 1, cols], pltpu.roll(slabs[i], 1, 0))
            for i in range(SUBLANES - (width - 1), SUBLANES)]
    for i in range(SUBLANES - (width - 1), SUBLANES):
        tail_scr[i:i + 1, cols] = slabs[i][ns - 1:ns, :]
    ext = back + slabs
    out = []
    for i in range(SUBLANES):
        y = ext[i] * w_ref[0:1, cols]
        for j in range(1, width):
            y = y + ext[i + j] * w_ref[j:j + 1, cols]
        out.append(y)
    return jnp.concatenate(out, axis=0)


def _in_proj_chunk_order():
    starts = list(range(0, N_MAIN, IN_COLS))
    heavy = [c for c in starts if c < OFF_DN + DN_QKV or OFF_LRU <= c < OFF_LRU + LRU_WIDTH]
    plain = [c for c in starts if c not in heavy]
    order, per = [], len(plain) // len(heavy)
    for i, c in enumerate(heavy):
        order.append(c)
        order.extend(plain[i * per:(i + 1) * per])
    order.extend(plain[len(heavy) * per:])
    return [(c, IN_COLS) for c in order]


def _in_proj_kernel(x_ref, g_ref, w_ref, wab_ref, wabt_ref, dncw_ref, lrucw_ref, lrucb_ref,
                    u_ref, ab_ref, abt_ref, dn_tail, lru_tail, perm_scr, *, tiles_per_seq):
    @pl.when(pl.program_id(0) % tiles_per_seq == 0)
    def _():
        dn_tail[...] = jnp.zeros(dn_tail.shape, F32)
        lru_tail[...] = jnp.zeros(lru_tail.shape, F32)

    x = x_ref[...]
    hb = (x * lax.rsqrt(jnp.mean(x * x, axis=-1, keepdims=True) + EPS) * g_ref[...]).astype(BF16)

    n_conv = 0
    for c0, width in _in_proj_chunk_order():
        p = jnp.dot(hb, w_ref[:, c0:c0 + width], preferred_element_type=F32)
        is_dn = c0 < OFF_DN + DN_QKV
        is_lru = OFF_LRU <= c0 < OFF_LRU + LRU_WIDTH
        if is_dn or is_lru:
            ps = _rows_to_slabs(perm_scr.at[2 * n_conv], p)
            if is_dn:
                cols = slice(c0 - OFF_DN, c0 - OFF_DN + IN_COLS)
                ps = _silu(_causal_conv_slabs(dn_tail, dncw_ref, cols, ps, DN_CONV))
                if c0 < OFF_DN + 2 * DN_QK:
                    scale = DN_DK ** -0.5 if c0 < OFF_DN + DN_QK else 1.0
                    heads = [ps[:, h0:h0 + DN_DK] for h0 in range(0, IN_COLS, DN_DK)]
                    ps = jnp.concatenate(
                        [t * (lax.rsqrt(jnp.sum(t * t, axis=-1, keepdims=True) + EPS) * scale) for t in heads],
                        axis=1)
            else:
                cols = slice(c0 - OFF_LRU, c0 - OFF_LRU + IN_COLS)
                ps = _causal_conv_slabs(lru_tail, lrucw_ref, cols, ps, LRU_CONV) + lrucb_ref[:, cols]
            p = _slabs_to_rows(perm_scr.at[2 * n_conv + 1], ps)
            n_conv += 1
        u_ref[:, c0:c0 + width] = p.astype(BF16)
    ab_ref[...] = jnp.dot(hb, wab_ref[...], preferred_element_type=F32)
    abt_ref[...] = lax.dot_general(wabt_ref[...], hb, (((1,), (1,)), ((), ())), preferred_element_type=F32)


def _in_proj(x2, g, w_main, w_ab, w_abt, dn_conv_w, lru_conv_w, lru_conv_b, layer, seq):
    m = x2.shape[0]
    tm = min(TM_IN, seq)
    return pl.pallas_call(
        functools.partial(_in_proj_kernel, tiles_per_seq=seq // tm),
        grid=(m // tm,),
        in_specs=[
            pl.BlockSpec((tm, D_MODEL), lambda i: (i, 0)),
            pl.BlockSpec((1, D_MODEL), lambda i: (0, 0)),
            pl.BlockSpec((None, D_MODEL, N_MAIN), lambda i: (layer, 0, 0), pipeline_mode=pl.Buffered(1)),
            pl.BlockSpec((None, D_MODEL, LANES), lambda i: (layer, 0, 0)),
            pl.BlockSpec((None, AB_ROWS, D_MODEL), lambda i: (layer, 0, 0)),
            pl.BlockSpec((DN_CONV, DN_QKV), lambda i: (0, 0)),
            pl.BlockSpec((LRU_CONV, LRU_WIDTH), lambda i: (0, 0)),
            pl.BlockSpec((1, LRU_WIDTH), lambda i: (0, 0)),
        ],
        out_specs=[
            pl.BlockSpec((tm, N_MAIN), lambda i: (i, 0)),
            pl.BlockSpec((tm, LANES), lambda i: (i, 0)),
            pl.BlockSpec((AB_ROWS, tm), lambda i: (0, i)),
        ],
        out_shape=[
            jax.ShapeDtypeStruct((m, N_MAIN), BF16),
            jax.ShapeDtypeStruct((m, LANES), F32),
            jax.ShapeDtypeStruct((AB_ROWS, m), F32),
        ],
        scratch_shapes=[
            pltpu.VMEM((SUBLANES, DN_QKV), F32),
            pltpu.VMEM((SUBLANES, LRU_WIDTH), F32),
            pltpu.VMEM((2 * (DN_QKV + LRU_WIDTH) // IN_COLS, IN_COLS // LANES, tm, LANES), F32),
        ],
        compiler_params=_params(("arbitrary",)),
        name="in_proj",
    )(x2, g, w_main, w_ab, w_abt, dn_conv_w, lru_conv_w, lru_conv_b)


def _dn_kernel(qkv_ref, z_ref, ab_ref, abt_ref, hpr_ref, hpc_ref, normw_ref, o_ref, s_scr, *, tt):
    a = qkv_ref[0].astype(F32)

    ab = ab_ref[...]
    abt = abt_ref[...]
    g_cols = -jnp.exp(hpr_ref[0:1, :]) * _softplus(ab + hpr_ref[1:2, :])
    beta_cols = _sigmoid(ab)
    g_rows = -jnp.exp(hpc_ref[:, 0:1]) * _softplus(abt + hpc_ref[:, 1:2])

    rt = _iota2((tt, tt), 0)
    ct = _iota2((tt, tt), 1)
    same_chunk = (rt ^ ct) < CHUNK
    gc_cols = _dot_f32(jnp.where(same_chunk & (rt >= ct), 1.0, 0.0).astype(F32), g_cols)
    gc_rows = _dot_f32(g_rows, jnp.where(same_chunk & (rt <= ct), 1.0, 0.0).astype(F32))
    normw = normw_ref[...]
    z = z_ref[0].astype(F32)

    ri = _iota2((DN_BLK, DN_BLK), 0)
    ci = _iota2((DN_BLK, DN_BLK), 1)
    xr = ri ^ ci
    lower = ri > ci
    causal = (xr < CHUNK) & (ri >= ci)
    eye = jnp.where(ri == ci, 1.0, 0.0).astype(F32)

    heads = range(DN_HEADS)
    q, k, v, gcc, beta, eg, k_beta = [], [], [], [], [], [], []
    for h in heads:
        q.append(a[:, h * DN_DK:(h + 1) * DN_DK])
        k.append(a[:, DN_QK + h * DN_DK:DN_QK + (h + 1) * DN_DK])
        v.append(a[:, 2 * DN_QK + h * DN_DV:2 * DN_QK + (h + 1) * DN_DV])
        gcc.append(gc_cols[:, h:h + 1])
        beta.append(beta_cols[:, DN_HEADS + h:DN_HEADS + h + 1])
        eg.append(jnp.exp(gcc[h]))
        k_beta.append(k[h] * beta[h])
    rhs = [jnp.concatenate([v[h] * beta[h], k_beta[h] * eg[h]], axis=1) for h in heads]
    q_dec = [q[h] * eg[h] for h in heads]

    probs = [(h, b) for h in heads for b in range(tt // DN_BLK)]
    rows_of = lambda b: slice(b * DN_BLK, (b + 1) * DN_BLK)
    decay = [jnp.where(causal, jnp.exp(gcc[h][rows_of(b)] - gc_rows[h:h + 1, rows_of(b)]), 0.0)
             for h, b in probs]
    low = [jnp.where(lower, _dot_nt(k_beta[h][rows_of(b)], k[h][rows_of(b)]) * decay[i], 0.0)
           for i, (h, b) in enumerate(probs)]
    attn = [_dot_nt(q[h][rows_of(b)], k[h][rows_of(b)]) * decay[i] for i, (h, b) in enumerate(probs)]
    n = range(len(probs))
    d4 = [jnp.where(xr < 4, low[i], 0.0) for i in n]
    d4sq = [_dot(d4[i], d4[i]) for i in n]
    t_inv = [(eye - d4[i]) + _dot(eye - d4[i], d4sq[i]) for i in n]
    for lv in range(2, 6):
        off_t = [_dot(jnp.where((xr >> lv) == 1, low[i], 0.0), t_inv[i]) for i in n]
        t_inv = [t_inv[i] - _dot(t_inv[i], off_t[i]) for i in n]
    uw = {hb: _dot(t_inv[i], rhs[hb[0]][rows_of(hb[1])]) for i, hb in enumerate(probs)}
    attn = {hb: attn[i] for i, hb in enumerate(probs)}

    state = [s_scr[h] for h in heads]
    per_blk = DN_BLK // CHUNK
    for c in range(tt // CHUNK):
        rows = slice(c * CHUNK, (c + 1) * CHUNK)
        b = c // per_blk
        loc = slice((c % per_blk) * CHUNK, (c % per_blk + 1) * CHUNK)
        g_last = [gcc[h][(c + 1) * CHUNK - 1:(c + 1) * CHUNK, :] for h in heads]
        v_new = [uw[h, b][loc, :DN_DV] - _dot(uw[h, b][loc, DN_DV:], state[h]) for h in heads]
        out = [_dot(q_dec[h][rows], state[h]) + _dot(attn[h, b][loc, loc], v_new[h]) for h in heads]
        k_dec = [k[h][rows] * jnp.exp(g_last[h] - gcc[h][rows]) for h in heads]
        state = [state[h] * jnp.exp(g_last[h]) + _dot_tn(k_dec[h], v_new[h]) for h in heads]
        for h in heads:
            o = out[h] * lax.rsqrt(jnp.mean(out[h] * out[h], axis=-1, keepdims=True) + EPS)
            o = o * normw * _silu(z[rows, h * DN_DV:(h + 1) * DN_DV])
            o_ref[0, rows, h * DN_DV:(h + 1) * DN_DV] = o.astype(BF16)
    for h in heads:
        s_scr[h] = state[h]


def _ret_kernel(r_ref, cos_ref, sin_ref, dmask_ref, xi_ref, zeta_ref, gch_ref, o_ref, s_scr, *, tt):
    r = r_ref[0].astype(F32)
    cos = cos_ref[0]
    sin_signed = sin_ref[0]
    first_half = (_iota2((tt, LANES), 1) % RET_DK) < (RET_DK // 2)

    def rotary(t):
        swapped = jnp.where(first_half, pltpu.roll(t, LANES - RET_DK // 2, 1),
                            pltpu.roll(t, RET_DK // 2, 1))
        return t * cos + swapped * sin_signed

    q = jnp.concatenate([rotary(r[:, i * LANES:(i + 1) * LANES]) for i in range(RET_QK // LANES)], axis=1)
    k = jnp.concatenate([rotary(r[:, RET_QK + i * LANES:RET_QK + (i + 1) * LANES])
                         for i in range(RET_QK // LANES)], axis=1) * (RET_DK ** -0.5)
    v = r[:, 2 * RET_QK:2 * RET_QK + RET_VW]
    gate = r[:, 2 * RET_QK + RET_VW:]

    heads = range(RET_HEADS)
    qh = [q[:, h * RET_DK:(h + 1) * RET_DK] for h in heads]
    kh = [k[:, h * RET_DK:(h + 1) * RET_DK] for h in heads]
    vh = [v[:, h * RET_DV:(h + 1) * RET_DV] for h in heads]
    state = [s_scr[h] for h in heads]
    scores = [_dot_nt(qh[h], kh[h]) * dmask_ref[h] for h in heads]
    cross = [_dot(qh[h], state[h]) * xi_ref[:, h:h + 1] for h in heads]
    inner = [_dot(scores[h], vh[h]) for h in heads]
    for h in heads:
        s_scr[h] = state[h] * gch_ref[h:h + 1, :] + _dot_tn(kh[h] * zeta_ref[:, h:h + 1], vh[h])
    for h in heads:
        out = cross[h] + inner[h]
        mu = jnp.mean(out, axis=-1, keepdims=True)
        cen = out - mu
        var = jnp.mean(cen * cen, axis=-1, keepdims=True)
        out = cen * lax.rsqrt(var + EPS) * _silu(gate[:, h * RET_DV:(h + 1) * RET_DV])
        o_ref[0, :, h * RET_DV:(h + 1) * RET_DV] = out.astype(BF16)


def _lru_kernel(l_ref, wa_ref, ba_ref, wx_ref, bx_ref, lam_ref, o_ref, h_scr, perm_scr, *, tt):
    l = l_ref[0]
    xc = _rows_to_slabs(perm_scr.at[0], l[:, 0:LRU_WIDTH].astype(F32))
    rs, gs = [], []
    for n in range(LRU_BLOCKS):
        xb = xc[:, n * LRU_BLOCK:(n + 1) * LRU_BLOCK].astype(BF16)
        rs.append(jnp.dot(xb, wa_ref[n], preferred_element_type=F32))
        gs.append(jnp.dot(xb, wx_ref[n], preferred_element_type=F32))
    rgate = _sigmoid(jnp.concatenate(rs, axis=1) + ba_ref[...])
    igate = _sigmoid(jnp.concatenate(gs, axis=1) + bx_ref[...])
    log_a = (-LRU_C) * rgate * _softplus(-lam_ref[...])
    a = jnp.exp(log_a)
    one_minus = -jnp.tanh(log_a) * (a * a + 1.0)
    bv = one_minus * lax.rsqrt(jnp.maximum(one_minus, F32_MIN_NORMAL)) * (igate * xc)

    ns = tt // SUBLANES
    hs = [bv[0:ns, :]]
    ps = [a[0:ns, :]]
    for i in range(1, SUBLANES):
        ai = a[i * ns:(i + 1) * ns, :]
        hs.append(ai * hs[-1] + bv[i * ns:(i + 1) * ns, :])
        ps.append(ai * ps[-1])
    pg, hg = ps[-1], hs[-1]
    row = _iota2((ns, LRU_WIDTH), 0)
    s = 1
    while s < ns:
        keep = row >= s
        p_prev = jnp.where(keep, pltpu.roll(pg, s, 0), 1.0)
        h_prev = jnp.where(keep, pltpu.roll(hg, s, 0), 0.0)
        hg = pg * h_prev + hg
        pg = pg * p_prev
        s *= 2
    h0 = h_scr[...]
    after = pg * h0 + hg
    h_scr[...] = after[ns - 1:ns, :]
    before = jnp.where(row == 0, h0, pltpu.roll(after, 1, 0))
    hseq = _slabs_to_rows(perm_scr.at[1], jnp.concatenate([hs[i] + ps[i] * before for i in range(SUBLANES)],
                                                          axis=0))
    g = l[:, LRU_WIDTH:].astype(F32)
    gelu = 0.5 * g * (1.0 + jnp.tanh(math.sqrt(2.0 / math.pi) * (g + 0.044715 * (g * g * g))))
    o_ref[0] = (hseq * gelu).astype(BF16)


def _rope_kernel(pos_ref, inv_ref, cos_ref, sin_ref):
    ang = pos_ref[0].astype(F32) * inv_ref[...]
    first_half = (_iota2(ang.shape, 1) % RET_DK) < (RET_DK // 2)
    sin = jnp.sin(ang)
    cos_ref[0] = jnp.cos(ang)
    sin_ref[0] = jnp.where(first_half, -sin, sin)


def _rope_tables(pos3, inv_row):
    b, t, _ = pos3.shape
    tt = min(TT_FFN, t)
    spec = pl.BlockSpec((1, tt, LANES), lambda i, j: (i, j, 0))
    shape = jax.ShapeDtypeStruct((b, t, LANES), F32)
    return pl.pallas_call(
        _rope_kernel,
        grid=(b, t // tt),
        in_specs=[pl.BlockSpec((1, tt, 1), lambda i, j: (i, j, 0)), pl.BlockSpec((1, LANES), lambda i, j: (0, 0))],
        out_specs=[spec, spec],
        out_shape=[shape, shape],
        compiler_params=_params(("parallel", "parallel")),
        name="rope_tables",
    )(pos3, inv_row)


N_DN_IN, N_RET_IN, N_LRU_IN = 7, 7, 6


def _mixer_kernel(*refs, tt):
    dn_in = refs[:N_DN_IN]
    ret_in = refs[N_DN_IN:N_DN_IN + N_RET_IN]
    lru_in = refs[N_DN_IN + N_RET_IN:N_DN_IN + N_RET_IN + N_LRU_IN]
    ya_ref, yb_ref, yc_ref, dn_state, ret_state, lru_state, lru_perm = refs[N_DN_IN + N_RET_IN + N_LRU_IN:]

    @pl.when(pl.program_id(1) == 0)
    def _():
        for scr in (dn_state, ret_state, lru_state):
            scr[...] = jnp.zeros(scr.shape, F32)

    _ret_kernel(*ret_in, yb_ref, ret_state, tt=tt)
    _dn_kernel(*dn_in, ya_ref, dn_state, tt=tt)
    _lru_kernel(*lru_in, yc_ref, lru_state, lru_perm, tt=tt)


def _mixer(u3, ab, abt, rope_cos, rope_sin, ret_tables, dn_p, lru_p):
    b, t, _ = u3.shape
    tt = min(TT_MIX, t)
    nt = t // tt
    full2 = lambda i, j: (0, 0)
    full3 = lambda i, j: (0, 0, 0)
    dn_specs = [
        pl.BlockSpec((1, tt, DN_QKV), lambda i, j: (i, j, OFF_DN // DN_QKV)),
        pl.BlockSpec((1, tt, DN_VW), lambda i, j: (i, j, OFF_Z // DN_VW)),
        pl.BlockSpec((tt, LANES), lambda i, j: (i * nt + j, 0)),
        pl.BlockSpec((AB_ROWS, tt), lambda i, j: (0, i * nt + j)),
        pl.BlockSpec((SUBLANES, LANES), full2),
        pl.BlockSpec((AB_ROWS, LANES), full2),
        pl.BlockSpec((1, DN_DV), full2),
    ]
    ret_specs = [
        pl.BlockSpec((1, tt, RET_W), lambda i, j: (i, j, OFF_RET // RET_W)),
        pl.BlockSpec((1, tt, LANES), lambda i, j: (i, j, 0)),
        pl.BlockSpec((1, tt, LANES), lambda i, j: (i, j, 0)),
        pl.BlockSpec((RET_HEADS, tt, tt), full3),
        pl.BlockSpec((tt, LANES), full2),
        pl.BlockSpec((tt, LANES), full2),
        pl.BlockSpec((SUBLANES, LANES), full2),
    ]
    lru_specs = [
        pl.BlockSpec((1, tt, LRU_W), lambda i, j: (i, j, OFF_LRU // LRU_W)),
        pl.BlockSpec((LRU_BLOCKS, LRU_BLOCK, LRU_BLOCK), full3),
        pl.BlockSpec((1, LRU_WIDTH), full2),
        pl.BlockSpec((LRU_BLOCKS, LRU_BLOCK, LRU_BLOCK), full3),
        pl.BlockSpec((1, LRU_WIDTH), full2),
        pl.BlockSpec((1, LRU_WIDTH), full2),
    ]
    assert (len(dn_specs), len(ret_specs), len(lru_specs)) == (N_DN_IN, N_RET_IN, N_LRU_IN)
    out_spec = pl.BlockSpec((1, tt, BRANCH_WIDTH), lambda i, j: (i, j, 0))
    out_shape = jax.ShapeDtypeStruct((b, t, BRANCH_WIDTH), BF16)
    return pl.pallas_call(
        functools.partial(_mixer_kernel, tt=tt),
        grid=(b, nt),
        in_specs=dn_specs + ret_specs + lru_specs,
        out_specs=[out_spec] * N_BRANCH,
        out_shape=[out_shape] * N_BRANCH,
        scratch_shapes=[
            pltpu.VMEM((DN_HEADS, DN_DK, DN_DV), F32),
            pltpu.VMEM((RET_HEADS, RET_DK, RET_DV), F32),
            pltpu.VMEM((1, LRU_WIDTH), F32),
            pltpu.VMEM((2, LRU_WIDTH // LANES, tt, LANES), F32),
        ],
        compiler_params=_params(("parallel", "arbitrary")),
        name="mixer",
    )(u3, u3, ab, abt, *dn_p, u3, rope_cos, rope_sin, *ret_tables, u3, *lru_p)


def _merge_kernel(ya_ref, yb_ref, yc_ref, g0_ref, g1_ref, g2_ref, x_ref, wb_ref, wo_ref, o_ref):
    merged = None
    for y_ref, g_ref, n in ((ya_ref, g0_ref, 0), (yb_ref, g1_ref, 1), (yc_ref, g2_ref, 2)):
        term = _sigmoid(g_ref[...].astype(F32)) * jnp.dot(y_ref[...], wb_ref[n], preferred_element_type=F32)
        merged = term if merged is None else merged + term
    o_ref[...] = x_ref[...] + jnp.dot(merged.astype(BF16), wo_ref[...], preferred_element_type=F32)


def _merge(ya, yb, yc, u2, x2, wb, wo, layer):
    m = x2.shape[0]
    tm = min(TM_MERGE, m)
    gate_blk = OFF_GATE // D_MODEL
    yspec = pl.BlockSpec((tm, BRANCH_WIDTH), lambda i: (i, 0))
    return pl.pallas_call(
        _merge_kernel,
        grid=(m // tm,),
        in_specs=[
            yspec, yspec, yspec,
            pl.BlockSpec((tm, D_MODEL), lambda i: (i, gate_blk)),
            pl.BlockSpec((tm, D_MODEL), lambda i: (i, gate_blk + 1)),
            pl.BlockSpec((tm, D_MODEL), lambda i: (i, gate_blk + 2)),
            pl.BlockSpec((tm, D_MODEL), lambda i: (i, 0)),
            pl.BlockSpec((None, N_BRANCH, BRANCH_WIDTH, D_MODEL), lambda i: (layer, 0, 0, 0)),
            pl.BlockSpec((None, D_MODEL, D_MODEL), lambda i: (layer, 0, 0)),
        ],
        out_specs=pl.BlockSpec((tm, D_MODEL), lambda i: (i, 0)),
        out_shape=jax.ShapeDtypeStruct((m, D_MODEL), F32),
        compiler_params=_params(("parallel",)),
        name="merge",
    )(ya, yb, yc, u2, u2, u2, x2, wb, wo)


def _ffn_kernel(x_ref, g_ref, wup_ref, cw_ref, cb_ref, wdn_ref, fg_ref, o_ref,
                tail_scr, act_scr, perm_scr, *, tt, final):
    @pl.when(pl.program_id(1) == 0)
    def _():
        tail_scr[...] = jnp.zeros(tail_scr.shape, F32)

    x = _rows_to_slabs(perm_scr, x_ref[0])
    hb = (x * lax.rsqrt(jnp.mean(x * x, axis=-1, keepdims=True) + EPS) * g_ref[...]).astype(BF16)

    def conv(pre, col0):
        cols = slice(col0, col0 + FFN_COLS)
        return _causal_conv_slabs(tail_scr, cw_ref, cols, pre, FFN_CONV) + cb_ref[:, cols]

    for c in range(D_FF // FFN_COLS):
        c0 = c * FFN_COLS
        gate = conv(jnp.dot(hb, wup_ref[:, c0:c0 + FFN_COLS], preferred_element_type=F32), c0)
        val = conv(jnp.dot(hb, wup_ref[:, D_FF + c0:D_FF + c0 + FFN_COLS], preferred_element_type=F32),
                   D_FF + c0)
        act_scr[:, c0:c0 + FFN_COLS] = (_silu(gate) * val).astype(BF16)

    y = x + jnp.dot(act_scr[...], wdn_ref[...], preferred_element_type=F32)
    if final:
        y = y * lax.rsqrt(jnp.mean(y * y, axis=-1, keepdims=True) + EPS) * fg_ref[...]
    o_ref[0] = _slabs_to_rows(perm_scr, y)


def _conv_ffn(x3, g, w_up, conv_w, conv_b, w_down, final_g, layer, final):
    b, t, _ = x3.shape
    tt = min(TT_FFN, t)
    full2 = lambda i, j: (0, 0)
    of_layer = lambda i, j: (layer, 0, 0)
    return pl.pallas_call(
        functools.partial(_ffn_kernel, tt=tt, final=final),
        grid=(b, t // tt),
        in_specs=[
            pl.BlockSpec((1, tt, D_MODEL), lambda i, j: (i, j, 0)),
            pl.BlockSpec((1, D_MODEL), full2),
            pl.BlockSpec((None, D_MODEL, 2 * D_FF), of_layer, pipeline_mode=pl.Buffered(1)),
            pl.BlockSpec((FFN_CONV, 2 * D_FF), full2),
            pl.BlockSpec((1, 2 * D_FF), full2),
            pl.BlockSpec((None, D_FF, D_MODEL), of_layer, pipeline_mode=pl.Buffered(1)),
            pl.BlockSpec((1, D_MODEL), full2),
        ],
        out_specs=pl.BlockSpec((1, tt, D_MODEL), lambda i, j: (i, j, 0)),
        out_shape=jax.ShapeDtypeStruct((b, t, D_MODEL), F32),
        scratch_shapes=[
            pltpu.VMEM((SUBLANES, 2 * D_FF), F32),
            pltpu.VMEM((tt, D_FF), BF16),
            pltpu.VMEM((D_MODEL // LANES, tt, LANES), F32),
        ],
        compiler_params=_params(("parallel", "arbitrary")),
        name="conv_ffn",
    )(x3, g, w_up, conv_w, conv_b, w_down, final_g)


def _retention_tables(chunk):
    log_gamma = np.log(1.0 - 2.0 ** (-5.0 - np.arange(RET_HEADS, dtype=np.float64)))
    idx = np.arange(chunk, dtype=np.float64)
    dist = idx[:, None] - idx[None, :]
    causal = dist >= 0
    dmask = np.where(causal, np.exp(np.where(causal, dist, 0.0) * log_gamma[:, None, None]), 0.0)
    xi = np.zeros((chunk, LANES), np.float64)
    zeta = np.zeros((chunk, LANES), np.float64)
    xi[:, :RET_HEADS] = np.exp((idx[:, None] + 1.0) * log_gamma[None, :])
    zeta[:, :RET_HEADS] = np.exp((chunk - 1.0 - idx[:, None]) * log_gamma[None, :])
    gch = np.zeros((SUBLANES, LANES), np.float64)
    gch[:RET_HEADS, :] = np.exp(chunk * log_gamma)[:, None]
    return tuple(jnp.asarray(t, F32) for t in (dmask, xi, zeta, gch))


def _rotary_inv_row():
    half = RET_DK // 2
    inv = (ROPE_BASE ** (-np.arange(half, dtype=np.float32) / half)).astype(np.float32)
    return jnp.asarray(np.tile(inv, LANES // half)[None, :], F32)


def _split_w_in(w_in):
    sizes = (DN_QKV, DN_HEADS, DN_HEADS, DN_VW, RET_QK, RET_QK, RET_VW, RET_VW, LRU_WIDTH, LRU_WIDTH,
             N_BRANCH * D_MODEL)
    offs = np.concatenate([[0], np.cumsum(sizes)])
    part = lambda n: w_in[..., int(offs[n]):int(offs[n + 1])]
    main = jnp.concatenate([part(0), part(4), part(5), part(6), part(7), part(8), part(9), part(10), part(3)],
                           axis=-1).astype(BF16)
    ab = jnp.concatenate([part(1), part(2)], axis=-1)
    w_ab = jnp.pad(ab, ((0, 0), (0, 0), (0, LANES - 2 * DN_HEADS))).astype(BF16)
    w_abt = jnp.pad(jnp.swapaxes(ab, -1, -2), ((0, 0), (0, AB_ROWS - 2 * DN_HEADS), (0, 0))).astype(BF16)
    return main, w_ab, w_abt


def kernel(x, positions, attn_norm, w_in, dn_conv_w, dn_a_log, dn_dt_bias, dn_norm_w, lru_conv_w, lru_conv_b,
           lru_wa, lru_ba, lru_wx, lru_bx, lru_lambda, w_branch, w_out, ffn_norm, w_up, ffn_conv_w, ffn_conv_b,
           w_down, final_norm):
    bsz, seq, _ = x.shape
    m = bsz * seq
    depth = w_in.shape[0]
    ret_tables = _retention_tables(min(TT_MIX, seq))
    rope_cos, rope_sin = _rope_tables(positions[:, :, None], _rotary_inv_row())
    w_main, w_ab, w_abt = _split_w_in(w_in)
    hp = jnp.stack([dn_a_log, dn_dt_bias], axis=1).astype(F32)
    hpr = jnp.pad(hp, ((0, 0), (0, SUBLANES - 2), (0, LANES - DN_HEADS)))
    hpc = jnp.pad(jnp.swapaxes(hp, 1, 2), ((0, 0), (0, AB_ROWS - DN_HEADS), (0, LANES - 2)))
    wb = w_branch.astype(BF16)
    wo = w_out.astype(BF16)
    wup = w_up.astype(BF16)
    wdn = w_down.astype(BF16)
    wa = lru_wa.astype(BF16)
    wx = lru_wx.astype(BF16)

    x2 = x.reshape(m, D_MODEL)
    for l in range(depth):
        u2, ab, abt = _in_proj(x2, attn_norm[l][None, :], w_main, w_ab, w_abt, dn_conv_w[l], lru_conv_w[l],
                               lru_conv_b[l][None, :], l, seq)
        dn_p = (hpr[l], hpc[l], dn_norm_w[l][None, :])
        lru_p = (wa[l], lru_ba[l].reshape(1, LRU_WIDTH), wx[l], lru_bx[l].reshape(1, LRU_WIDTH),
                 lru_lambda[l][None, :])
        ya, yb, yc = _mixer(u2.reshape(bsz, seq, N_MAIN), ab, abt, rope_cos, rope_sin, ret_tables, dn_p, lru_p)
        x2 = _merge(ya.reshape(m, DN_VW), yb.reshape(m, RET_VW), yc.reshape(m, LRU_WIDTH), u2, x2, wb, wo, l)
        x3 = _conv_ffn(x2.reshape(bsz, seq, D_MODEL), ffn_norm[l][None, :], wup, ffn_conv_w[l],
                       ffn_conv_b[l][None, :], wdn, final_norm[None, :], l, final=(l == depth - 1))
        x2 = x3.reshape(m, D_MODEL)
    return x2.reshape(bsz, seq, D_MODEL)
```

```python
import functools
import math

import numpy as np
import jax
import jax.numpy as jnp
from jax import lax
from jax.experimental import pallas as pl
from jax.experimental.pallas import tpu as pltpu

F32 = jnp.float32
BF16 = jnp.bfloat16
HIGHEST = lax.Precision.HIGHEST
F32_MIN_NORMAL = float(np.finfo(np.float32).tiny)

D_MODEL = 1024
DEPTH = 4
CHUNK = 64
EPS = 1e-6
N_BRANCH = 3
BRANCH_WIDTH = 512

DN_HEADS = 4
DN_DK = 128
DN_DV = 128
DN_CONV = 4
DN_QK = DN_HEADS * DN_DK
DN_VW = DN_HEADS * DN_DV
DN_QKV = 2 * DN_QK + DN_VW

RET_HEADS = 4
RET_DK = 64
RET_DV = 128
RET_QK = RET_HEADS * RET_DK
RET_VW = RET_HEADS * RET_DV
ROPE_BASE = 10000.0

LRU_WIDTH = 512
LRU_BLOCKS = 4
LRU_BLOCK = LRU_WIDTH // LRU_BLOCKS
LRU_CONV = 4
LRU_C = 8.0

D_FF = 2816
FFN_CONV = 3

SUBLANES = 8
LANES = 128
VMEM_LIMIT_BYTES = 56 * 1024 * 1024

RET_W = 2 * RET_QK + 2 * RET_VW
LRU_W = 2 * LRU_WIDTH
OFF_DN = 0
OFF_RET = OFF_DN + DN_QKV
OFF_LRU = OFF_RET + RET_W
OFF_GATE = OFF_LRU + LRU_W
OFF_Z = OFF_GATE + N_BRANCH * D_MODEL
N_MAIN = OFF_Z + DN_VW
AB_ROWS = 16

TM_IN = 512
IN_COLS = 256
TT_MIX = 256
DN_BLK = 2 * CHUNK
TM_MERGE = 512
TT_FFN = 512
FFN_COLS = 256


def _params(sem):
    return pltpu.CompilerParams(dimension_semantics=sem, vmem_limit_bytes=VMEM_LIMIT_BYTES)


def _sigmoid(x):
    return 1.0 / (1.0 + jnp.exp(-x))


def _silu(x):
    return x * _sigmoid(x)


def _softplus(x):
    return jnp.maximum(x, 0.0) + jnp.log1p(jnp.exp(-jnp.abs(x)))


def _dot(a, b):
    return jnp.dot(a.astype(BF16), b.astype(BF16), preferred_element_type=F32)


def _dot_nt(a, b):
    return lax.dot_general(a.astype(BF16), b.astype(BF16), (((1,), (1,)), ((), ())),
                           preferred_element_type=F32)


def _dot_tn(a, b):
    return lax.dot_general(a.astype(BF16), b.astype(BF16), (((0,), (0,)), ((), ())),
                           preferred_element_type=F32)


def _dot_f32(a, b):
    return jnp.dot(a, b, preferred_element_type=F32, precision=HIGHEST)


def _iota2(shape, dim):
    return lax.broadcasted_iota(jnp.int32, shape, dim)


def _rows_to_slabs(scr, x):
    tt = x.shape[0]
    ns = tt // SUBLANES
    panels = x.shape[1] // LANES
    for p in range(panels):
        scr[p] = x[:, p * LANES:(p + 1) * LANES]
    return jnp.concatenate(
        [jnp.concatenate([scr[p, pl.ds(i, ns, stride=SUBLANES), :] for p in range(panels)], axis=1)
         for i in range(SUBLANES)], axis=0)


def _slabs_to_rows(scr, y):
    tt = y.shape[0]
    ns = tt // SUBLANES
    panels = y.shape[1] // LANES
    for p in range(panels):
        for i in range(SUBLANES):
            scr[p, pl.ds(i, ns, stride=SUBLANES), :] = y[i * ns:(i + 1) * ns, p * LANES:(p + 1) * LANES]
    return jnp.concatenate([scr[p] for p in range(panels)], axis=1)


def _causal_conv_slabs(tail_scr, w_ref, cols, xs, width):
    ns = xs.shape[0] // SUBLANES
    group0 = _iota2((ns, xs.shape[1]), 0) == 0
    slabs = [xs[i * ns:(i + 1) * ns, :] for i in range(SUBLANES)]
    back = [jnp.where(group0, tail_scr[i:i + 1, cols], pltpu.roll(slabs[i], 1, 0))
            for i in range(SUBLANES - (width - 1), SUBLANES)]
    for i in range(SUBLANES - (width - 1), SUBLANES):
        tail_scr[i:i + 1, cols] = slabs[i][ns - 1:ns, :]
    ext = back + slabs
    out = []
    for i in range(SUBLANES):
        y = ext[i] * w_ref[0:1, cols]
        for j in range(1, width):
            y = y + ext[i + j] * w_ref[j:j + 1, cols]
        out.append(y)
    return jnp.concatenate(out, axis=0)


def _in_proj_chunk_order():
    starts = list(range(0, N_MAIN, IN_COLS))
    heavy = [c for c in starts if c < OFF_DN + DN_QKV or OFF_LRU <= c < OFF_LRU + LRU_WIDTH]
    plain = [c for c in starts if c not in heavy]
    order, per = [], len(plain) // len(heavy)
    for i, c in enumerate(heavy):
        order.append(c)
        order.extend(plain[i * per:(i + 1) * per])
    order.extend(plain[len(heavy) * per:])
    return [(c, IN_COLS) for c in order]


def _in_proj_kernel(x_ref, g_ref, w_ref, wab_ref, wabt_ref, dncw_ref, lrucw_ref, lrucb_ref,
                    u_ref, ab_ref, abt_ref, dn_tail, lru_tail, perm_scr, *, tiles_per_seq):
    @pl.when(pl.program_id(0) % tiles_per_seq == 0)
    def _():
        dn_tail[...] = jnp.zeros(dn_tail.shape, F32)
        lru_tail[...] = jnp.zeros(lru_tail.shape, F32)

    x = x_ref[...]
    hb = (x * lax.rsqrt(jnp.mean(x * x, axis=-1, keepdims=True) + EPS) * g_ref[...]).astype(BF16)

    n_conv = 0
    for c0, width in _in_proj_chunk_order():
        p = jnp.dot(hb, w_ref[:, c0:c0 + width], preferred_element_type=F32)
        is_dn = c0 < OFF_DN + DN_QKV
        is_lru = OFF_LRU <= c0 < OFF_LRU + LRU_WIDTH
        if is_dn or is_lru:
            ps = _rows_to_slabs(perm_scr.at[2 * n_conv], p)
            if is_dn:
                cols = slice(c0 - OFF_DN, c0 - OFF_DN + IN_COLS)
                ps = _silu(_causal_conv_slabs(dn_tail, dncw_ref, cols, ps, DN_CONV))
                if c0 < OFF_DN + 2 * DN_QK:
                    scale = DN_DK ** -0.5 if c0 < OFF_DN + DN_QK else 1.0
                    heads = [ps[:, h0:h0 + DN_DK] for h0 in range(0, IN_COLS, DN_DK)]
                    ps = jnp.concatenate(
                        [t * (lax.rsqrt(jnp.sum(t * t, axis=-1, keepdims=True) + EPS) * scale) for t in heads],
                        axis=1)
            else:
                cols = slice(c0 - OFF_LRU, c0 - OFF_LRU + IN_COLS)
                ps = _causal_conv_slabs(lru_tail, lrucw_ref, cols, ps, LRU_CONV) + lrucb_ref[:, cols]
            p = _slabs_to_rows(perm_scr.at[2 * n_conv + 1], ps)
            n_conv += 1
        u_ref[:, c0:c0 + width] = p.astype(BF16)
    ab_ref[...] = jnp.dot(hb, wab_ref[...], preferred_element_type=F32)
    abt_ref[...] = lax.dot_general(wabt_ref[...], hb, (((1,), (1,)), ((), ())), preferred_element_type=F32)


def _in_proj(x2, g, w_main, w_ab, w_abt, dn_conv_w, lru_conv_w, lru_conv_b, layer, seq):
    m = x2.shape[0]
    tm = min(TM_IN, seq)
    return pl.pallas_call(
        functools.partial(_in_proj_kernel, tiles_per_seq=seq // tm),
        grid=(m // tm,),
        in_specs=[
            pl.BlockSpec((tm, D_MODEL), lambda i: (i, 0)),
            pl.BlockSpec((1, D_MODEL), lambda i: (0, 0)),
            pl.BlockSpec((None, D_MODEL, N_MAIN), lambda i: (layer, 0, 0), pipeline_mode=pl.Buffered(1)),
            pl.BlockSpec((None, D_MODEL, LANES), lambda i: (layer, 0, 0)),
            pl.BlockSpec((None, AB_ROWS, D_MODEL), lambda i: (layer, 0, 0)),
            pl.BlockSpec((DN_CONV, DN_QKV), lambda i: (0, 0)),
            pl.BlockSpec((LRU_CONV, LRU_WIDTH), lambda i: (0, 0)),
            pl.BlockSpec((1, LRU_WIDTH), lambda i: (0, 0)),
        ],
        out_specs=[
            pl.BlockSpec((tm, N_MAIN), lambda i: (i, 0)),
            pl.BlockSpec((tm, LANES), lambda i: (i, 0)),
            pl.BlockSpec((AB_ROWS, tm), lambda i: (0, i)),
        ],
        out_shape=[
            jax.ShapeDtypeStruct((m, N_MAIN), BF16),
            jax.ShapeDtypeStruct((m, LANES), F32),
            jax.ShapeDtypeStruct((AB_ROWS, m), F32),
        ],
        scratch_shapes=[
            pltpu.VMEM((SUBLANES, DN_QKV), F32),
            pltpu.VMEM((SUBLANES, LRU_WIDTH), F32),
            pltpu.VMEM((2 * (DN_QKV + LRU_WIDTH) // IN_COLS, IN_COLS // LANES, tm, LANES), F32),
        ],
        compiler_params=_params(("arbitrary",)),
        name="in_proj",
    )(x2, g, w_main, w_ab, w_abt, dn_conv_w, lru_conv_w, lru_conv_b)


def _dn_kernel(qkv_ref, z_ref, ab_ref, abt_ref, hpr_ref, hpc_ref, normw_ref, o_ref, s_scr, *, tt):
    a = qkv_ref[0].astype(F32)

    ab = ab_ref[...]
    abt = abt_ref[...]
    g_cols = -jnp.exp(hpr_ref[0:1, :]) * _softplus(ab + hpr_ref[1:2, :])
    beta_cols = _sigmoid(ab)
    g_rows = -jnp.exp(hpc_ref[:, 0:1]) * _softplus(abt + hpc_ref[:, 1:2])

    rt = _iota2((tt, tt), 0)
    ct = _iota2((tt, tt), 1)
    same_chunk = (rt ^ ct) < CHUNK
    gc_cols = _dot_f32(jnp.where(same_chunk & (rt >= ct), 1.0, 0.0).astype(F32), g_cols)
    gc_rows = _dot_f32(g_rows, jnp.where(same_chunk & (rt <= ct), 1.0, 0.0).astype(F32))
    normw = normw_ref[...]
    z = z_ref[0].astype(F32)

    ri = _iota2((DN_BLK, DN_BLK), 0)
    ci = _iota2((DN_BLK, DN_BLK), 1)
    xr = ri ^ ci
    lower = ri > ci
    causal = (xr < CHUNK) & (ri >= ci)
    eye = jnp.where(ri == ci, 1.0, 0.0).astype(F32)

    heads = range(DN_HEADS)
    q, k, v, gcc, beta, eg, k_beta = [], [], [], [], [], [], []
    for h in heads:
        q.append(a[:, h * DN_DK:(h + 1) * DN_DK])
        k.append(a[:, DN_QK + h * DN_DK:DN_QK + (h + 1) * DN_DK])
        v.append(a[:, 2 * DN_QK + h * DN_DV:2 * DN_QK + (h + 1) * DN_DV])
        gcc.append(gc_cols[:, h:h + 1])
        beta.append(beta_cols[:, DN_HEADS + h:DN_HEADS + h + 1])
        eg.append(jnp.exp(gcc[h]))
        k_beta.append(k[h] * beta[h])
    rhs = [jnp.concatenate([v[h] * beta[h], k_beta[h] * eg[h]], axis=1) for h in heads]
    q_dec = [q[h] * eg[h] for h in heads]

    probs = [(h, b) for h in heads for b in range(tt // DN_BLK)]
    rows_of = lambda b: slice(b * DN_BLK, (b + 1) * DN_BLK)
    decay = [jnp.where(causal, jnp.exp(gcc[h][rows_of(b)] - gc_rows[h:h + 1, rows_of(b)]), 0.0)
             for h, b in probs]
    low = [jnp.where(lower, _dot_nt(k_beta[h][rows_of(b)], k[h][rows_of(b)]) * decay[i], 0.0)
           for i, (h, b) in enumerate(probs)]
    attn = [_dot_nt(q[h][rows_of(b)], k[h][rows_of(b)]) * decay[i] for i, (h, b) in enumerate(probs)]
    n = range(len(probs))
    d4 = [jnp.where(xr < 4, low[i], 0.0) for i in n]
    d4sq = [_dot(d4[i], d4[i]) for i in n]
    t_inv = [(eye - d4[i]) + _dot(eye - d4[i], d4sq[i]) for i in n]
    for lv in range(2, 6):
        off_t = [_dot(jnp.where((xr >> lv) == 1, low[i], 0.0), t_inv[i]) for i in n]
        t_inv = [t_inv[i] - _dot(t_inv[i], off_t[i]) for i in n]
    uw = {hb: _dot(t_inv[i], rhs[hb[0]][rows_of(hb[1])]) for i, hb in enumerate(probs)}
    attn = {hb: attn[i] for i, hb in enumerate(probs)}

    state = [s_scr[h] for h in heads]
    per_blk = DN_BLK // CHUNK
    for c in range(tt // CHUNK):
        rows = slice(c * CHUNK, (c + 1) * CHUNK)
        b = c // per_blk
        loc = slice((c % per_blk) * CHUNK, (c % per_blk + 1) * CHUNK)
        g_last = [gcc[h][(c + 1) * CHUNK - 1:(c + 1) * CHUNK, :] for h in heads]
        v_new = [uw[h, b][loc, :DN_DV] - _dot(uw[h, b][loc, DN_DV:], state[h]) for h in heads]
        out = [_dot(q_dec[h][rows], state[h]) + _dot(attn[h, b][loc, loc], v_new[h]) for h in heads]
        k_dec = [k[h][rows] * jnp.exp(g_last[h] - gcc[h][rows]) for h in heads]
        state = [state[h] * jnp.exp(g_last[h]) + _dot_tn(k_dec[h], v_new[h]) for h in heads]
        for h in heads:
            o = out[h] * lax.rsqrt(jnp.mean(out[h] * out[h], axis=-1, keepdims=True) + EPS)
            o = o * normw * _silu(z[rows, h * DN_DV:(h + 1) * DN_DV])
            o_ref[0, rows, h * DN_DV:(h + 1) * DN_DV] = o.astype(BF16)
    for h in heads:
        s_scr[h] = state[h]


def _ret_kernel(r_ref, cos_ref, sin_ref, dmask_ref, xi_ref, zeta_ref, gch_ref, o_ref, s_scr, *, tt):
    r = r_ref[0].astype(F32)
    cos = cos_ref[0]
    sin_signed = sin_ref[0]
    first_half = (_iota2((tt, LANES), 1) % RET_DK) < (RET_DK // 2)

    def rotary(t):
        swapped = jnp.where(first_half, pltpu.roll(t, LANES - RET_DK // 2, 1),
                            pltpu.roll(t, RET_DK // 2, 1))
        return t * cos + swapped * sin_signed

    q = jnp.concatenate([rotary(r[:, i * LANES:(i + 1) * LANES]) for i in range(RET_QK // LANES)], axis=1)
    k = jnp.concatenate([rotary(r[:, RET_QK + i * LANES:RET_QK + (i + 1) * LANES])
                         for i in range(RET_QK // LANES)], axis=1) * (RET_DK ** -0.5)
    v = r[:, 2 * RET_QK:2 * RET_QK + RET_VW]
    gate = r[:, 2 * RET_QK + RET_VW:]

    heads = range(RET_HEADS)
    qh = [q[:, h * RET_DK:(h + 1) * RET_DK] for h in heads]
    kh = [k[:, h * RET_DK:(h + 1) * RET_DK] for h in heads]
    vh = [v[:, h * RET_DV:(h + 1) * RET_DV] for h in heads]
    state = [s_scr[h] for h in heads]
    scores = [_dot_nt(qh[h], kh[h]) * dmask_ref[h] for h in heads]
    cross = [_dot(qh[h], state[h]) * xi_ref[:, h:h + 1] for h in heads]
    inner = [_dot(scores[h], vh[h]) for h in heads]
    for h in heads:
        s_scr[h] = state[h] * gch_ref[h:h + 1, :] + _dot_tn(kh[h] * zeta_ref[:, h:h + 1], vh[h])
    for h in heads:
        out = cross[h] + inner[h]
        mu = jnp.mean(out, axis=-1, keepdims=True)
        cen = out - mu
        var = jnp.mean(cen * cen, axis=-1, keepdims=True)
        out = cen * lax.rsqrt(var + EPS) * _silu(gate[:, h * RET_DV:(h + 1) * RET_DV])
        o_ref[0, :, h * RET_DV:(h + 1) * RET_DV] = out.astype(BF16)


def _lru_kernel(l_ref, wax_ref, ba_ref, bx_ref, lam_ref, o_ref, h_scr, perm_scr, *, tt):
    l = l_ref[0]
    xc = _rows_to_slabs(perm_scr.at[0], l[:, 0:LRU_WIDTH].astype(F32))
    rs, gs = [], []
    for n in range(LRU_BLOCKS):
        xb = xc[:, n * LRU_BLOCK:(n + 1) * LRU_BLOCK].astype(BF16)
        both = jnp.dot(xb, wax_ref[n], preferred_element_type=F32)
        rs.append(both[:, :LRU_BLOCK])
        gs.append(both[:, LRU_BLOCK:])
    rgate = _sigmoid(jnp.concatenate(rs, axis=1) + ba_ref[...])
    igate = _sigmoid(jnp.concatenate(gs, axis=1) + bx_ref[...])
    log_a = (-LRU_C) * rgate * _softplus(-lam_ref[...])
    a = jnp.exp(log_a)
    one_minus = -jnp.tanh(log_a) * (a * a + 1.0)
    bv = one_minus * lax.rsqrt(jnp.maximum(one_minus, F32_MIN_NORMAL)) * (igate * xc)

    ns = tt // SUBLANES
    hs = [bv[0:ns, :]]
    ps = [a[0:ns, :]]
    for i in range(1, SUBLANES):
        ai = a[i * ns:(i + 1) * ns, :]
        hs.append(ai * hs[-1] + bv[i * ns:(i + 1) * ns, :])
        ps.append(ai * ps[-1])
    pg, hg = ps[-1], hs[-1]
    row = _iota2((ns, LRU_WIDTH), 0)
    s = 1
    while s < ns:
        keep = row >= s
        p_prev = jnp.where(keep, pltpu.roll(pg, s, 0), 1.0)
        h_prev = jnp.where(keep, pltpu.roll(hg, s, 0), 0.0)
        hg = pg * h_prev + hg
        pg = pg * p_prev
        s *= 2
    h0 = h_scr[...]
    after = pg * h0 + hg
    h_scr[...] = after[ns - 1:ns, :]
    before = jnp.where(row == 0, h0, pltpu.roll(after, 1, 0))
    hseq = _slabs_to_rows(perm_scr.at[1], jnp.concatenate([hs[i] + ps[i] * before for i in range(SUBLANES)],
                                                          axis=0))
    g = l[:, LRU_WIDTH:].astype(F32)
    gelu = 0.5 * g * (1.0 + jnp.tanh(math.sqrt(2.0 / math.pi) * (g + 0.044715 * (g * g * g))))
    o_ref[0] = (hseq * gelu).astype(BF16)


def _rope_kernel(pos_ref, inv_ref, cos_ref, sin_ref):
    ang = pos_ref[0].astype(F32) * inv_ref[...]
    first_half = (_iota2(ang.shape, 1) % RET_DK) < (RET_DK // 2)
    sin = jnp.sin(ang)
    cos_ref[0] = jnp.cos(ang)
    sin_ref[0] = jnp.where(first_half, -sin, sin)


def _rope_tables(positions, inv_row):
    b, t = positions.shape
    tt = min(TT_FFN, t)
    spec = pl.BlockSpec((1, tt, LANES), lambda i, j: (i, j, 0))
    shape = jax.ShapeDtypeStruct((b, t, LANES), F32)
    pos3 = jnp.broadcast_to(positions[:, :, None], (b, t, LANES))
    return pl.pallas_call(
        _rope_kernel,
        grid=(b, t // tt),
        in_specs=[spec, pl.BlockSpec((1, LANES), lambda i, j: (0, 0))],
        out_specs=[spec, spec],
        out_shape=[shape, shape],
        compiler_params=_params(("parallel", "parallel")),
        name="rope_tables",
    )(pos3, inv_row)


N_DN_IN, N_RET_IN, N_LRU_IN = 7, 7, 5


def _mixer_kernel(*refs, tt):
    dn_in = refs[:N_DN_IN]
    ret_in = refs[N_DN_IN:N_DN_IN + N_RET_IN]
    lru_in = refs[N_DN_IN + N_RET_IN:N_DN_IN + N_RET_IN + N_LRU_IN]
    ya_ref, yb_ref, yc_ref, dn_state, ret_state, lru_state, lru_perm = refs[N_DN_IN + N_RET_IN + N_LRU_IN:]

    @pl.when(pl.program_id(1) == 0)
    def _():
        for scr in (dn_state, ret_state, lru_state):
            scr[...] = jnp.zeros(scr.shape, F32)

    _ret_kernel(*ret_in, yb_ref, ret_state, tt=tt)
    _dn_kernel(*dn_in, ya_ref, dn_state, tt=tt)
    _lru_kernel(*lru_in, yc_ref, lru_state, lru_perm, tt=tt)


def _mixer(u3, ab, abt, rope_cos, rope_sin, ret_tables, dn_p, lru_p):
    b, t, _ = u3.shape
    tt = min(TT_MIX, t)
    nt = t // tt
    full2 = lambda i, j: (0, 0)
    full3 = lambda i, j: (0, 0, 0)
    dn_specs = [
        pl.BlockSpec((1, tt, DN_QKV), lambda i, j: (i, j, OFF_DN // DN_QKV)),
        pl.BlockSpec((1, tt, DN_VW), lambda i, j: (i, j, OFF_Z // DN_VW)),
        pl.BlockSpec((tt, LANES), lambda i, j: (i * nt + j, 0)),
        pl.BlockSpec((AB_ROWS, tt), lambda i, j: (0, i * nt + j)),
        pl.BlockSpec((SUBLANES, LANES), full2),
        pl.BlockSpec((AB_ROWS, LANES), full2),
        pl.BlockSpec((1, DN_DV), full2),
    ]
    ret_specs = [
        pl.BlockSpec((1, tt, RET_W), lambda i, j: (i, j, OFF_RET // RET_W)),
        pl.BlockSpec((1, tt, LANES), lambda i, j: (i, j, 0)),
        pl.BlockSpec((1, tt, LANES), lambda i, j: (i, j, 0)),
        pl.BlockSpec((RET_HEADS, tt, tt), full3),
        pl.BlockSpec((tt, LANES), full2),
        pl.BlockSpec((tt, LANES), full2),
        pl.BlockSpec((SUBLANES, LANES), full2),
    ]
    lru_specs = [
        pl.BlockSpec((1, tt, LRU_W), lambda i, j: (i, j, OFF_LRU // LRU_W)),
        pl.BlockSpec((LRU_BLOCKS, LRU_BLOCK, 2 * LRU_BLOCK), full3),
        pl.BlockSpec((1, LRU_WIDTH), full2),
        pl.BlockSpec((1, LRU_WIDTH), full2),
        pl.BlockSpec((1, LRU_WIDTH), full2),
    ]
    assert (len(dn_specs), len(ret_specs), len(lru_specs)) == (N_DN_IN, N_RET_IN, N_LRU_IN)
    out_spec = pl.BlockSpec((1, tt, BRANCH_WIDTH), lambda i, j: (i, j, 0))
    out_shape = jax.ShapeDtypeStruct((b, t, BRANCH_WIDTH), BF16)
    return pl.pallas_call(
        functools.partial(_mixer_kernel, tt=tt),
        grid=(b, nt),
        in_specs=dn_specs + ret_specs + lru_specs,
        out_specs=[out_spec] * N_BRANCH,
        out_shape=[out_shape] * N_BRANCH,
        scratch_shapes=[
            pltpu.VMEM((DN_HEADS, DN_DK, DN_DV), F32),
            pltpu.VMEM((RET_HEADS, RET_DK, RET_DV), F32),
            pltpu.VMEM((1, LRU_WIDTH), F32),
            pltpu.VMEM((2, LRU_WIDTH // LANES, tt, LANES), F32),
        ],
        compiler_params=_params(("parallel", "arbitrary")),
        name="mixer",
    )(u3, u3, ab, abt, *dn_p, u3, rope_cos, rope_sin, *ret_tables, u3, *lru_p)


def _merge_kernel(ya_ref, yb_ref, yc_ref, g0_ref, g1_ref, g2_ref, x_ref, wb_ref, wo_ref, o_ref):
    merged = None
    for y_ref, g_ref, n in ((ya_ref, g0_ref, 0), (yb_ref, g1_ref, 1), (yc_ref, g2_ref, 2)):
        term = _sigmoid(g_ref[...].astype(F32)) * jnp.dot(y_ref[...], wb_ref[n], preferred_element_type=F32)
        merged = term if merged is None else merged + term
    o_ref[...] = x_ref[...] + jnp.dot(merged.astype(BF16), wo_ref[...], preferred_element_type=F32)


def _merge(ya, yb, yc, u2, x2, wb, wo, layer):
    m = x2.shape[0]
    tm = min(TM_MERGE, m)
    gate_blk = OFF_GATE // D_MODEL
    yspec = pl.BlockSpec((tm, BRANCH_WIDTH), lambda i: (i, 0))
    return pl.pallas_call(
        _merge_kernel,
        grid=(m // tm,),
        in_specs=[
            yspec, yspec, yspec,
            pl.BlockSpec((tm, D_MODEL), lambda i: (i, gate_blk)),
            pl.BlockSpec((tm, D_MODEL), lambda i: (i, gate_blk + 1)),
            pl.BlockSpec((tm, D_MODEL), lambda i: (i, gate_blk + 2)),
            pl.BlockSpec((tm, D_MODEL), lambda i: (i, 0)),
            pl.BlockSpec((None, N_BRANCH, BRANCH_WIDTH, D_MODEL), lambda i: (layer, 0, 0, 0)),
            pl.BlockSpec((None, D_MODEL, D_MODEL), lambda i: (layer, 0, 0)),
        ],
        out_specs=pl.BlockSpec((tm, D_MODEL), lambda i: (i, 0)),
        out_shape=jax.ShapeDtypeStruct((m, D_MODEL), F32),
        compiler_params=_params(("parallel",)),
        name="merge",
    )(ya, yb, yc, u2, u2, u2, x2, wb, wo)


def _ffn_kernel(x_ref, g_ref, wup_ref, cw_ref, cb_ref, wdn_ref, fg_ref, o_ref,
                tail_scr, act_scr, perm_scr, *, tt, final):
    @pl.when(pl.program_id(1) == 0)
    def _():
        tail_scr[...] = jnp.zeros(tail_scr.shape, F32)

    x = _rows_to_slabs(perm_scr, x_ref[0])
    hb = (x * lax.rsqrt(jnp.mean(x * x, axis=-1, keepdims=True) + EPS) * g_ref[...]).astype(BF16)

    def conv(pre, col0):
        cols = slice(col0, col0 + FFN_COLS)
        return _causal_conv_slabs(tail_scr, cw_ref, cols, pre, FFN_CONV) + cb_ref[:, cols]

    for c in range(D_FF // FFN_COLS):
        c0 = c * FFN_COLS
        gate = conv(jnp.dot(hb, wup_ref[:, c0:c0 + FFN_COLS], preferred_element_type=F32), c0)
        val = conv(jnp.dot(hb, wup_ref[:, D_FF + c0:D_FF + c0 + FFN_COLS], preferred_element_type=F32),
                   D_FF + c0)
        act_scr[:, c0:c0 + FFN_COLS] = (_silu(gate) * val).astype(BF16)

    y = x + jnp.dot(act_scr[...], wdn_ref[...], preferred_element_type=F32)
    if final:
        y = y * lax.rsqrt(jnp.mean(y * y, axis=-1, keepdims=True) + EPS) * fg_ref[...]
    o_ref[0] = _slabs_to_rows(perm_scr, y)


def _conv_ffn(x3, g, w_up, conv_w, conv_b, w_down, final_g, layer, final):
    b, t, _ = x3.shape
    tt = min(TT_FFN, t)
    full2 = lambda i, j: (0, 0)
    of_layer = lambda i, j: (layer, 0, 0)
    return pl.pallas_call(
        functools.partial(_ffn_kernel, tt=tt, final=final),
        grid=(b, t // tt),
        in_specs=[
            pl.BlockSpec((1, tt, D_MODEL), lambda i, j: (i, j, 0)),
            pl.BlockSpec((1, D_MODEL), full2),
            pl.BlockSpec((None, D_MODEL, 2 * D_FF), of_layer, pipeline_mode=pl.Buffered(1)),
            pl.BlockSpec((FFN_CONV, 2 * D_FF), full2),
            pl.BlockSpec((1, 2 * D_FF), full2),
            pl.BlockSpec((None, D_FF, D_MODEL), of_layer, pipeline_mode=pl.Buffered(1)),
            pl.BlockSpec((1, D_MODEL), full2),
        ],
        out_specs=pl.BlockSpec((1, tt, D_MODEL), lambda i, j: (i, j, 0)),
        out_shape=jax.ShapeDtypeStruct((b, t, D_MODEL), F32),
        scratch_shapes=[
            pltpu.VMEM((SUBLANES, 2 * D_FF), F32),
            pltpu.VMEM((tt, D_FF), BF16),
            pltpu.VMEM((D_MODEL // LANES, tt, LANES), F32),
        ],
        compiler_params=_params(("parallel", "arbitrary")),
        name="conv_ffn",
    )(x3, g, w_up, conv_w, conv_b, w_down, final_g)


def _retention_tables(chunk):
    log_gamma = np.log(1.0 - 2.0 ** (-5.0 - np.arange(RET_HEADS, dtype=np.float64)))
    idx = np.arange(chunk, dtype=np.float64)
    dist = idx[:, None] - idx[None, :]
    causal = dist >= 0
    dmask = np.where(causal, np.exp(np.where(causal, dist, 0.0) * log_gamma[:, None, None]), 0.0)
    xi = np.zeros((chunk, LANES), np.float64)
    zeta = np.zeros((chunk, LANES), np.float64)
    xi[:, :RET_HEADS] = np.exp((idx[:, None] + 1.0) * log_gamma[None, :])
    zeta[:, :RET_HEADS] = np.exp((chunk - 1.0 - idx[:, None]) * log_gamma[None, :])
    gch = np.zeros((SUBLANES, LANES), np.float64)
    gch[:RET_HEADS, :] = np.exp(chunk * log_gamma)[:, None]
    return tuple(jnp.asarray(t, F32) for t in (dmask, xi, zeta, gch))


def _rotary_inv_row():
    half = RET_DK // 2
    inv = (ROPE_BASE ** (-np.arange(half, dtype=np.float32) / half)).astype(np.float32)
    return jnp.asarray(np.tile(inv, LANES // half)[None, :], F32)


def _split_w_in(w_in):
    sizes = (DN_QKV, DN_HEADS, DN_HEADS, DN_VW, RET_QK, RET_QK, RET_VW, RET_VW, LRU_WIDTH, LRU_WIDTH,
             N_BRANCH * D_MODEL)
    offs = np.concatenate([[0], np.cumsum(sizes)])
    w_in = w_in.astype(BF16)
    part = lambda n: w_in[..., int(offs[n]):int(offs[n + 1])]
    main = jnp.concatenate([part(0), part(4), part(5), part(6), part(7), part(8), part(9), part(10), part(3)],
                           axis=-1)
    ab = jnp.concatenate([part(1), part(2)], axis=-1)
    w_ab = jnp.pad(ab, ((0, 0), (0, 0), (0, LANES - 2 * DN_HEADS)))
    w_abt = jnp.pad(jnp.swapaxes(ab, -1, -2), ((0, 0), (0, AB_ROWS - 2 * DN_HEADS), (0, 0)))
    return main, w_ab, w_abt


def kernel(x, positions, attn_norm, w_in, dn_conv_w, dn_a_log, dn_dt_bias, dn_norm_w, lru_conv_w, lru_conv_b,
           lru_wa, lru_ba, lru_wx, lru_bx, lru_lambda, w_branch, w_out, ffn_norm, w_up, ffn_conv_w, ffn_conv_b,
           w_down, final_norm):
    bsz, seq, _ = x.shape
    m = bsz * seq
    depth = w_in.shape[0]
    ret_tables = _retention_tables(min(TT_MIX, seq))
    rope_cos, rope_sin = _rope_tables(positions, _rotary_inv_row())
    w_main, w_ab, w_abt = _split_w_in(w_in)
    hp = jnp.stack([dn_a_log, dn_dt_bias], axis=1).astype(F32)
    hpr = jnp.pad(hp, ((0, 0), (0, SUBLANES - 2), (0, LANES - DN_HEADS)))
    hpc = jnp.pad(jnp.swapaxes(hp, 1, 2), ((0, 0), (0, AB_ROWS - DN_HEADS), (0, LANES - 2)))
    wb = w_branch.astype(BF16)
    wo = w_out.astype(BF16)
    wup = w_up.astype(BF16)
    wdn = w_down.astype(BF16)
    wax = jnp.concatenate([lru_wa, lru_wx], axis=-1).astype(BF16)

    x2 = x.reshape(m, D_MODEL)
    for l in range(depth):
        u2, ab, abt = _in_proj(x2, attn_norm[l][None, :], w_main, w_ab, w_abt, dn_conv_w[l], lru_conv_w[l],
                               lru_conv_b[l][None, :], l, seq)
        dn_p = (hpr[l], hpc[l], dn_norm_w[l][None, :])
        lru_p = (wax[l], lru_ba[l].reshape(1, LRU_WIDTH), lru_bx[l].reshape(1, LRU_WIDTH),
                 lru_lambda[l][None, :])
        ya, yb, yc = _mixer(u2.reshape(bsz, seq, N_MAIN), ab, abt, rope_cos, rope_sin, ret_tables, dn_p, lru_p)
        x2 = _merge(ya.reshape(m, DN_VW), yb.reshape(m, RET_VW), yc.reshape(m, LRU_WIDTH), u2, x2, wb, wo, l)
        x3 = _conv_ffn(x2.reshape(bsz, seq, D_MODEL), ffn_norm[l][None, :], wup, ffn_conv_w[l],
                       ffn_conv_b[l][None, :], wdn, final_norm[None, :], l, final=(l == depth - 1))
        x2 = x3.reshape(m, D_MODEL)
    return x2.reshape(bsz, seq, D_MODEL)
```

```python
import functools
import math

import numpy as np
import jax
import jax.numpy as jnp
from jax import lax
from jax.experimental import pallas as pl
from jax.experimental.pallas import tpu as pltpu

F32 = jnp.float32
BF16 = jnp.bfloat16
HIGHEST = lax.Precision.HIGHEST
F32_MIN_NORMAL = float(np.finfo(np.float32).tiny)

D_MODEL = 1024
DEPTH = 4
CHUNK = 64
EPS = 1e-6
N_BRANCH = 3
BRANCH_WIDTH = 512

DN_HEADS = 4
DN_DK = 128
DN_DV = 128
DN_CONV = 4
DN_QK = DN_HEADS * DN_DK
DN_VW = DN_HEADS * DN_DV
DN_QKV = 2 * DN_QK + DN_VW

RET_HEADS = 4
RET_DK = 64
RET_DV = 128
RET_QK = RET_HEADS * RET_DK
RET_VW = RET_HEADS * RET_DV
ROPE_BASE = 10000.0

LRU_WIDTH = 512
LRU_BLOCKS = 4
LRU_BLOCK = LRU_WIDTH // LRU_BLOCKS
LRU_CONV = 4
LRU_C = 8.0

D_FF = 2816
FFN_CONV = 3

SUBLANES = 8
LANES = 128
VMEM_LIMIT_BYTES = 56 * 1024 * 1024

RET_W = 2 * RET_QK + 2 * RET_VW
LRU_W = 2 * LRU_WIDTH
OFF_DN = 0
OFF_RET = OFF_DN + DN_QKV
OFF_LRU = OFF_RET + RET_W
OFF_GATE = OFF_LRU + LRU_W
OFF_Z = OFF_GATE + N_BRANCH * D_MODEL
N_MAIN = OFF_Z + DN_VW
AB_ROWS = 16

TM_IN = 512
IN_COLS = 256
TT_MIX = 512
RET_CHUNK = 256
DN_BLK = 2 * CHUNK
TM_MERGE = 512
TT_FFN = 512
FFN_COLS = 256


def _params(sem):
    return pltpu.CompilerParams(dimension_semantics=sem, vmem_limit_bytes=VMEM_LIMIT_BYTES)


def _sigmoid(x):
    return 1.0 / (1.0 + jnp.exp(-x))


def _silu(x):
    return x * _sigmoid(x)


def _softplus(x):
    return jnp.maximum(x, 0.0) + jnp.log1p(jnp.exp(-jnp.abs(x)))


def _dot(a, b):
    return jnp.dot(a.astype(BF16), b.astype(BF16), preferred_element_type=F32)


def _dot_nt(a, b):
    return lax.dot_general(a.astype(BF16), b.astype(BF16), (((1,), (1,)), ((), ())),
                           preferred_element_type=F32)


def _dot_tn(a, b):
    return lax.dot_general(a.astype(BF16), b.astype(BF16), (((0,), (0,)), ((), ())),
                           preferred_element_type=F32)


def _dot_f32(a, b):
    return jnp.dot(a, b, preferred_element_type=F32, precision=HIGHEST)


def _iota2(shape, dim):
    return lax.broadcasted_iota(jnp.int32, shape, dim)


def _rows_to_slabs(scr, x):
    tt = x.shape[0]
    ns = tt // SUBLANES
    panels = x.shape[1] // LANES
    for p in range(panels):
        scr[p] = x[:, p * LANES:(p + 1) * LANES]
    return jnp.concatenate(
        [jnp.concatenate([scr[p, pl.ds(i, ns, stride=SUBLANES), :] for p in range(panels)], axis=1)
         for i in range(SUBLANES)], axis=0)


def _slabs_to_rows(scr, y):
    tt = y.shape[0]
    ns = tt // SUBLANES
    panels = y.shape[1] // LANES
    for p in range(panels):
        for i in range(SUBLANES):
            scr[p, pl.ds(i, ns, stride=SUBLANES), :] = y[i * ns:(i + 1) * ns, p * LANES:(p + 1) * LANES]
    return jnp.concatenate([scr[p] for p in range(panels)], axis=1)


def _causal_conv_slabs(tail_scr, w_ref, cols, xs, width):
    ns = xs.shape[0] // SUBLANES
    group0 = _iota2((ns, xs.shape[1]), 0) == 0
    slabs = [xs[i * ns:(i + 1) * ns, :] for i in range(SUBLANES)]
    back = [jnp.where(group0, tail_scr[i:i + 1, cols], pltpu.roll(slabs[i], 1, 0))
            for i in range(SUBLANES - (width - 1), SUBLANES)]
    for i in range(SUBLANES - (width - 1), SUBLANES):
        tail_scr[i:i + 1, cols] = slabs[i][ns - 1:ns, :]
    ext = back + slabs
    out = []
    for i in range(SUBLANES):
        y = ext[i] * w_ref[0:1, cols]
        for j in range(1, width):
            y = y + ext[i + j] * w_ref[j:j + 1, cols]
        out.append(y)
    return jnp.concatenate(out, axis=0)


def _in_proj_chunk_order():
    starts = list(range(0, N_MAIN, IN_COLS))
    heavy = [c for c in starts if c < OFF_DN + DN_QKV or OFF_LRU <= c < OFF_LRU + LRU_WIDTH]
    plain = [c for c in starts if c not in heavy]
    order, per = [], len(plain) // len(heavy)
    for i, c in enumerate(heavy):
        order.append(c)
        order.extend(plain[i * per:(i + 1) * per])
    order.extend(plain[len(heavy) * per:])
    return [(c, IN_COLS) for c in order]


def _in_proj_kernel(x_ref, g_ref, w_ref, wab_ref, wabt_ref, dncw_ref, lrucw_ref, lrucb_ref,
                    u_ref, ab_ref, abt_ref, dn_tail, lru_tail, perm_scr, *, tiles_per_seq):
    @pl.when(pl.program_id(0) % tiles_per_seq == 0)
    def _():
        dn_tail[...] = jnp.zeros(dn_tail.shape, F32)
        lru_tail[...] = jnp.zeros(lru_tail.shape, F32)

    x = x_ref[...]
    hb = (x * lax.rsqrt(jnp.mean(x * x, axis=-1, keepdims=True) + EPS) * g_ref[...]).astype(BF16)

    n_conv = 0
    for c0, width in _in_proj_chunk_order():
        p = jnp.dot(hb, w_ref[:, c0:c0 + width], preferred_element_type=F32)
        is_dn = c0 < OFF_DN + DN_QKV
        is_lru = OFF_LRU <= c0 < OFF_LRU + LRU_WIDTH
        if is_dn or is_lru:
            ps = _rows_to_slabs(perm_scr.at[2 * n_conv], p)
            if is_dn:
                cols = slice(c0 - OFF_DN, c0 - OFF_DN + IN_COLS)
                ps = _silu(_causal_conv_slabs(dn_tail, dncw_ref, cols, ps, DN_CONV))
                if c0 < OFF_DN + 2 * DN_QK:
                    scale = DN_DK ** -0.5 if c0 < OFF_DN + DN_QK else 1.0
                    heads = [ps[:, h0:h0 + DN_DK] for h0 in range(0, IN_COLS, DN_DK)]
                    ps = jnp.concatenate(
                        [t * (lax.rsqrt(jnp.sum(t * t, axis=-1, keepdims=True) + EPS) * scale) for t in heads],
                        axis=1)
            else:
                cols = slice(c0 - OFF_LRU, c0 - OFF_LRU + IN_COLS)
                ps = _causal_conv_slabs(lru_tail, lrucw_ref, cols, ps, LRU_CONV) + lrucb_ref[:, cols]
            p = _slabs_to_rows(perm_scr.at[2 * n_conv + 1], ps)
            n_conv += 1
        u_ref[:, c0:c0 + width] = p.astype(BF16)
    ab_ref[...] = jnp.dot(hb, wab_ref[...], preferred_element_type=F32)
    abt_ref[...] = lax.dot_general(wabt_ref[...], hb, (((1,), (1,)), ((), ())), preferred_element_type=F32)


def _in_proj(x2, g, w_main, w_ab, w_abt, dn_conv_w, lru_conv_w, lru_conv_b, layer, seq):
    m = x2.shape[0]
    tm = min(TM_IN, seq)
    return pl.pallas_call(
        functools.partial(_in_proj_kernel, tiles_per_seq=seq // tm),
        grid=(m // tm,),
        in_specs=[
            pl.BlockSpec((tm, D_MODEL), lambda i: (i, 0)),
            pl.BlockSpec((1, D_MODEL), lambda i: (0, 0)),
            pl.BlockSpec((None, D_MODEL, N_MAIN), lambda i: (layer, 0, 0), pipeline_mode=pl.Buffered(1)),
            pl.BlockSpec((None, D_MODEL, LANES), lambda i: (layer, 0, 0)),
            pl.BlockSpec((None, AB_ROWS, D_MODEL), lambda i: (layer, 0, 0)),
            pl.BlockSpec((DN_CONV, DN_QKV), lambda i: (0, 0)),
            pl.BlockSpec((LRU_CONV, LRU_WIDTH), lambda i: (0, 0)),
            pl.BlockSpec((1, LRU_WIDTH), lambda i: (0, 0)),
        ],
        out_specs=[
            pl.BlockSpec((tm, N_MAIN), lambda i: (i, 0)),
            pl.BlockSpec((tm, LANES), lambda i: (i, 0)),
            pl.BlockSpec((AB_ROWS, tm), lambda i: (0, i)),
        ],
        out_shape=[
            jax.ShapeDtypeStruct((m, N_MAIN), BF16),
            jax.ShapeDtypeStruct((m, LANES), F32),
            jax.ShapeDtypeStruct((AB_ROWS, m), F32),
        ],
        scratch_shapes=[
            pltpu.VMEM((SUBLANES, DN_QKV), F32),
            pltpu.VMEM((SUBLANES, LRU_WIDTH), F32),
            pltpu.VMEM((2 * (DN_QKV + LRU_WIDTH) // IN_COLS, IN_COLS // LANES, tm, LANES), F32),
        ],
        compiler_params=_params(("arbitrary",)),
        name="in_proj",
    )(x2, g, w_main, w_ab, w_abt, dn_conv_w, lru_conv_w, lru_conv_b)


def _dn_kernel(qkv_ref, z_ref, ab_ref, abt_ref, hpr_ref, hpc_ref, normw_ref, o_ref, s_scr, *, tt):
    a = qkv_ref[0].astype(F32)

    ab = ab_ref[...]
    abt = abt_ref[...]
    g_cols = -jnp.exp(hpr_ref[0:1, :]) * _softplus(ab + hpr_ref[1:2, :])
    beta_cols = _sigmoid(ab)
    g_rows = -jnp.exp(hpc_ref[:, 0:1]) * _softplus(abt + hpc_ref[:, 1:2])

    ri = _iota2((DN_BLK, DN_BLK), 0)
    ci = _iota2((DN_BLK, DN_BLK), 1)
    xr = ri ^ ci
    lower = ri > ci
    causal = (xr < CHUNK) & (ri >= ci)
    eye = jnp.where(ri == ci, 1.0, 0.0).astype(F32)

    cum_l = jnp.where(causal, 1.0, 0.0).astype(F32)
    cum_u = jnp.where((xr < CHUNK) & (ri <= ci), 1.0, 0.0).astype(F32)
    blocks = [slice(b * DN_BLK, (b + 1) * DN_BLK) for b in range(tt // DN_BLK)]
    gc_cols = jnp.concatenate([_dot_f32(cum_l, g_cols[r, :]) for r in blocks], axis=0)
    gc_rows = jnp.concatenate([_dot_f32(g_rows[:, r], cum_u) for r in blocks], axis=1)
    normw = normw_ref[...]
    z = z_ref[0].astype(F32)

    heads = range(DN_HEADS)
    q, k, v, gcc, beta, eg, k_beta = [], [], [], [], [], [], []
    for h in heads:
        q.append(a[:, h * DN_DK:(h + 1) * DN_DK])
        k.append(a[:, DN_QK + h * DN_DK:DN_QK + (h + 1) * DN_DK])
        v.append(a[:, 2 * DN_QK + h * DN_DV:2 * DN_QK + (h + 1) * DN_DV])
        gcc.append(gc_cols[:, h:h + 1])
        beta.append(beta_cols[:, DN_HEADS + h:DN_HEADS + h + 1])
        eg.append(jnp.exp(gcc[h]))
        k_beta.append(k[h] * beta[h])
    rhs = [jnp.concatenate([v[h] * beta[h], k_beta[h] * eg[h]], axis=1) for h in heads]
    q_dec = [q[h] * eg[h] for h in heads]

    probs = [(h, b) for h in heads for b in range(tt // DN_BLK)]
    rows_of = lambda b: slice(b * DN_BLK, (b + 1) * DN_BLK)
    decay = [jnp.where(causal, jnp.exp(gcc[h][rows_of(b)] - gc_rows[h:h + 1, rows_of(b)]), 0.0)
             for h, b in probs]
    low = [jnp.where(lower, _dot_nt(k_beta[h][rows_of(b)], k[h][rows_of(b)]) * decay[i], 0.0)
           for i, (h, b) in enumerate(probs)]
    attn = [_dot_nt(q[h][rows_of(b)], k[h][rows_of(b)]) * decay[i] for i, (h, b) in enumerate(probs)]
    n = range(len(probs))
    d4 = [jnp.where(xr < 4, low[i], 0.0) for i in n]
    d4sq = [_dot(d4[i], d4[i]) for i in n]
    t_inv = [(eye - d4[i]) + _dot(eye - d4[i], d4sq[i]) for i in n]
    for lv in range(2, 6):
        off_t = [_dot(jnp.where((xr >> lv) == 1, low[i], 0.0), t_inv[i]) for i in n]
        t_inv = [t_inv[i] - _dot(t_inv[i], off_t[i]) for i in n]
    uw = {hb: _dot(t_inv[i], rhs[hb[0]][rows_of(hb[1])]) for i, hb in enumerate(probs)}
    attn = {hb: attn[i] for i, hb in enumerate(probs)}

    state = [s_scr[h] for h in heads]
    per_blk = DN_BLK // CHUNK
    for c in range(tt // CHUNK):
        rows = slice(c * CHUNK, (c + 1) * CHUNK)
        b = c // per_blk
        loc = slice((c % per_blk) * CHUNK, (c % per_blk + 1) * CHUNK)
        g_last = [gcc[h][(c + 1) * CHUNK - 1:(c + 1) * CHUNK, :] for h in heads]
        v_new = [uw[h, b][loc, :DN_DV] - _dot(uw[h, b][loc, DN_DV:], state[h]) for h in heads]
        out = [_dot(q_dec[h][rows], state[h]) + _dot(attn[h, b][loc, loc], v_new[h]) for h in heads]
        k_dec = [k[h][rows] * jnp.exp(g_last[h] - gcc[h][rows]) for h in heads]
        state = [state[h] * jnp.exp(g_last[h]) + _dot_tn(k_dec[h], v_new[h]) for h in heads]
        for h in heads:
            o = out[h] * lax.rsqrt(jnp.mean(out[h] * out[h], axis=-1, keepdims=True) + EPS)
            o = o * normw * _silu(z[rows, h * DN_DV:(h + 1) * DN_DV])
            o_ref[0, rows, h * DN_DV:(h + 1) * DN_DV] = o.astype(BF16)
    for h in heads:
        s_scr[h] = state[h]


def _ret_kernel(r_ref, cos_ref, sin_ref, dmask_ref, xi_ref, zeta_ref, gch_ref, o_ref, s_scr, *, tt):
    r = r_ref[0].astype(F32)
    cos = cos_ref[0]
    sin_signed = sin_ref[0]
    first_half = (_iota2((tt, LANES), 1) % RET_DK) < (RET_DK // 2)

    def rotary(t):
        swapped = jnp.where(first_half, pltpu.roll(t, LANES - RET_DK // 2, 1),
                            pltpu.roll(t, RET_DK // 2, 1))
        return t * cos + swapped * sin_signed

    q = jnp.concatenate([rotary(r[:, i * LANES:(i + 1) * LANES]) for i in range(RET_QK // LANES)], axis=1)
    k = jnp.concatenate([rotary(r[:, RET_QK + i * LANES:RET_QK + (i + 1) * LANES])
                         for i in range(RET_QK // LANES)], axis=1) * (RET_DK ** -0.5)
    v = r[:, 2 * RET_QK:2 * RET_QK + RET_VW]
    gate = r[:, 2 * RET_QK + RET_VW:]

    heads = range(RET_HEADS)
    state = [s_scr[h] for h in heads]
    chunk = dmask_ref.shape[1]
    for r0 in range(0, tt, chunk):
        rows = slice(r0, r0 + chunk)
        qh = [q[rows, h * RET_DK:(h + 1) * RET_DK] for h in heads]
        kh = [k[rows, h * RET_DK:(h + 1) * RET_DK] for h in heads]
        vh = [v[rows, h * RET_DV:(h + 1) * RET_DV] for h in heads]
        scores = [_dot_nt(qh[h], kh[h]) * dmask_ref[h] for h in heads]
        cross = [_dot(qh[h], state[h]) * xi_ref[:, h:h + 1] for h in heads]
        inner = [_dot(scores[h], vh[h]) for h in heads]
        state = [state[h] * gch_ref[h:h + 1, :] + _dot_tn(kh[h] * zeta_ref[:, h:h + 1], vh[h]) for h in heads]
        for h in heads:
            out = cross[h] + inner[h]
            mu = jnp.mean(out, axis=-1, keepdims=True)
            cen = out - mu
            var = jnp.mean(cen * cen, axis=-1, keepdims=True)
            out = cen * lax.rsqrt(var + EPS) * _silu(gate[rows, h * RET_DV:(h + 1) * RET_DV])
            o_ref[0, rows, h * RET_DV:(h + 1) * RET_DV] = out.astype(BF16)
    for h in heads:
        s_scr[h] = state[h]


def _lru_kernel(l_ref, wax_ref, ba_ref, bx_ref, lam_ref, o_ref, h_scr, perm_scr, *, tt):
    l = l_ref[0]
    xc = _rows_to_slabs(perm_scr.at[0], l[:, 0:LRU_WIDTH].astype(F32))
    rs, gs = [], []
    for n in range(LRU_BLOCKS):
        xb = xc[:, n * LRU_BLOCK:(n + 1) * LRU_BLOCK].astype(BF16)
        both = jnp.dot(xb, wax_ref[n], preferred_element_type=F32)
        rs.append(both[:, :LRU_BLOCK])
        gs.append(both[:, LRU_BLOCK:])
    rgate = _sigmoid(jnp.concatenate(rs, axis=1) + ba_ref[...])
    igate = _sigmoid(jnp.concatenate(gs, axis=1) + bx_ref[...])
    log_a = (-LRU_C) * rgate * _softplus(-lam_ref[...])
    a = jnp.exp(log_a)
    one_minus = -jnp.tanh(log_a) * (a * a + 1.0)
    bv = one_minus * lax.rsqrt(jnp.maximum(one_minus, F32_MIN_NORMAL)) * (igate * xc)

    ns = tt // SUBLANES
    hs = [bv[0:ns, :]]
    ps = [a[0:ns, :]]
    for i in range(1, SUBLANES):
        ai = a[i * ns:(i + 1) * ns, :]
        hs.append(ai * hs[-1] + bv[i * ns:(i + 1) * ns, :])
        ps.append(ai * ps[-1])
    pg, hg = ps[-1], hs[-1]
    row = _iota2((ns, LRU_WIDTH), 0)
    s = 1
    while s < ns:
        keep = row >= s
        p_prev = jnp.where(keep, pltpu.roll(pg, s, 0), 1.0)
        h_prev = jnp.where(keep, pltpu.roll(hg, s, 0), 0.0)
        hg = pg * h_prev + hg
        pg = pg * p_prev
        s *= 2
    h0 = h_scr[...]
    after = pg * h0 + hg
    h_scr[...] = after[ns - 1:ns, :]
    before = jnp.where(row == 0, h0, pltpu.roll(after, 1, 0))
    hseq = _slabs_to_rows(perm_scr.at[1], jnp.concatenate([hs[i] + ps[i] * before for i in range(SUBLANES)],
                                                          axis=0))
    g = l[:, LRU_WIDTH:].astype(F32)
    gelu = 0.5 * g * (1.0 + jnp.tanh(math.sqrt(2.0 / math.pi) * (g + 0.044715 * (g * g * g))))
    o_ref[0] = (hseq * gelu).astype(BF16)


def _rope_kernel(pos_ref, inv_ref, cos_ref, sin_ref):
    ang = pos_ref[0].astype(F32) * inv_ref[...]
    first_half = (_iota2(ang.shape, 1) % RET_DK) < (RET_DK // 2)
    sin = jnp.sin(ang)
    cos_ref[0] = jnp.cos(ang)
    sin_ref[0] = jnp.where(first_half, -sin, sin)


def _rope_tables(positions, inv_row):
    b, t = positions.shape
    tt = min(TT_FFN, t)
    spec = pl.BlockSpec((1, tt, LANES), lambda i, j: (i, j, 0))
    shape = jax.ShapeDtypeStruct((b, t, LANES), F32)
    pos3 = jnp.broadcast_to(positions[:, :, None], (b, t, LANES))
    return pl.pallas_call(
        _rope_kernel,
        grid=(b, t // tt),
        in_specs=[spec, pl.BlockSpec((1, LANES), lambda i, j: (0, 0))],
        out_specs=[spec, spec],
        out_shape=[shape, shape],
        compiler_params=_params(("parallel", "parallel")),
        name="rope_tables",
    )(pos3, inv_row)


N_DN_IN, N_RET_IN, N_LRU_IN = 7, 7, 5


def _mixer_kernel(*refs, tt):
    dn_in = refs[:N_DN_IN]
    ret_in = refs[N_DN_IN:N_DN_IN + N_RET_IN]
    lru_in = refs[N_DN_IN + N_RET_IN:N_DN_IN + N_RET_IN + N_LRU_IN]
    ya_ref, yb_ref, yc_ref, dn_state, ret_state, lru_state, lru_perm = refs[N_DN_IN + N_RET_IN + N_LRU_IN:]

    @pl.when(pl.program_id(1) == 0)
    def _():
        for scr in (dn_state, ret_state, lru_state):
            scr[...] = jnp.zeros(scr.shape, F32)

    _ret_kernel(*ret_in, yb_ref, ret_state, tt=tt)
    _dn_kernel(*dn_in, ya_ref, dn_state, tt=tt)
    _lru_kernel(*lru_in, yc_ref, lru_state, lru_perm, tt=tt)


def _mixer(u3, ab, abt, rope_cos, rope_sin, ret_tables, dn_p, lru_p):
    b, t, _ = u3.shape
    tt = min(TT_MIX, t)
    nt = t // tt
    full2 = lambda i, j: (0, 0)
    full3 = lambda i, j: (0, 0, 0)
    rc = ret_tables[0].shape[1]
    dn_specs = [
        pl.BlockSpec((1, tt, DN_QKV), lambda i, j: (i, j, OFF_DN // DN_QKV)),
        pl.BlockSpec((1, tt, DN_VW), lambda i, j: (i, j, OFF_Z // DN_VW)),
        pl.BlockSpec((tt, LANES), lambda i, j: (i * nt + j, 0)),
        pl.BlockSpec((AB_ROWS, tt), lambda i, j: (0, i * nt + j)),
        pl.BlockSpec((SUBLANES, LANES), full2),
        pl.BlockSpec((AB_ROWS, LANES), full2),
        pl.BlockSpec((1, DN_DV), full2),
    ]
    ret_specs = [
        pl.BlockSpec((1, tt, RET_W), lambda i, j: (i, j, OFF_RET // RET_W)),
        pl.BlockSpec((1, tt, LANES), lambda i, j: (i, j, 0)),
        pl.BlockSpec((1, tt, LANES), lambda i, j: (i, j, 0)),
        pl.BlockSpec((RET_HEADS, rc, rc), full3),
        pl.BlockSpec((rc, LANES), full2),
        pl.BlockSpec((rc, LANES), full2),
        pl.BlockSpec((SUBLANES, LANES), full2),
    ]
    lru_specs = [
        pl.BlockSpec((1, tt, LRU_W), lambda i, j: (i, j, OFF_LRU // LRU_W)),
        pl.BlockSpec((LRU_BLOCKS, LRU_BLOCK, 2 * LRU_BLOCK), full3),
        pl.BlockSpec((1, LRU_WIDTH), full2),
        pl.BlockSpec((1, LRU_WIDTH), full2),
        pl.BlockSpec((1, LRU_WIDTH), full2),
    ]
    assert (len(dn_specs), len(ret_specs), len(lru_specs)) == (N_DN_IN, N_RET_IN, N_LRU_IN)
    out_spec = pl.BlockSpec((1, tt, BRANCH_WIDTH), lambda i, j: (i, j, 0))
    out_shape = jax.ShapeDtypeStruct((b, t, BRANCH_WIDTH), BF16)
    return pl.pallas_call(
        functools.partial(_mixer_kernel, tt=tt),
        grid=(b, nt),
        in_specs=dn_specs + ret_specs + lru_specs,
        out_specs=[out_spec] * N_BRANCH,
        out_shape=[out_shape] * N_BRANCH,
        scratch_shapes=[
            pltpu.VMEM((DN_HEADS, DN_DK, DN_DV), F32),
            pltpu.VMEM((RET_HEADS, RET_DK, RET_DV), F32),
            pltpu.VMEM((1, LRU_WIDTH), F32),
            pltpu.VMEM((2, LRU_WIDTH // LANES, tt, LANES), F32),
        ],
        compiler_params=_params(("parallel", "arbitrary")),
        name="mixer",
    )(u3, u3, ab, abt, *dn_p, u3, rope_cos, rope_sin, *ret_tables, u3, *lru_p)


def _merge_kernel(ya_ref, yb_ref, yc_ref, g0_ref, g1_ref, g2_ref, x_ref, wb_ref, wo_ref, o_ref):
    merged = None
    for y_ref, g_ref, n in ((ya_ref, g0_ref, 0), (yb_ref, g1_ref, 1), (yc_ref, g2_ref, 2)):
        term = _sigmoid(g_ref[...].astype(F32)) * jnp.dot(y_ref[...], wb_ref[n], preferred_element_type=F32)
        merged = term if merged is None else merged + term
    o_ref[...] = x_ref[...] + jnp.dot(merged.astype(BF16), wo_ref[...], preferred_element_type=F32)


def _merge(ya, yb, yc, u2, x2, wb, wo, layer):
    m = x2.shape[0]
    tm = min(TM_MERGE, m)
    gate_blk = OFF_GATE // D_MODEL
    yspec = pl.BlockSpec((tm, BRANCH_WIDTH), lambda i: (i, 0))
    return pl.pallas_call(
        _merge_kernel,
        grid=(m // tm,),
        in_specs=[
            yspec, yspec, yspec,
            pl.BlockSpec((tm, D_MODEL), lambda i: (i, gate_blk)),
            pl.BlockSpec((tm, D_MODEL), lambda i: (i, gate_blk + 1)),
            pl.BlockSpec((tm, D_MODEL), lambda i: (i, gate_blk + 2)),
            pl.BlockSpec((tm, D_MODEL), lambda i: (i, 0)),
            pl.BlockSpec((None, N_BRANCH, BRANCH_WIDTH, D_MODEL), lambda i: (layer, 0, 0, 0)),
            pl.BlockSpec((None, D_MODEL, D_MODEL), lambda i: (layer, 0, 0)),
        ],
        out_specs=pl.BlockSpec((tm, D_MODEL), lambda i: (i, 0)),
        out_shape=jax.ShapeDtypeStruct((m, D_MODEL), F32),
        compiler_params=_params(("parallel",)),
        name="merge",
    )(ya, yb, yc, u2, u2, u2, x2, wb, wo)


def _ffn_kernel(x_ref, g_ref, wup_ref, cw_ref, cb_ref, wdn_ref, fg_ref, o_ref,
                tail_scr, act_scr, perm_scr, *, tt, final):
    @pl.when(pl.program_id(1) == 0)
    def _():
        tail_scr[...] = jnp.zeros(tail_scr.shape, F32)

    x = _rows_to_slabs(perm_scr, x_ref[0])
    hb = (x * lax.rsqrt(jnp.mean(x * x, axis=-1, keepdims=True) + EPS) * g_ref[...]).astype(BF16)

    def conv(pre, col0):
        cols = slice(col0, col0 + FFN_COLS)
        return _causal_conv_slabs(tail_scr, cw_ref, cols, pre, FFN_CONV) + cb_ref[:, cols]

    for c in range(D_FF // FFN_COLS):
        c0 = c * FFN_COLS
        gate = conv(jnp.dot(hb, wup_ref[:, c0:c0 + FFN_COLS], preferred_element_type=F32), c0)
        val = conv(jnp.dot(hb, wup_ref[:, D_FF + c0:D_FF + c0 + FFN_COLS], preferred_element_type=F32),
                   D_FF + c0)
        act_scr[:, c0:c0 + FFN_COLS] = (_silu(gate) * val).astype(BF16)

    y = x + jnp.dot(act_scr[...], wdn_ref[...], preferred_element_type=F32)
    if final:
        y = y * lax.rsqrt(jnp.mean(y * y, axis=-1, keepdims=True) + EPS) * fg_ref[...]
    o_ref[0] = _slabs_to_rows(perm_scr, y)


def _conv_ffn(x3, g, w_up, conv_w, conv_b, w_down, final_g, layer, final):
    b, t, _ = x3.shape
    tt = min(TT_FFN, t)
    full2 = lambda i, j: (0, 0)
    of_layer = lambda i, j: (layer, 0, 0)
    return pl.pallas_call(
        functools.partial(_ffn_kernel, tt=tt, final=final),
        grid=(b, t // tt),
        in_specs=[
            pl.BlockSpec((1, tt, D_MODEL), lambda i, j: (i, j, 0)),
            pl.BlockSpec((1, D_MODEL), full2),
            pl.BlockSpec((None, D_MODEL, 2 * D_FF), of_layer, pipeline_mode=pl.Buffered(1)),
            pl.BlockSpec((FFN_CONV, 2 * D_FF), full2),
            pl.BlockSpec((1, 2 * D_FF), full2),
            pl.BlockSpec((None, D_FF, D_MODEL), of_layer, pipeline_mode=pl.Buffered(1)),
            pl.BlockSpec((1, D_MODEL), full2),
        ],
        out_specs=pl.BlockSpec((1, tt, D_MODEL), lambda i, j: (i, j, 0)),
        out_shape=jax.ShapeDtypeStruct((b, t, D_MODEL), F32),
        scratch_shapes=[
            pltpu.VMEM((SUBLANES, 2 * D_FF), F32),
            pltpu.VMEM((tt, D_FF), BF16),
            pltpu.VMEM((D_MODEL // LANES, tt, LANES), F32),
        ],
        compiler_params=_params(("parallel", "arbitrary")),
        name="conv_ffn",
    )(x3, g, w_up, conv_w, conv_b, w_down, final_g)


def _retention_tables(chunk):
    log_gamma = np.log(1.0 - 2.0 ** (-5.0 - np.arange(RET_HEADS, dtype=np.float64)))
    idx = np.arange(chunk, dtype=np.float64)
    dist = idx[:, None] - idx[None, :]
    causal = dist >= 0
    dmask = np.where(causal, np.exp(np.where(causal, dist, 0.0) * log_gamma[:, None, None]), 0.0)
    xi = np.zeros((chunk, LANES), np.float64)
    zeta = np.zeros((chunk, LANES), np.float64)
    xi[:, :RET_HEADS] = np.exp((idx[:, None] + 1.0) * log_gamma[None, :])
    zeta[:, :RET_HEADS] = np.exp((chunk - 1.0 - idx[:, None]) * log_gamma[None, :])
    gch = np.zeros((SUBLANES, LANES), np.float64)
    gch[:RET_HEADS, :] = np.exp(chunk * log_gamma)[:, None]
    return tuple(jnp.asarray(t, F32) for t in (dmask, xi, zeta, gch))


def _rotary_inv_row():
    half = RET_DK // 2
    inv = (ROPE_BASE ** (-np.arange(half, dtype=np.float32) / half)).astype(np.float32)
    return jnp.asarray(np.tile(inv, LANES // half)[None, :], F32)


def _split_w_in(w_in):
    sizes = (DN_QKV, DN_HEADS, DN_HEADS, DN_VW, RET_QK, RET_QK, RET_VW, RET_VW, LRU_WIDTH, LRU_WIDTH,
             N_BRANCH * D_MODEL)
    offs = np.concatenate([[0], np.cumsum(sizes)])
    w_in = w_in.astype(BF16)
    part = lambda n: w_in[..., int(offs[n]):int(offs[n + 1])]
    main = jnp.concatenate([part(0), part(4), part(5), part(6), part(7), part(8), part(9), part(10), part(3)],
                           axis=-1)
    ab = jnp.concatenate([part(1), part(2)], axis=-1)
    w_ab = jnp.pad(ab, ((0, 0), (0, 0), (0, LANES - 2 * DN_HEADS)))
    w_abt = jnp.pad(jnp.swapaxes(ab, -1, -2), ((0, 0), (0, AB_ROWS - 2 * DN_HEADS), (0, 0)))
    return main, w_ab, w_abt


def kernel(x, positions, attn_norm, w_in, dn_conv_w, dn_a_log, dn_dt_bias, dn_norm_w, lru_conv_w, lru_conv_b,
           lru_wa, lru_ba, lru_wx, lru_bx, lru_lambda, w_branch, w_out, ffn_norm, w_up, ffn_conv_w, ffn_conv_b,
           w_down, final_norm):
    bsz, seq, _ = x.shape
    m = bsz * seq
    depth = w_in.shape[0]
    ret_tables = _retention_tables(min(RET_CHUNK, seq))
    rope_cos, rope_sin = _rope_tables(positions, _rotary_inv_row())
    w_main, w_ab, w_abt = _split_w_in(w_in)
    hp = jnp.stack([dn_a_log, dn_dt_bias], axis=1).astype(F32)
    hpr = jnp.pad(hp, ((0, 0), (0, SUBLANES - 2), (0, LANES - DN_HEADS)))
    hpc = jnp.pad(jnp.swapaxes(hp, 1, 2), ((0, 0), (0, AB_ROWS - DN_HEADS), (0, LANES - 2)))
    wb = w_branch.astype(BF16)
    wo = w_out.astype(BF16)
    wup = w_up.astype(BF16)
    wdn = w_down.astype(BF16)
    wax = jnp.concatenate([lru_wa, lru_wx], axis=-1).astype(BF16)

    x2 = x.reshape(m, D_MODEL)
    for l in range(depth):
        u2, ab, abt = _in_proj(x2, attn_norm[l][None, :], w_main, w_ab, w_abt, dn_conv_w[l], lru_conv_w[l],
                               lru_conv_b[l][None, :], l, seq)
        dn_p = (hpr[l], hpc[l], dn_norm_w[l][None, :])
        lru_p = (wax[l], lru_ba[l].reshape(1, LRU_WIDTH), lru_bx[l].reshape(1, LRU_WIDTH),
                 lru_lambda[l][None, :])
        ya, yb, yc = _mixer(u2.reshape(bsz, seq, N_MAIN), ab, abt, rope_cos, rope_sin, ret_tables, dn_p, lru_p)
        x2 = _merge(ya.reshape(m, DN_VW), yb.reshape(m, RET_VW), yc.reshape(m, LRU_WIDTH), u2, x2, wb, wo, l)
        x3 = _conv_ffn(x2.reshape(bsz, seq, D_MODEL), ffn_norm[l][None, :], wup, ffn_conv_w[l],
                       ffn_conv_b[l][None, :], wdn, final_norm[None, :], l, final=(l == depth - 1))
        x2 = x3.reshape(m, D_MODEL)
    return x2.reshape(bsz, seq, D_MODEL)
```

```python
import functools
import math

import numpy as np
import jax
import jax.numpy as jnp
from jax import lax
from jax.experimental import pallas as pl
from jax.experimental.pallas import tpu as pltpu

F32 = jnp.float32
BF16 = jnp.bfloat16
F32_MIN_NORMAL = float(np.finfo(np.float32).tiny)

D_MODEL = 1024
CHUNK = 64
EPS = 1e-6
N_BRANCH = 3
BRANCH_WIDTH = 512

DN_HEADS = 4
DN_DK = 128
DN_DV = 128
DN_CONV = 4
DN_QK = DN_HEADS * DN_DK
DN_VW = DN_HEADS * DN_DV
DN_QKV = 2 * DN_QK + DN_VW

RET_HEADS = 4
RET_DK = 64
RET_DV = 128
RET_QK = RET_HEADS * RET_DK
RET_VW = RET_HEADS * RET_DV
ROPE_BASE = 10000.0

LRU_WIDTH = 512
LRU_BLOCKS = 4
LRU_BLOCK = LRU_WIDTH // LRU_BLOCKS
LRU_CONV = 4
LRU_C = 8.0

D_FF = 2816
FFN_CONV = 3

SUBLANES = 8
LANES = 128
V7X_VMEM_BYTES = 64 * 1024 * 1024
VMEM_LIMIT_BYTES = V7X_VMEM_BYTES * 7 // 8

RET_W = 2 * RET_QK + 2 * RET_VW
LRU_W = 2 * LRU_WIDTH
OFF_DN = 0
OFF_RET = OFF_DN + DN_QKV
OFF_LRU = OFF_RET + RET_W
OFF_GATE = OFF_LRU + LRU_W
OFF_Z = OFF_GATE + N_BRANCH * D_MODEL
N_MAIN = OFF_Z + DN_VW
AB_ROWS = 16

TM_IN = 512
IN_COLS = 256
TT_MIX = 512
RET_CHUNK = 256
DN_BLK = 2 * CHUNK
TM_MERGE = 512
TT_FFN = 512
FFN_COLS = 256


def _params(sem):
    return pltpu.CompilerParams(dimension_semantics=sem, vmem_limit_bytes=VMEM_LIMIT_BYTES)


def _sigmoid(x):
    return 1.0 / (1.0 + jnp.exp(-x))


def _silu(x):
    return x * _sigmoid(x)


def _softplus(x):
    return jnp.maximum(x, 0.0) + jnp.log1p(jnp.exp(-jnp.abs(x)))


def _dot(a, b):
    return jnp.dot(a.astype(BF16), b.astype(BF16), preferred_element_type=F32)


def _dot_nt(a, b):
    return lax.dot_general(a.astype(BF16), b.astype(BF16), (((1,), (1,)), ((), ())),
                           preferred_element_type=F32)


def _dot_tn(a, b):
    return lax.dot_general(a.astype(BF16), b.astype(BF16), (((0,), (0,)), ((), ())),
                           preferred_element_type=F32)


def _iota2(shape, dim):
    return lax.broadcasted_iota(jnp.int32, shape, dim)


def _rows_to_slabs(scr, x):
    tt = x.shape[0]
    ns = tt // SUBLANES
    panels = x.shape[1] // LANES
    for p in range(panels):
        scr[p] = x[:, p * LANES:(p + 1) * LANES]
    return jnp.concatenate(
        [jnp.concatenate([scr[p, pl.ds(i, ns, stride=SUBLANES), :] for p in range(panels)], axis=1)
         for i in range(SUBLANES)], axis=0)


def _slabs_to_rows(scr, y):
    tt = y.shape[0]
    ns = tt // SUBLANES
    panels = y.shape[1] // LANES
    for p in range(panels):
        for i in range(SUBLANES):
            scr[p, pl.ds(i, ns, stride=SUBLANES), :] = y[i * ns:(i + 1) * ns, p * LANES:(p + 1) * LANES]
    return jnp.concatenate([scr[p] for p in range(panels)], axis=1)


def _causal_conv_slabs(tail_scr, w_ref, cols, xs, width):
    ns = xs.shape[0] // SUBLANES
    group0 = _iota2((ns, xs.shape[1]), 0) == 0
    slabs = [xs[i * ns:(i + 1) * ns, :] for i in range(SUBLANES)]
    back = [jnp.where(group0, tail_scr[i:i + 1, cols], pltpu.roll(slabs[i], 1, 0))
            for i in range(SUBLANES - (width - 1), SUBLANES)]
    for i in range(SUBLANES - (width - 1), SUBLANES):
        tail_scr[i:i + 1, cols] = slabs[i][ns - 1:ns, :]
    ext = back + slabs
    out = []
    for i in range(SUBLANES):
        y = ext[i] * w_ref[0:1, cols]
        for j in range(1, width):
            y = y + ext[i + j] * w_ref[j:j + 1, cols]
        out.append(y)
    return jnp.concatenate(out, axis=0)


def _in_proj_chunk_order():
    starts = list(range(0, N_MAIN, IN_COLS))
    heavy = [c for c in starts if c < OFF_DN + DN_QKV or OFF_LRU <= c < OFF_LRU + LRU_WIDTH]
    plain = [c for c in starts if c not in heavy]
    order, per = [], len(plain) // len(heavy)
    for i, c in enumerate(heavy):
        order.append(c)
        order.extend(plain[i * per:(i + 1) * per])
    order.extend(plain[len(heavy) * per:])
    return [(c, IN_COLS) for c in order]


def _in_proj_kernel(x_ref, g_ref, w_ref, wab_ref, wabt_ref, dncw_ref, lrucw_ref, lrucb_ref,
                    u_ref, ab_ref, abt_ref, dn_tail, lru_tail, perm_scr, *, tiles_per_seq):
    @pl.when(pl.program_id(0) % tiles_per_seq == 0)
    def _():
        dn_tail[...] = jnp.zeros(dn_tail.shape, F32)
        lru_tail[...] = jnp.zeros(lru_tail.shape, F32)

    x = x_ref[...]
    hb = (x * lax.rsqrt(jnp.mean(x * x, axis=-1, keepdims=True) + EPS) * g_ref[...]).astype(BF16)

    n_conv = 0
    for c0, width in _in_proj_chunk_order():
        p = jnp.dot(hb, w_ref[:, c0:c0 + width], preferred_element_type=F32)
        is_dn = c0 < OFF_DN + DN_QKV
        is_lru = OFF_LRU <= c0 < OFF_LRU + LRU_WIDTH
        if is_dn or is_lru:
            ps = _rows_to_slabs(perm_scr.at[2 * n_conv], p)
            if is_dn:
                cols = slice(c0 - OFF_DN, c0 - OFF_DN + IN_COLS)
                ps = _silu(_causal_conv_slabs(dn_tail, dncw_ref, cols, ps, DN_CONV))
                if c0 < OFF_DN + 2 * DN_QK:
                    scale = DN_DK ** -0.5 if c0 < OFF_DN + DN_QK else 1.0
                    heads = [ps[:, h0:h0 + DN_DK] for h0 in range(0, IN_COLS, DN_DK)]
                    ps = jnp.concatenate(
                        [t * (lax.rsqrt(jnp.sum(t * t, axis=-1, keepdims=True) + EPS) * scale) for t in heads],
                        axis=1)
            else:
                cols = slice(c0 - OFF_LRU, c0 - OFF_LRU + IN_COLS)
                ps = _causal_conv_slabs(lru_tail, lrucw_ref, cols, ps, LRU_CONV) + lrucb_ref[:, cols]
            p = _slabs_to_rows(perm_scr.at[2 * n_conv + 1], ps)
            n_conv += 1
        u_ref[:, c0:c0 + width] = p.astype(BF16)
    ab_ref[...] = jnp.dot(hb, wab_ref[...], preferred_element_type=F32)
    abt_ref[...] = lax.dot_general(wabt_ref[...], hb, (((1,), (1,)), ((), ())), preferred_element_type=F32)


def _in_proj(x2, g, w_main, w_ab, w_abt, dn_conv_w, lru_conv_w, lru_conv_b, layer, seq):
    m = x2.shape[0]
    tm = min(TM_IN, seq)
    return pl.pallas_call(
        functools.partial(_in_proj_kernel, tiles_per_seq=seq // tm),
        grid=(m // tm,),
        in_specs=[
            pl.BlockSpec((tm, D_MODEL), lambda i: (i, 0)),
            pl.BlockSpec((1, D_MODEL), lambda i: (0, 0)),
            pl.BlockSpec((None, D_MODEL, N_MAIN), lambda i: (layer, 0, 0), pipeline_mode=pl.Buffered(1)),
            pl.BlockSpec((None, D_MODEL, LANES), lambda i: (layer, 0, 0)),
            pl.BlockSpec((None, AB_ROWS, D_MODEL), lambda i: (layer, 0, 0)),
            pl.BlockSpec((DN_CONV, DN_QKV), lambda i: (0, 0)),
            pl.BlockSpec((LRU_CONV, LRU_WIDTH), lambda i: (0, 0)),
            pl.BlockSpec((1, LRU_WIDTH), lambda i: (0, 0)),
        ],
        out_specs=[
            pl.BlockSpec((tm, N_MAIN), lambda i: (i, 0)),
            pl.BlockSpec((tm, LANES), lambda i: (i, 0)),
            pl.BlockSpec((AB_ROWS, tm), lambda i: (0, i)),
        ],
        out_shape=[
            jax.ShapeDtypeStruct((m, N_MAIN), BF16),
            jax.ShapeDtypeStruct((m, LANES), F32),
            jax.ShapeDtypeStruct((AB_ROWS, m), F32),
        ],
        scratch_shapes=[
            pltpu.VMEM((SUBLANES, DN_QKV), F32),
            pltpu.VMEM((SUBLANES, LRU_WIDTH), F32),
            pltpu.VMEM((2 * (DN_QKV + LRU_WIDTH) // IN_COLS, IN_COLS // LANES, tm, LANES), F32),
        ],
        compiler_params=_params(("arbitrary",)),
        name="in_proj",
    )(x2, g, w_main, w_ab, w_abt, dn_conv_w, lru_conv_w, lru_conv_b)


def _second_rows(x, half):
    return jnp.concatenate([x[r:r + half, :] for r in range(half, x.shape[0], 2 * half)], axis=0)


def _set_second_rows(x, rows, half):
    parts = []
    for j, r in enumerate(range(0, x.shape[0], 2 * half)):
        parts += [x[r:r + half, :], rows[j * half:(j + 1) * half, :]]
    return jnp.concatenate(parts, axis=0)


def _dn_kernel(qkv_ref, z_ref, ab_ref, abt_ref, hpr_ref, hpc_ref, normw_ref, o_ref, s_scr, *, tt):
    a = qkv_ref[0].astype(F32)

    ab = ab_ref[...]
    abt = abt_ref[...]
    g_cols = -jnp.exp(hpr_ref[0:1, :]) * _softplus(ab + hpr_ref[1:2, :])
    beta_cols = _sigmoid(ab)
    g_rows = -jnp.exp(hpc_ref[:, 0:1]) * _softplus(abt + hpc_ref[:, 1:2])

    ri = _iota2((DN_BLK, DN_BLK), 0)
    ci = _iota2((DN_BLK, DN_BLK), 1)
    xr = ri ^ ci
    lower = ri > ci
    causal = (xr < CHUNK) & (ri >= ci)
    eye = jnp.where(ri == ci, 1.0, 0.0).astype(F32)

    gc_cols, gc_rows = g_cols, g_rows
    t_col = _iota2(g_cols.shape, 0) % CHUNK
    t_row = _iota2(g_rows.shape, 1) % CHUNK
    s = 1
    while s < CHUNK:
        gc_cols = gc_cols + jnp.where(t_col >= s, pltpu.roll(gc_cols, s, 0), 0.0)
        gc_rows = gc_rows + jnp.where(t_row >= s, pltpu.roll(gc_rows, s, 1), 0.0)
        s *= 2
    normw = normw_ref[...]
    z = z_ref[0].astype(F32)

    heads = range(DN_HEADS)
    q, k, v, gcc, beta, eg, k_beta = [], [], [], [], [], [], []
    for h in heads:
        q.append(a[:, h * DN_DK:(h + 1) * DN_DK])
        k.append(a[:, DN_QK + h * DN_DK:DN_QK + (h + 1) * DN_DK])
        v.append(a[:, 2 * DN_QK + h * DN_DV:2 * DN_QK + (h + 1) * DN_DV])
        gcc.append(gc_cols[:, h:h + 1])
        beta.append(beta_cols[:, DN_HEADS + h:DN_HEADS + h + 1])
        eg.append(jnp.exp(gcc[h]))
        k_beta.append(k[h] * beta[h])
    rhs = [jnp.concatenate([v[h] * beta[h], k_beta[h] * eg[h]], axis=1) for h in heads]
    q_dec = [q[h] * eg[h] for h in heads]

    probs = [(h, b) for h in heads for b in range(tt // DN_BLK)]
    rows_of = lambda b: slice(b * DN_BLK, (b + 1) * DN_BLK)
    decay = [jnp.where(causal, jnp.exp(gcc[h][rows_of(b)] - gc_rows[h:h + 1, rows_of(b)]), 0.0)
             for h, b in probs]
    low = [jnp.where(lower, _dot_nt(k_beta[h][rows_of(b)], k[h][rows_of(b)]) * decay[i], 0.0)
           for i, (h, b) in enumerate(probs)]
    attn = [_dot_nt(q[h][rows_of(b)], k[h][rows_of(b)]) * decay[i] for i, (h, b) in enumerate(probs)]
    n = range(len(probs))
    d4 = [jnp.where(xr < 4, low[i], 0.0) for i in n]
    d4sq = [_dot(d4[i], d4[i]) for i in n]
    t_inv = [(eye - d4[i]) + _dot(eye - d4[i], d4sq[i]) for i in n]
    for lv in range(2, 6):
        half = 1 << lv
        off = [jnp.where((xr >> lv) == 1, low[i], 0.0) for i in n]
        if half < SUBLANES:
            off_t = [_dot(off[i], t_inv[i]) for i in n]
            t_inv = [t_inv[i] - _dot(t_inv[i], off_t[i]) for i in n]
        else:
            off_t = [_dot(_second_rows(off[i], half), t_inv[i]) for i in n]
            t_low = [_second_rows(t_inv[i], half) for i in n]
            upd = [t_low[i] - _dot(t_low[i], _set_second_rows(jnp.zeros_like(t_inv[i]), off_t[i], half))
                   for i in n]
            t_inv = [_set_second_rows(t_inv[i], upd[i], half) for i in n]
    uw = {hb: _dot(t_inv[i], rhs[hb[0]][rows_of(hb[1])]) for i, hb in enumerate(probs)}
    attn = {hb: attn[i] for i, hb in enumerate(probs)}

    state = [s_scr[h] for h in heads]
    per_blk = DN_BLK // CHUNK
    for c in range(tt // CHUNK):
        rows = slice(c * CHUNK, (c + 1) * CHUNK)
        b = c // per_blk
        loc = slice((c % per_blk) * CHUNK, (c % per_blk + 1) * CHUNK)
        g_last = [gcc[h][(c + 1) * CHUNK - 1:(c + 1) * CHUNK, :] for h in heads]
        v_new = [uw[h, b][loc, :DN_DV] - _dot(uw[h, b][loc, DN_DV:], state[h]) for h in heads]
        out = [_dot(q_dec[h][rows], state[h]) + _dot(attn[h, b][loc, loc], v_new[h]) for h in heads]
        k_dec = [k[h][rows] * jnp.exp(g_last[h] - gcc[h][rows]) for h in heads]
        state = [state[h] * jnp.exp(g_last[h]) + _dot_tn(k_dec[h], v_new[h]) for h in heads]
        for h in heads:
            o = out[h] * lax.rsqrt(jnp.mean(out[h] * out[h], axis=-1, keepdims=True) + EPS)
            o = o * normw * _silu(z[rows, h * DN_DV:(h + 1) * DN_DV])
            o_ref[0, rows, h * DN_DV:(h + 1) * DN_DV] = o.astype(BF16)
    for h in heads:
        s_scr[h] = state[h]


def _ret_kernel(r_ref, cos_ref, sin_ref, dmask_ref, xi_ref, zeta_ref, gch_ref, o_ref, s_scr, *, tt):
    r = r_ref[0].astype(F32)
    cos = cos_ref[0]
    sin_signed = sin_ref[0]
    first_half = (_iota2((tt, LANES), 1) % RET_DK) < (RET_DK // 2)

    def rotary(t):
        swapped = jnp.where(first_half, pltpu.roll(t, LANES - RET_DK // 2, 1),
                            pltpu.roll(t, RET_DK // 2, 1))
        return t * cos + swapped * sin_signed

    q = jnp.concatenate([rotary(r[:, i * LANES:(i + 1) * LANES]) for i in range(RET_QK // LANES)], axis=1)
    k = jnp.concatenate([rotary(r[:, RET_QK + i * LANES:RET_QK + (i + 1) * LANES])
                         for i in range(RET_QK // LANES)], axis=1) * (RET_DK ** -0.5)
    v = r[:, 2 * RET_QK:2 * RET_QK + RET_VW]
    gate = r[:, 2 * RET_QK + RET_VW:]

    heads = range(RET_HEADS)
    state = [s_scr[h] for h in heads]
    chunk = dmask_ref.shape[1]
    for r0 in range(0, tt, chunk):
        rows = slice(r0, r0 + chunk)
        qh = [q[rows, h * RET_DK:(h + 1) * RET_DK] for h in heads]
        kh = [k[rows, h * RET_DK:(h + 1) * RET_DK] for h in heads]
        vh = [v[rows, h * RET_DV:(h + 1) * RET_DV] for h in heads]
        scores = [_dot_nt(qh[h], kh[h]) * dmask_ref[h] for h in heads]
        cross = [_dot(qh[h], state[h]) * xi_ref[:, h:h + 1] for h in heads]
        inner = [_dot(scores[h], vh[h]) for h in heads]
        state = [state[h] * gch_ref[h:h + 1, :] + _dot_tn(kh[h] * zeta_ref[:, h:h + 1], vh[h]) for h in heads]
        for h in heads:
            out = cross[h] + inner[h]
            mu = jnp.mean(out, axis=-1, keepdims=True)
            cen = out - mu
            var = jnp.mean(cen * cen, axis=-1, keepdims=True)
            out = cen * lax.rsqrt(var + EPS) * _silu(gate[rows, h * RET_DV:(h + 1) * RET_DV])
            o_ref[0, rows, h * RET_DV:(h + 1) * RET_DV] = out.astype(BF16)
    for h in heads:
        s_scr[h] = state[h]


def _lru_kernel(l_ref, wax_ref, ba_ref, bx_ref, lam_ref, o_ref, h_scr, perm_scr, *, tt):
    l = l_ref[0]
    xc = _rows_to_slabs(perm_scr.at[0], l[:, 0:LRU_WIDTH].astype(F32))
    rs, gs = [], []
    for n in range(LRU_BLOCKS):
        xb = xc[:, n * LRU_BLOCK:(n + 1) * LRU_BLOCK].astype(BF16)
        both = jnp.dot(xb, wax_ref[n], preferred_element_type=F32)
        rs.append(both[:, :LRU_BLOCK])
        gs.append(both[:, LRU_BLOCK:])
    rgate = _sigmoid(jnp.concatenate(rs, axis=1) + ba_ref[...])
    igate = _sigmoid(jnp.concatenate(gs, axis=1) + bx_ref[...])
    log_a = (-LRU_C) * rgate * _softplus(-lam_ref[...])
    a = jnp.exp(log_a)
    one_minus = -jnp.tanh(log_a) * (a * a + 1.0)
    bv = one_minus * lax.rsqrt(jnp.maximum(one_minus, F32_MIN_NORMAL)) * (igate * xc)

    ns = tt // SUBLANES
    hs = [bv[0:ns, :]]
    ps = [a[0:ns, :]]
    for i in range(1, SUBLANES):
        ai = a[i * ns:(i + 1) * ns, :]
        hs.append(ai * hs[-1] + bv[i * ns:(i + 1) * ns, :])
        ps.append(ai * ps[-1])
    pg, hg = ps[-1], hs[-1]
    row = _iota2((ns, LRU_WIDTH), 0)
    s = 1
    while s < ns:
        keep = row >= s
        p_prev = jnp.where(keep, pltpu.roll(pg, s, 0), 1.0)
        h_prev = jnp.where(keep, pltpu.roll(hg, s, 0), 0.0)
        hg = pg * h_prev + hg
        pg = pg * p_prev
        s *= 2
    h0 = h_scr[...]
    after = pg * h0 + hg
    h_scr[...] = after[ns - 1:ns, :]
    before = jnp.where(row == 0, h0, pltpu.roll(after, 1, 0))
    hseq = _slabs_to_rows(perm_scr.at[1], jnp.concatenate([hs[i] + ps[i] * before for i in range(SUBLANES)],
                                                          axis=0))
    g = l[:, LRU_WIDTH:].astype(F32)
    gelu = 0.5 * g * (1.0 + jnp.tanh(math.sqrt(2.0 / math.pi) * (g + 0.044715 * (g * g * g))))
    o_ref[0] = (hseq * gelu).astype(BF16)


def _rope_kernel(pos_ref, inv_ref, cos_ref, sin_ref):
    ang = pos_ref[0].astype(F32) * inv_ref[...]
    first_half = (_iota2(ang.shape, 1) % RET_DK) < (RET_DK // 2)
    sin = jnp.sin(ang)
    cos_ref[0] = jnp.cos(ang)
    sin_ref[0] = jnp.where(first_half, -sin, sin)


def _rope_tables(positions, inv_row):
    b, t = positions.shape
    tt = min(TT_FFN, t)
    spec = pl.BlockSpec((1, tt, LANES), lambda i, j: (i, j, 0))
    shape = jax.ShapeDtypeStruct((b, t, LANES), F32)
    pos3 = jnp.broadcast_to(positions[:, :, None], (b, t, LANES))
    return pl.pallas_call(
        _rope_kernel,
        grid=(b, t // tt),
        in_specs=[spec, pl.BlockSpec((1, LANES), lambda i, j: (0, 0))],
        out_specs=[spec, spec],
        out_shape=[shape, shape],
        compiler_params=_params(("parallel", "parallel")),
        name="rope_tables",
    )(pos3, inv_row)


N_DN_IN, N_RET_IN, N_LRU_IN = 7, 7, 5


def _mixer_kernel(*refs, tt):
    dn_in = refs[:N_DN_IN]
    ret_in = refs[N_DN_IN:N_DN_IN + N_RET_IN]
    lru_in = refs[N_DN_IN + N_RET_IN:N_DN_IN + N_RET_IN + N_LRU_IN]
    ya_ref, yb_ref, yc_ref, dn_state, ret_state, lru_state, lru_perm = refs[N_DN_IN + N_RET_IN + N_LRU_IN:]

    @pl.when(pl.program_id(1) == 0)
    def _():
        for scr in (dn_state, ret_state, lru_state):
            scr[...] = jnp.zeros(scr.shape, F32)

    _ret_kernel(*ret_in, yb_ref, ret_state, tt=tt)
    _dn_kernel(*dn_in, ya_ref, dn_state, tt=tt)
    _lru_kernel(*lru_in, yc_ref, lru_state, lru_perm, tt=tt)


def _mixer(u3, ab, abt, rope_cos, rope_sin, ret_tables, dn_p, lru_p):
    b, t, _ = u3.shape
    tt = min(TT_MIX, t)
    nt = t // tt
    full2 = lambda i, j: (0, 0)
    full3 = lambda i, j: (0, 0, 0)
    rc = ret_tables[0].shape[1]
    dn_specs = [
        pl.BlockSpec((1, tt, DN_QKV), lambda i, j: (i, j, OFF_DN // DN_QKV)),
        pl.BlockSpec((1, tt, DN_VW), lambda i, j: (i, j, OFF_Z // DN_VW)),
        pl.BlockSpec((tt, LANES), lambda i, j: (i * nt + j, 0)),
        pl.BlockSpec((AB_ROWS, tt), lambda i, j: (0, i * nt + j)),
        pl.BlockSpec((SUBLANES, LANES), full2),
        pl.BlockSpec((AB_ROWS, LANES), full2),
        pl.BlockSpec((1, DN_DV), full2),
    ]
    ret_specs = [
        pl.BlockSpec((1, tt, RET_W), lambda i, j: (i, j, OFF_RET // RET_W)),
        pl.BlockSpec((1, tt, LANES), lambda i, j: (i, j, 0)),
        pl.BlockSpec((1, tt, LANES), lambda i, j: (i, j, 0)),
        pl.BlockSpec((RET_HEADS, rc, rc), full3),
        pl.BlockSpec((rc, LANES), full2),
        pl.BlockSpec((rc, LANES), full2),
        pl.BlockSpec((SUBLANES, LANES), full2),
    ]
    lru_specs = [
        pl.BlockSpec((1, tt, LRU_W), lambda i, j: (i, j, OFF_LRU // LRU_W)),
        pl.BlockSpec((LRU_BLOCKS, LRU_BLOCK, 2 * LRU_BLOCK), full3),
        pl.BlockSpec((1, LRU_WIDTH), full2),
        pl.BlockSpec((1, LRU_WIDTH), full2),
        pl.BlockSpec((1, LRU_WIDTH), full2),
    ]
    assert (len(dn_specs), len(ret_specs), len(lru_specs)) == (N_DN_IN, N_RET_IN, N_LRU_IN)
    out_spec = pl.BlockSpec((1, tt, BRANCH_WIDTH), lambda i, j: (i, j, 0))
    out_shape = jax.ShapeDtypeStruct((b, t, BRANCH_WIDTH), BF16)
    return pl.pallas_call(
        functools.partial(_mixer_kernel, tt=tt),
        grid=(b, nt),
        in_specs=dn_specs + ret_specs + lru_specs,
        out_specs=[out_spec] * N_BRANCH,
        out_shape=[out_shape] * N_BRANCH,
        scratch_shapes=[
            pltpu.VMEM((DN_HEADS, DN_DK, DN_DV), F32),
            pltpu.VMEM((RET_HEADS, RET_DK, RET_DV), F32),
            pltpu.VMEM((1, LRU_WIDTH), F32),
            pltpu.VMEM((2, LRU_WIDTH // LANES, tt, LANES), F32),
        ],
        compiler_params=_params(("parallel", "arbitrary")),
        name="mixer",
    )(u3, u3, ab, abt, *dn_p, u3, rope_cos, rope_sin, *ret_tables, u3, *lru_p)


def _merge_kernel(ya_ref, yb_ref, yc_ref, g0_ref, g1_ref, g2_ref, x_ref, wb_ref, wo_ref, o_ref):
    merged = None
    for y_ref, g_ref, n in ((ya_ref, g0_ref, 0), (yb_ref, g1_ref, 1), (yc_ref, g2_ref, 2)):
        term = _sigmoid(g_ref[...].astype(F32)) * jnp.dot(y_ref[...], wb_ref[n], preferred_element_type=F32)
        merged = term if merged is None else merged + term
    o_ref[...] = x_ref[...] + jnp.dot(merged.astype(BF16), wo_ref[...], preferred_element_type=F32)


def _merge(ya, yb, yc, u2, x2, wb, wo, layer):
    m = x2.shape[0]
    tm = min(TM_MERGE, m)
    gate_blk = OFF_GATE // D_MODEL
    yspec = pl.BlockSpec((tm, BRANCH_WIDTH), lambda i: (i, 0))
    return pl.pallas_call(
        _merge_kernel,
        grid=(m // tm,),
        in_specs=[
            yspec, yspec, yspec,
            pl.BlockSpec((tm, D_MODEL), lambda i: (i, gate_blk)),
            pl.BlockSpec((tm, D_MODEL), lambda i: (i, gate_blk + 1)),
            pl.BlockSpec((tm, D_MODEL), lambda i: (i, gate_blk + 2)),
            pl.BlockSpec((tm, D_MODEL), lambda i: (i, 0)),
            pl.BlockSpec((None, N_BRANCH, BRANCH_WIDTH, D_MODEL), lambda i: (layer, 0, 0, 0)),
            pl.BlockSpec((None, D_MODEL, D_MODEL), lambda i: (layer, 0, 0)),
        ],
        out_specs=pl.BlockSpec((tm, D_MODEL), lambda i: (i, 0)),
        out_shape=jax.ShapeDtypeStruct((m, D_MODEL), F32),
        compiler_params=_params(("parallel",)),
        name="merge",
    )(ya, yb, yc, u2, u2, u2, x2, wb, wo)


def _ffn_kernel(x_ref, g_ref, wup_ref, cw_ref, cb_ref, wdn_ref, fg_ref, o_ref,
                tail_scr, act_scr, perm_scr, *, tt, final):
    @pl.when(pl.program_id(1) == 0)
    def _():
        tail_scr[...] = jnp.zeros(tail_scr.shape, F32)

    x = _rows_to_slabs(perm_scr, x_ref[0])
    hb = (x * lax.rsqrt(jnp.mean(x * x, axis=-1, keepdims=True) + EPS) * g_ref[...]).astype(BF16)

    def conv(pre, col0):
        cols = slice(col0, col0 + FFN_COLS)
        return _causal_conv_slabs(tail_scr, cw_ref, cols, pre, FFN_CONV) + cb_ref[:, cols]

    for c in range(D_FF // FFN_COLS):
        c0 = c * FFN_COLS
        gate = conv(jnp.dot(hb, wup_ref[:, c0:c0 + FFN_COLS], preferred_element_type=F32), c0)
        val = conv(jnp.dot(hb, wup_ref[:, D_FF + c0:D_FF + c0 + FFN_COLS], preferred_element_type=F32),
                   D_FF + c0)
        act_scr[:, c0:c0 + FFN_COLS] = (_silu(gate) * val).astype(BF16)

    y = x + jnp.dot(act_scr[...], wdn_ref[...], preferred_element_type=F32)
    if final:
        y = y * lax.rsqrt(jnp.mean(y * y, axis=-1, keepdims=True) + EPS) * fg_ref[...]
    o_ref[0] = _slabs_to_rows(perm_scr, y)


def _conv_ffn(x3, g, w_up, conv_w, conv_b, w_down, final_g, layer, final):
    b, t, _ = x3.shape
    tt = min(TT_FFN, t)
    full2 = lambda i, j: (0, 0)
    of_layer = lambda i, j: (layer, 0, 0)
    return pl.pallas_call(
        functools.partial(_ffn_kernel, tt=tt, final=final),
        grid=(b, t // tt),
        in_specs=[
            pl.BlockSpec((1, tt, D_MODEL), lambda i, j: (i, j, 0)),
            pl.BlockSpec((1, D_MODEL), full2),
            pl.BlockSpec((None, D_MODEL, 2 * D_FF), of_layer, pipeline_mode=pl.Buffered(1)),
            pl.BlockSpec((FFN_CONV, 2 * D_FF), full2),
            pl.BlockSpec((1, 2 * D_FF), full2),
            pl.BlockSpec((None, D_FF, D_MODEL), of_layer, pipeline_mode=pl.Buffered(1)),
            pl.BlockSpec((1, D_MODEL), full2),
        ],
        out_specs=pl.BlockSpec((1, tt, D_MODEL), lambda i, j: (i, j, 0)),
        out_shape=jax.ShapeDtypeStruct((b, t, D_MODEL), F32),
        scratch_shapes=[
            pltpu.VMEM((SUBLANES, 2 * D_FF), F32),
            pltpu.VMEM((tt, D_FF), BF16),
            pltpu.VMEM((D_MODEL // LANES, tt, LANES), F32),
        ],
        compiler_params=_params(("parallel", "arbitrary")),
        name="conv_ffn",
    )(x3, g, w_up, conv_w, conv_b, w_down, final_g)


def _retention_tables(chunk):
    log_gamma = np.log(1.0 - 2.0 ** (-5.0 - np.arange(RET_HEADS, dtype=np.float64)))
    idx = np.arange(chunk, dtype=np.float64)
    dist = idx[:, None] - idx[None, :]
    causal = dist >= 0
    dmask = np.where(causal, np.exp(np.where(causal, dist, 0.0) * log_gamma[:, None, None]), 0.0)
    xi = np.zeros((chunk, LANES), np.float64)
    zeta = np.zeros((chunk, LANES), np.float64)
    xi[:, :RET_HEADS] = np.exp((idx[:, None] + 1.0) * log_gamma[None, :])
    zeta[:, :RET_HEADS] = np.exp((chunk - 1.0 - idx[:, None]) * log_gamma[None, :])
    gch = np.zeros((SUBLANES, LANES), np.float64)
    gch[:RET_HEADS, :] = np.exp(chunk * log_gamma)[:, None]
    return tuple(jnp.asarray(t, F32) for t in (dmask, xi, zeta, gch))


def _rotary_inv_row():
    half = RET_DK // 2
    inv = (ROPE_BASE ** (-np.arange(half, dtype=np.float32) / half)).astype(np.float32)
    return jnp.asarray(np.tile(inv, LANES // half)[None, :], F32)


def _split_w_in(w_in):
    sizes = (DN_QKV, DN_HEADS, DN_HEADS, DN_VW, RET_QK, RET_QK, RET_VW, RET_VW, LRU_WIDTH, LRU_WIDTH,
             N_BRANCH * D_MODEL)
    offs = np.concatenate([[0], np.cumsum(sizes)])
    w_in = w_in.astype(BF16)
    part = lambda n: w_in[..., int(offs[n]):int(offs[n + 1])]
    main = jnp.concatenate([part(0), part(4), part(5), part(6), part(7), part(8), part(9), part(10), part(3)],
                           axis=-1)
    ab = jnp.concatenate([part(1), part(2)], axis=-1)
    w_ab = jnp.pad(ab, ((0, 0), (0, 0), (0, LANES - 2 * DN_HEADS)))
    w_abt = jnp.pad(jnp.swapaxes(ab, -1, -2), ((0, 0), (0, AB_ROWS - 2 * DN_HEADS), (0, 0)))
    return main, w_ab, w_abt


def kernel(x, positions, attn_norm, w_in, dn_conv_w, dn_a_log, dn_dt_bias, dn_norm_w, lru_conv_w, lru_conv_b,
           lru_wa, lru_ba, lru_wx, lru_bx, lru_lambda, w_branch, w_out, ffn_norm, w_up, ffn_conv_w, ffn_conv_b,
           w_down, final_norm):
    bsz, seq, _ = x.shape
    m = bsz * seq
    depth = w_in.shape[0]
    ret_tables = _retention_tables(min(RET_CHUNK, seq))
    rope_cos, rope_sin = _rope_tables(positions, _rotary_inv_row())
    w_main, w_ab, w_abt = _split_w_in(w_in)
    hp = jnp.stack([dn_a_log, dn_dt_bias], axis=1).astype(F32)
    hpr = jnp.pad(hp, ((0, 0), (0, SUBLANES - 2), (0, LANES - DN_HEADS)))
    hpc = jnp.pad(jnp.swapaxes(hp, 1, 2), ((0, 0), (0, AB_ROWS - DN_HEADS), (0, LANES - 2)))
    wb = w_branch.astype(BF16)
    wo = w_out.astype(BF16)
    wup = w_up.astype(BF16)
    wdn = w_down.astype(BF16)
    wax = jnp.concatenate([lru_wa, lru_wx], axis=-1).astype(BF16)

    x2 = x.reshape(m, D_MODEL)
    for l in range(depth):
        u2, ab, abt = _in_proj(x2, attn_norm[l][None, :], w_main, w_ab, w_abt, dn_conv_w[l], lru_conv_w[l],
                               lru_conv_b[l][None, :], l, seq)
        dn_p = (hpr[l], hpc[l], dn_norm_w[l][None, :])
        lru_p = (wax[l], lru_ba[l].reshape(1, LRU_WIDTH), lru_bx[l].reshape(1, LRU_WIDTH),
                 lru_lambda[l][None, :])
        ya, yb, yc = _mixer(u2.reshape(bsz, seq, N_MAIN), ab, abt, rope_cos, rope_sin, ret_tables, dn_p, lru_p)
        x2 = _merge(ya.reshape(m, DN_VW), yb.reshape(m, RET_VW), yc.reshape(m, LRU_WIDTH), u2, x2, wb, wo, l)
        x3 = _conv_ffn(x2.reshape(bsz, seq, D_MODEL), ffn_norm[l][None, :], wup, ffn_conv_w[l],
                       ffn_conv_b[l][None, :], wdn, final_norm[None, :], l, final=(l == depth - 1))
        x2 = x3.reshape(m, D_MODEL)
    return x2.reshape(bsz, seq, D_MODEL)
```

```python
import functools
import math

import numpy as np
import jax
import jax.numpy as jnp
from jax import lax
from jax.experimental import pallas as pl
from jax.experimental.pallas import tpu as pltpu

F32 = jnp.float32
BF16 = jnp.bfloat16
F32_MIN_NORMAL = float(np.finfo(np.float32).tiny)

D_MODEL = 1024
CHUNK = 64
EPS = 1e-6
N_BRANCH = 3
BRANCH_WIDTH = 512

DN_HEADS = 4
DN_DK = 128
DN_DV = 128
DN_CONV = 4
DN_QK = DN_HEADS * DN_DK
DN_VW = DN_HEADS * DN_DV
DN_QKV = 2 * DN_QK + DN_VW

RET_HEADS = 4
RET_DK = 64
RET_DV = 128
RET_QK = RET_HEADS * RET_DK
RET_VW = RET_HEADS * RET_DV
ROPE_BASE = 10000.0

LRU_WIDTH = 512
LRU_BLOCKS = 4
LRU_BLOCK = LRU_WIDTH // LRU_BLOCKS
LRU_CONV = 4
LRU_C = 8.0

D_FF = 2816
FFN_CONV = 3

SUBLANES = 8
LANES = 128
V7X_VMEM_BYTES = 64 * 1024 * 1024
VMEM_LIMIT_BYTES = V7X_VMEM_BYTES * 7 // 8

RET_W = 2 * RET_QK + 2 * RET_VW
LRU_W = 2 * LRU_WIDTH
OFF_DN = 0
OFF_RET = OFF_DN + DN_QKV
OFF_LRU = OFF_RET + RET_W
OFF_GATE = OFF_LRU + LRU_W
OFF_Z = OFF_GATE + N_BRANCH * D_MODEL
N_MAIN = OFF_Z + DN_VW
AB_ROWS = 16

TM_IN = 512
IN_COLS = 256
TT_MIX = 512
RET_CHUNK = 256
DN_BLK = 2 * CHUNK
TM_MERGE = 512
TT_FFN = 512
FFN_COLS = 256


def _params(sem):
    return pltpu.CompilerParams(dimension_semantics=sem, vmem_limit_bytes=VMEM_LIMIT_BYTES)


def _sigmoid(x):
    return 1.0 / (1.0 + jnp.exp(-x))


def _silu(x):
    return x * _sigmoid(x)


def _softplus(x):
    return jnp.maximum(x, 0.0) + jnp.log1p(jnp.exp(-jnp.abs(x)))


def _dot(a, b):
    return jnp.dot(a.astype(BF16), b.astype(BF16), preferred_element_type=F32)


def _dot_nt(a, b):
    return lax.dot_general(a.astype(BF16), b.astype(BF16), (((1,), (1,)), ((), ())),
                           preferred_element_type=F32)


def _dot_tn(a, b):
    return lax.dot_general(a.astype(BF16), b.astype(BF16), (((0,), (0,)), ((), ())),
                           preferred_element_type=F32)


def _iota2(shape, dim):
    return lax.broadcasted_iota(jnp.int32, shape, dim)


def _rows_to_slabs(scr, x):
    tt = x.shape[0]
    ns = tt // SUBLANES
    panels = x.shape[1] // LANES
    for p in range(panels):
        scr[p] = x[:, p * LANES:(p + 1) * LANES]
    return jnp.concatenate(
        [jnp.concatenate([scr[p, pl.ds(i, ns, stride=SUBLANES), :] for p in range(panels)], axis=1)
         for i in range(SUBLANES)], axis=0)


def _slabs_to_rows(scr, y):
    tt = y.shape[0]
    ns = tt // SUBLANES
    panels = y.shape[1] // LANES
    for p in range(panels):
        for i in range(SUBLANES):
            scr[p, pl.ds(i, ns, stride=SUBLANES), :] = y[i * ns:(i + 1) * ns, p * LANES:(p + 1) * LANES]
    return jnp.concatenate([scr[p] for p in range(panels)], axis=1)


def _causal_conv_slabs(tail_scr, w_ref, cols, xs, width):
    ns = xs.shape[0] // SUBLANES
    group0 = _iota2((ns, xs.shape[1]), 0) == 0
    slabs = [xs[i * ns:(i + 1) * ns, :] for i in range(SUBLANES)]
    back = [jnp.where(group0, tail_scr[i:i + 1, cols], pltpu.roll(slabs[i], 1, 0))
            for i in range(SUBLANES - (width - 1), SUBLANES)]
    for i in range(SUBLANES - (width - 1), SUBLANES):
        tail_scr[i:i + 1, cols] = slabs[i][ns - 1:ns, :]
    ext = back + slabs
    out = []
    for i in range(SUBLANES):
        y = ext[i] * w_ref[0:1, cols]
        for j in range(1, width):
            y = y + ext[i + j] * w_ref[j:j + 1, cols]
        out.append(y)
    return jnp.concatenate(out, axis=0)


def _in_proj_chunk_order():
    starts = list(range(0, N_MAIN, IN_COLS))
    heavy = [c for c in starts if c < OFF_DN + DN_QKV or OFF_LRU <= c < OFF_LRU + LRU_WIDTH]
    plain = [c for c in starts if c not in heavy]
    order, per = [], len(plain) // len(heavy)
    for i, c in enumerate(heavy):
        order.append(c)
        order.extend(plain[i * per:(i + 1) * per])
    order.extend(plain[len(heavy) * per:])
    return [(c, IN_COLS) for c in order]


def _in_proj_kernel(x_ref, g_ref, w_ref, wab_ref, dncw_ref, lrucw_ref, lrucb_ref,
                    u_ref, ab_ref, abt_ref, dn_tail, lru_tail, perm_scr, *, tiles_per_seq):
    @pl.when(pl.program_id(0) % tiles_per_seq == 0)
    def _():
        dn_tail[...] = jnp.zeros(dn_tail.shape, F32)
        lru_tail[...] = jnp.zeros(lru_tail.shape, F32)

    x = x_ref[...]
    hb = (x * lax.rsqrt(jnp.mean(x * x, axis=-1, keepdims=True) + EPS) * g_ref[...]).astype(BF16)

    n_conv = 0
    for c0, width in _in_proj_chunk_order():
        p = jnp.dot(hb, w_ref[:, c0:c0 + width], preferred_element_type=F32)
        is_dn = c0 < OFF_DN + DN_QKV
        is_lru = OFF_LRU <= c0 < OFF_LRU + LRU_WIDTH
        if is_dn or is_lru:
            ps = _rows_to_slabs(perm_scr.at[2 * n_conv], p)
            if is_dn:
                cols = slice(c0 - OFF_DN, c0 - OFF_DN + IN_COLS)
                ps = _silu(_causal_conv_slabs(dn_tail, dncw_ref, cols, ps, DN_CONV))
                if c0 < OFF_DN + 2 * DN_QK:
                    scale = DN_DK ** -0.5 if c0 < OFF_DN + DN_QK else 1.0
                    heads = [ps[:, h0:h0 + DN_DK] for h0 in range(0, IN_COLS, DN_DK)]
                    ps = jnp.concatenate(
                        [t * (lax.rsqrt(jnp.sum(t * t, axis=-1, keepdims=True) + EPS) * scale) for t in heads],
                        axis=1)
            else:
                cols = slice(c0 - OFF_LRU, c0 - OFF_LRU + IN_COLS)
                ps = _causal_conv_slabs(lru_tail, lrucw_ref, cols, ps, LRU_CONV) + lrucb_ref[:, cols]
            p = _slabs_to_rows(perm_scr.at[2 * n_conv + 1], ps)
            n_conv += 1
        u_ref[:, c0:c0 + width] = p.astype(BF16)
    ab = jnp.dot(hb, wab_ref[...], preferred_element_type=F32)
    ab_ref[...] = ab
    abt_ref[...] = ab.T[0:AB_ROWS, :]


def _in_proj(x2, g, w_main, w_ab, dn_conv_w, lru_conv_w, lru_conv_b, layer, seq):
    m = x2.shape[0]
    tm = min(TM_IN, seq)
    return pl.pallas_call(
        functools.partial(_in_proj_kernel, tiles_per_seq=seq // tm),
        grid=(m // tm,),
        in_specs=[
            pl.BlockSpec((tm, D_MODEL), lambda i: (i, 0)),
            pl.BlockSpec((1, D_MODEL), lambda i: (0, 0)),
            pl.BlockSpec((None, D_MODEL, N_MAIN), lambda i: (layer, 0, 0), pipeline_mode=pl.Buffered(1)),
            pl.BlockSpec((None, D_MODEL, LANES), lambda i: (layer, 0, 0)),
            pl.BlockSpec((DN_CONV, DN_QKV), lambda i: (0, 0)),
            pl.BlockSpec((LRU_CONV, LRU_WIDTH), lambda i: (0, 0)),
            pl.BlockSpec((1, LRU_WIDTH), lambda i: (0, 0)),
        ],
        out_specs=[
            pl.BlockSpec((tm, N_MAIN), lambda i: (i, 0)),
            pl.BlockSpec((tm, LANES), lambda i: (i, 0)),
            pl.BlockSpec((AB_ROWS, tm), lambda i: (0, i)),
        ],
        out_shape=[
            jax.ShapeDtypeStruct((m, N_MAIN), BF16),
            jax.ShapeDtypeStruct((m, LANES), F32),
            jax.ShapeDtypeStruct((AB_ROWS, m), F32),
        ],
        scratch_shapes=[
            pltpu.VMEM((SUBLANES, DN_QKV), F32),
            pltpu.VMEM((SUBLANES, LRU_WIDTH), F32),
            pltpu.VMEM((2 * (DN_QKV + LRU_WIDTH) // IN_COLS, IN_COLS // LANES, tm, LANES), F32),
        ],
        compiler_params=_params(("arbitrary",)),
        name="in_proj",
    )(x2, g, w_main, w_ab, dn_conv_w, lru_conv_w, lru_conv_b)


def _second_rows(x, half):
    return jnp.concatenate([x[r:r + half, :] for r in range(half, x.shape[0], 2 * half)], axis=0)


def _set_second_rows(x, rows, half):
    parts = []
    for j, r in enumerate(range(0, x.shape[0], 2 * half)):
        parts += [x[r:r + half, :], rows[j * half:(j + 1) * half, :]]
    return jnp.concatenate(parts, axis=0)


def _dn_kernel(qkv_ref, z_ref, ab_ref, abt_ref, hpr_ref, hpc_ref, normw_ref, o_ref, s_scr, *, tt):
    a_in = qkv_ref[0]
    a = a_in.astype(F32)

    ab = ab_ref[...]
    abt = abt_ref[...]
    g_cols = -jnp.exp(hpr_ref[0:1, :]) * _softplus(ab + hpr_ref[1:2, :])
    beta_cols = _sigmoid(ab)
    g_rows = -jnp.exp(hpc_ref[:, 0:1]) * _softplus(abt + hpc_ref[:, 1:2])

    ri = _iota2((DN_BLK, DN_BLK), 0)
    ci = _iota2((DN_BLK, DN_BLK), 1)
    xr = ri ^ ci
    lower = ri > ci
    causal = (xr < CHUNK) & (ri >= ci)
    eye = jnp.where(ri == ci, 1.0, 0.0).astype(F32)

    gc_cols, gc_rows = g_cols, g_rows
    t_col = _iota2(g_cols.shape, 0) % CHUNK
    t_row = _iota2(g_rows.shape, 1) % CHUNK
    s = 1
    while s < CHUNK:
        gc_cols = gc_cols + jnp.where(t_col >= s, pltpu.roll(gc_cols, s, 0), 0.0)
        gc_rows = gc_rows + jnp.where(t_row >= s, pltpu.roll(gc_rows, s, 1), 0.0)
        s *= 2
    normw = normw_ref[...]
    z = z_ref[0].astype(F32)

    heads = range(DN_HEADS)
    q, k, v, gcc, beta, eg, k_beta = [], [], [], [], [], [], []
    for h in heads:
        q.append(a[:, h * DN_DK:(h + 1) * DN_DK])
        k.append(a[:, DN_QK + h * DN_DK:DN_QK + (h + 1) * DN_DK])
        v.append(a[:, 2 * DN_QK + h * DN_DV:2 * DN_QK + (h + 1) * DN_DV])
        gcc.append(gc_cols[:, h:h + 1])
        beta.append(beta_cols[:, DN_HEADS + h:DN_HEADS + h + 1])
        eg.append(jnp.exp(gcc[h]))
        k_beta.append(k[h] * beta[h])
    rhs = [jnp.concatenate([v[h] * beta[h], k_beta[h] * eg[h]], axis=1) for h in heads]
    q_dec = [q[h] * eg[h] for h in heads]

    probs = [(h, b) for h in heads for b in range(tt // DN_BLK)]
    rows_of = lambda b: slice(b * DN_BLK, (b + 1) * DN_BLK)
    decay = [jnp.where(causal, jnp.exp(gcc[h][rows_of(b)] - gc_rows[h:h + 1, rows_of(b)]), 0.0)
             for h, b in probs]
    k_op = lambda h, b: a_in[rows_of(b), DN_QK + h * DN_DK:DN_QK + (h + 1) * DN_DK]
    q_op = lambda h, b: a_in[rows_of(b), h * DN_DK:(h + 1) * DN_DK]
    low = [jnp.where(lower, _dot_nt(k_beta[h][rows_of(b)], k_op(h, b)) * decay[i], 0.0)
           for i, (h, b) in enumerate(probs)]
    attn = [_dot_nt(q_op(h, b), k_op(h, b)) * decay[i] for i, (h, b) in enumerate(probs)]
    n = range(len(probs))
    d4 = [jnp.where(xr < 4, low[i], 0.0) for i in n]
    d4sq = [_dot(d4[i], d4[i]) for i in n]
    t_inv = [(eye - d4[i]) + _dot(eye - d4[i], d4sq[i]) for i in n]
    for lv in range(2, 6):
        half = 1 << lv
        off = [jnp.where((xr >> lv) == 1, low[i], 0.0) for i in n]
        if half < SUBLANES:
            off_t = [_dot(off[i], t_inv[i]) for i in n]
            t_inv = [t_inv[i] - _dot(t_inv[i], off_t[i]) for i in n]
        else:
            off_t = [_dot(_second_rows(off[i], half), t_inv[i]) for i in n]
            t_low = [_second_rows(t_inv[i], half) for i in n]
            upd = [t_low[i] - _dot(t_low[i], _set_second_rows(jnp.zeros_like(t_inv[i]), off_t[i], half))
                   for i in n]
            t_inv = [_set_second_rows(t_inv[i], upd[i], half) for i in n]
    uw = {hb: _dot(t_inv[i], rhs[hb[0]][rows_of(hb[1])]) for i, hb in enumerate(probs)}
    attn = {hb: attn[i] for i, hb in enumerate(probs)}

    state = [s_scr[h] for h in heads]
    per_blk = DN_BLK // CHUNK
    for c in range(tt // CHUNK):
        rows = slice(c * CHUNK, (c + 1) * CHUNK)
        b = c // per_blk
        loc = slice((c % per_blk) * CHUNK, (c % per_blk + 1) * CHUNK)
        g_last = [gcc[h][(c + 1) * CHUNK - 1:(c + 1) * CHUNK, :] for h in heads]
        v_new = [uw[h, b][loc, :DN_DV] - _dot(uw[h, b][loc, DN_DV:], state[h]) for h in heads]
        out = [_dot(q_dec[h][rows], state[h]) + _dot(attn[h, b][loc, loc], v_new[h]) for h in heads]
        k_dec = [k[h][rows] * jnp.exp(g_last[h] - gcc[h][rows]) for h in heads]
        state = [state[h] * jnp.exp(g_last[h]) + _dot_tn(k_dec[h], v_new[h]) for h in heads]
        for h in heads:
            o = out[h] * lax.rsqrt(jnp.mean(out[h] * out[h], axis=-1, keepdims=True) + EPS)
            o = o * normw * _silu(z[rows, h * DN_DV:(h + 1) * DN_DV])
            o_ref[0, rows, h * DN_DV:(h + 1) * DN_DV] = o.astype(BF16)
    for h in heads:
        s_scr[h] = state[h]


def _ret_kernel(r_ref, cos_ref, sin_ref, dmask_ref, xi_ref, zeta_ref, gch_ref, o_ref, s_scr, *, tt):
    r_in = r_ref[0]
    r = r_in[:, 0:2 * RET_QK].astype(F32)
    cos = cos_ref[0]
    sin_signed = sin_ref[0]
    first_half = (_iota2((tt, LANES), 1) % RET_DK) < (RET_DK // 2)

    def rotary(t):
        swapped = jnp.where(first_half, pltpu.roll(t, LANES - RET_DK // 2, 1),
                            pltpu.roll(t, RET_DK // 2, 1))
        return t * cos + swapped * sin_signed

    q = jnp.concatenate([rotary(r[:, i * LANES:(i + 1) * LANES]) for i in range(RET_QK // LANES)], axis=1)
    k = jnp.concatenate([rotary(r[:, RET_QK + i * LANES:RET_QK + (i + 1) * LANES])
                         for i in range(RET_QK // LANES)], axis=1) * (RET_DK ** -0.5)
    v = r_in[:, 2 * RET_QK:2 * RET_QK + RET_VW]
    gate = r_in[:, 2 * RET_QK + RET_VW:].astype(F32)

    heads = range(RET_HEADS)
    state = [s_scr[h] for h in heads]
    chunk = dmask_ref.shape[1]
    for r0 in range(0, tt, chunk):
        rows = slice(r0, r0 + chunk)
        qh = [q[rows, h * RET_DK:(h + 1) * RET_DK] for h in heads]
        kh = [k[rows, h * RET_DK:(h + 1) * RET_DK] for h in heads]
        vh = [v[rows, h * RET_DV:(h + 1) * RET_DV] for h in heads]
        scores = [_dot_nt(qh[h], kh[h]) * dmask_ref[h] for h in heads]
        cross = [_dot(qh[h], state[h]) * xi_ref[:, h:h + 1] for h in heads]
        inner = [_dot(scores[h], vh[h]) for h in heads]
        state = [state[h] * gch_ref[h:h + 1, :] + _dot_tn(kh[h] * zeta_ref[:, h:h + 1], vh[h]) for h in heads]
        for h in heads:
            out = cross[h] + inner[h]
            mu = jnp.mean(out, axis=-1, keepdims=True)
            cen = out - mu
            var = jnp.mean(cen * cen, axis=-1, keepdims=True)
            out = cen * lax.rsqrt(var + EPS) * _silu(gate[rows, h * RET_DV:(h + 1) * RET_DV])
            o_ref[0, rows, h * RET_DV:(h + 1) * RET_DV] = out.astype(BF16)
    for h in heads:
        s_scr[h] = state[h]


def _lru_kernel(l_ref, wax_ref, ba_ref, bx_ref, lam_ref, o_ref, h_scr, perm_scr, *, tt):
    l = l_ref[0]
    xc = _rows_to_slabs(perm_scr.at[0], l[:, 0:LRU_WIDTH].astype(F32))
    rs, gs = [], []
    for n in range(LRU_BLOCKS):
        xb = xc[:, n * LRU_BLOCK:(n + 1) * LRU_BLOCK].astype(BF16)
        both = jnp.dot(xb, wax_ref[n], preferred_element_type=F32)
        rs.append(both[:, :LRU_BLOCK])
        gs.append(both[:, LRU_BLOCK:])
    rgate = _sigmoid(jnp.concatenate(rs, axis=1) + ba_ref[...])
    igate = _sigmoid(jnp.concatenate(gs, axis=1) + bx_ref[...])
    log_a = (-LRU_C) * rgate * _softplus(-lam_ref[...])
    a = jnp.exp(log_a)
    one_minus = -jnp.tanh(log_a) * (a * a + 1.0)
    bv = one_minus * lax.rsqrt(jnp.maximum(one_minus, F32_MIN_NORMAL)) * (igate * xc)

    ns = tt // SUBLANES
    hs = [bv[0:ns, :]]
    ps = [a[0:ns, :]]
    for i in range(1, SUBLANES):
        ai = a[i * ns:(i + 1) * ns, :]
        hs.append(ai * hs[-1] + bv[i * ns:(i + 1) * ns, :])
        ps.append(ai * ps[-1])
    pg, hg = ps[-1], hs[-1]
    row = _iota2((ns, LRU_WIDTH), 0)
    s = 1
    while s < ns:
        keep = row >= s
        p_prev = jnp.where(keep, pltpu.roll(pg, s, 0), 1.0)
        h_prev = jnp.where(keep, pltpu.roll(hg, s, 0), 0.0)
        hg = pg * h_prev + hg
        pg = pg * p_prev
        s *= 2
    h0 = h_scr[...]
    after = pg * h0 + hg
    h_scr[...] = after[ns - 1:ns, :]
    before = jnp.where(row == 0, h0, pltpu.roll(after, 1, 0))
    hseq = _slabs_to_rows(perm_scr.at[1], jnp.concatenate([hs[i] + ps[i] * before for i in range(SUBLANES)],
                                                          axis=0))
    g = l[:, LRU_WIDTH:].astype(F32)
    gelu = 0.5 * g * (1.0 + jnp.tanh(math.sqrt(2.0 / math.pi) * (g + 0.044715 * (g * g * g))))
    o_ref[0] = (hseq * gelu).astype(BF16)


def _rope_kernel(pos_ref, inv_ref, cos_ref, sin_ref):
    ang = pos_ref[0].astype(F32) * inv_ref[...]
    first_half = (_iota2(ang.shape, 1) % RET_DK) < (RET_DK // 2)
    sin = jnp.sin(ang)
    cos_ref[0] = jnp.cos(ang)
    sin_ref[0] = jnp.where(first_half, -sin, sin)


def _rope_tables(positions, inv_row):
    b, t = positions.shape
    tt = min(TT_FFN, t)
    spec = pl.BlockSpec((1, tt, LANES), lambda i, j: (i, j, 0))
    shape = jax.ShapeDtypeStruct((b, t, LANES), F32)
    pos3 = jnp.broadcast_to(positions[:, :, None], (b, t, LANES))
    return pl.pallas_call(
        _rope_kernel,
        grid=(b, t // tt),
        in_specs=[spec, pl.BlockSpec((1, LANES), lambda i, j: (0, 0))],
        out_specs=[spec, spec],
        out_shape=[shape, shape],
        compiler_params=_params(("parallel", "parallel")),
        name="rope_tables",
    )(pos3, inv_row)


N_DN_IN, N_RET_IN, N_LRU_IN = 7, 7, 5


def _mixer_kernel(*refs, tt):
    dn_in = refs[:N_DN_IN]
    ret_in = refs[N_DN_IN:N_DN_IN + N_RET_IN]
    lru_in = refs[N_DN_IN + N_RET_IN:N_DN_IN + N_RET_IN + N_LRU_IN]
    ya_ref, yb_ref, yc_ref, dn_state, ret_state, lru_state, lru_perm = refs[N_DN_IN + N_RET_IN + N_LRU_IN:]

    @pl.when(pl.program_id(1) == 0)
    def _():
        for scr in (dn_state, ret_state, lru_state):
            scr[...] = jnp.zeros(scr.shape, F32)

    _ret_kernel(*ret_in, yb_ref, ret_state, tt=tt)
    _dn_kernel(*dn_in, ya_ref, dn_state, tt=tt)
    _lru_kernel(*lru_in, yc_ref, lru_state, lru_perm, tt=tt)


def _mixer(u3, ab, abt, rope_cos, rope_sin, ret_tables, dn_p, lru_p):
    b, t, _ = u3.shape
    tt = min(TT_MIX, t)
    nt = t // tt
    full2 = lambda i, j: (0, 0)
    full3 = lambda i, j: (0, 0, 0)
    rc = ret_tables[0].shape[1]
    dn_specs = [
        pl.BlockSpec((1, tt, DN_QKV), lambda i, j: (i, j, OFF_DN // DN_QKV)),
        pl.BlockSpec((1, tt, DN_VW), lambda i, j: (i, j, OFF_Z // DN_VW)),
        pl.BlockSpec((tt, LANES), lambda i, j: (i * nt + j, 0)),
        pl.BlockSpec((AB_ROWS, tt), lambda i, j: (0, i * nt + j)),
        pl.BlockSpec((SUBLANES, LANES), full2),
        pl.BlockSpec((AB_ROWS, LANES), full2),
        pl.BlockSpec((1, DN_DV), full2),
    ]
    ret_specs = [
        pl.BlockSpec((1, tt, RET_W), lambda i, j: (i, j, OFF_RET // RET_W)),
        pl.BlockSpec((1, tt, LANES), lambda i, j: (i, j, 0)),
        pl.BlockSpec((1, tt, LANES), lambda i, j: (i, j, 0)),
        pl.BlockSpec((RET_HEADS, rc, rc), full3),
        pl.BlockSpec((rc, LANES), full2),
        pl.BlockSpec((rc, LANES), full2),
        pl.BlockSpec((SUBLANES, LANES), full2),
    ]
    lru_specs = [
        pl.BlockSpec((1, tt, LRU_W), lambda i, j: (i, j, OFF_LRU // LRU_W)),
        pl.BlockSpec((LRU_BLOCKS, LRU_BLOCK, 2 * LRU_BLOCK), full3),
        pl.BlockSpec((1, LRU_WIDTH), full2),
        pl.BlockSpec((1, LRU_WIDTH), full2),
        pl.BlockSpec((1, LRU_WIDTH), full2),
    ]
    assert (len(dn_specs), len(ret_specs), len(lru_specs)) == (N_DN_IN, N_RET_IN, N_LRU_IN)
    out_spec = pl.BlockSpec((1, tt, BRANCH_WIDTH), lambda i, j: (i, j, 0))
    out_shape = jax.ShapeDtypeStruct((b, t, BRANCH_WIDTH), BF16)
    return pl.pallas_call(
        functools.partial(_mixer_kernel, tt=tt),
        grid=(b, nt),
        in_specs=dn_specs + ret_specs + lru_specs,
        out_specs=[out_spec] * N_BRANCH,
        out_shape=[out_shape] * N_BRANCH,
        scratch_shapes=[
            pltpu.VMEM((DN_HEADS, DN_DK, DN_DV), F32),
            pltpu.VMEM((RET_HEADS, RET_DK, RET_DV), F32),
            pltpu.VMEM((1, LRU_WIDTH), F32),
            pltpu.VMEM((2, LRU_WIDTH // LANES, tt, LANES), F32),
        ],
        compiler_params=_params(("parallel", "arbitrary")),
        name="mixer",
    )(u3, u3, ab, abt, *dn_p, u3, rope_cos, rope_sin, *ret_tables, u3, *lru_p)


def _merge_kernel(ya_ref, yb_ref, yc_ref, g0_ref, g1_ref, g2_ref, x_ref, wb_ref, wo_ref, o_ref):
    merged = None
    for y_ref, g_ref, n in ((ya_ref, g0_ref, 0), (yb_ref, g1_ref, 1), (yc_ref, g2_ref, 2)):
        term = _sigmoid(g_ref[...].astype(F32)) * jnp.dot(y_ref[...], wb_ref[n], preferred_element_type=F32)
        merged = term if merged is None else merged + term
    o_ref[...] = x_ref[...] + jnp.dot(merged.astype(BF16), wo_ref[...], preferred_element_type=F32)


def _merge(ya, yb, yc, u2, x2, wb, wo, layer):
    m = x2.shape[0]
    tm = min(TM_MERGE, m)
    gate_blk = OFF_GATE // D_MODEL
    yspec = pl.BlockSpec((tm, BRANCH_WIDTH), lambda i: (i, 0))
    return pl.pallas_call(
        _merge_kernel,
        grid=(m // tm,),
        in_specs=[
            yspec, yspec, yspec,
            pl.BlockSpec((tm, D_MODEL), lambda i: (i, gate_blk)),
            pl.BlockSpec((tm, D_MODEL), lambda i: (i, gate_blk + 1)),
            pl.BlockSpec((tm, D_MODEL), lambda i: (i, gate_blk + 2)),
            pl.BlockSpec((tm, D_MODEL), lambda i: (i, 0)),
            pl.BlockSpec((None, N_BRANCH, BRANCH_WIDTH, D_MODEL), lambda i: (layer, 0, 0, 0)),
            pl.BlockSpec((None, D_MODEL, D_MODEL), lambda i: (layer, 0, 0)),
        ],
        out_specs=pl.BlockSpec((tm, D_MODEL), lambda i: (i, 0)),
        out_shape=jax.ShapeDtypeStruct((m, D_MODEL), F32),
        compiler_params=_params(("parallel",)),
        name="merge",
    )(ya, yb, yc, u2, u2, u2, x2, wb, wo)


def _ffn_kernel(x_ref, g_ref, wup_ref, cw_ref, cb_ref, wdn_ref, fg_ref, o_ref,
                tail_scr, act_scr, perm_scr, *, tt, final):
    @pl.when(pl.program_id(1) == 0)
    def _():
        tail_scr[...] = jnp.zeros(tail_scr.shape, F32)

    x = _rows_to_slabs(perm_scr, x_ref[0])
    hb = (x * lax.rsqrt(jnp.mean(x * x, axis=-1, keepdims=True) + EPS) * g_ref[...]).astype(BF16)

    def conv(pre, col0):
        cols = slice(col0, col0 + FFN_COLS)
        return _causal_conv_slabs(tail_scr, cw_ref, cols, pre, FFN_CONV) + cb_ref[:, cols]

    for c in range(D_FF // FFN_COLS):
        c0 = c * FFN_COLS
        gate = conv(jnp.dot(hb, wup_ref[:, c0:c0 + FFN_COLS], preferred_element_type=F32), c0)
        val = conv(jnp.dot(hb, wup_ref[:, D_FF + c0:D_FF + c0 + FFN_COLS], preferred_element_type=F32),
                   D_FF + c0)
        act_scr[:, c0:c0 + FFN_COLS] = (_silu(gate) * val).astype(BF16)

    y = x + jnp.dot(act_scr[...], wdn_ref[...], preferred_element_type=F32)
    if final:
        y = y * lax.rsqrt(jnp.mean(y * y, axis=-1, keepdims=True) + EPS) * fg_ref[...]
    o_ref[0] = _slabs_to_rows(perm_scr, y)


def _conv_ffn(x3, g, w_up, conv_w, conv_b, w_down, final_g, layer, final):
    b, t, _ = x3.shape
    tt = min(TT_FFN, t)
    full2 = lambda i, j: (0, 0)
    of_layer = lambda i, j: (layer, 0, 0)
    return pl.pallas_call(
        functools.partial(_ffn_kernel, tt=tt, final=final),
        grid=(b, t // tt),
        in_specs=[
            pl.BlockSpec((1, tt, D_MODEL), lambda i, j: (i, j, 0)),
            pl.BlockSpec((1, D_MODEL), full2),
            pl.BlockSpec((None, D_MODEL, 2 * D_FF), of_layer, pipeline_mode=pl.Buffered(1)),
            pl.BlockSpec((FFN_CONV, 2 * D_FF), full2),
            pl.BlockSpec((1, 2 * D_FF), full2),
            pl.BlockSpec((None, D_FF, D_MODEL), of_layer, pipeline_mode=pl.Buffered(1)),
            pl.BlockSpec((1, D_MODEL), full2),
        ],
        out_specs=pl.BlockSpec((1, tt, D_MODEL), lambda i, j: (i, j, 0)),
        out_shape=jax.ShapeDtypeStruct((b, t, D_MODEL), F32),
        scratch_shapes=[
            pltpu.VMEM((SUBLANES, 2 * D_FF), F32),
            pltpu.VMEM((tt, D_FF), BF16),
            pltpu.VMEM((D_MODEL // LANES, tt, LANES), F32),
        ],
        compiler_params=_params(("parallel", "arbitrary")),
        name="conv_ffn",
    )(x3, g, w_up, conv_w, conv_b, w_down, final_g)


def _retention_tables(chunk):
    log_gamma = np.log(1.0 - 2.0 ** (-5.0 - np.arange(RET_HEADS, dtype=np.float64)))
    idx = np.arange(chunk, dtype=np.float64)
    dist = idx[:, None] - idx[None, :]
    causal = dist >= 0
    dmask = np.where(causal, np.exp(np.where(causal, dist, 0.0) * log_gamma[:, None, None]), 0.0)
    xi = np.zeros((chunk, LANES), np.float64)
    zeta = np.zeros((chunk, LANES), np.float64)
    xi[:, :RET_HEADS] = np.exp((idx[:, None] + 1.0) * log_gamma[None, :])
    zeta[:, :RET_HEADS] = np.exp((chunk - 1.0 - idx[:, None]) * log_gamma[None, :])
    gch = np.zeros((SUBLANES, LANES), np.float64)
    gch[:RET_HEADS, :] = np.exp(chunk * log_gamma)[:, None]
    return tuple(jnp.asarray(t, F32) for t in (dmask, xi, zeta, gch))


def _rotary_inv_row():
    half = RET_DK // 2
    inv = (ROPE_BASE ** (-np.arange(half, dtype=np.float32) / half)).astype(np.float32)
    return jnp.asarray(np.tile(inv, LANES // half)[None, :], F32)


def _split_w_in(w_in):
    sizes = (DN_QKV, DN_HEADS, DN_HEADS, DN_VW, RET_QK, RET_QK, RET_VW, RET_VW, LRU_WIDTH, LRU_WIDTH,
             N_BRANCH * D_MODEL)
    offs = np.concatenate([[0], np.cumsum(sizes)])
    w_in = w_in.astype(BF16)
    part = lambda n: w_in[..., int(offs[n]):int(offs[n + 1])]
    main = jnp.concatenate([part(0), part(4), part(5), part(6), part(7), part(8), part(9), part(10), part(3)],
                           axis=-1)
    ab = jnp.concatenate([part(1), part(2)], axis=-1)
    w_ab = jnp.pad(ab, ((0, 0), (0, 0), (0, LANES - 2 * DN_HEADS)))
    return main, w_ab


def kernel(x, positions, attn_norm, w_in, dn_conv_w, dn_a_log, dn_dt_bias, dn_norm_w, lru_conv_w, lru_conv_b,
           lru_wa, lru_ba, lru_wx, lru_bx, lru_lambda, w_branch, w_out, ffn_norm, w_up, ffn_conv_w, ffn_conv_b,
           w_down, final_norm):
    bsz, seq, _ = x.shape
    m = bsz * seq
    depth = w_in.shape[0]
    ret_tables = _retention_tables(min(RET_CHUNK, seq))
    rope_cos, rope_sin = _rope_tables(positions, _rotary_inv_row())
    w_main, w_ab = _split_w_in(w_in)
    hp = jnp.stack([dn_a_log, dn_dt_bias], axis=1).astype(F32)
    hpr = jnp.pad(hp, ((0, 0), (0, SUBLANES - 2), (0, LANES - DN_HEADS)))
    hpc = jnp.pad(jnp.swapaxes(hp, 1, 2), ((0, 0), (0, AB_ROWS - DN_HEADS), (0, LANES - 2)))
    wb = w_branch.astype(BF16)
    wo = w_out.astype(BF16)
    wup = w_up.astype(BF16)
    wdn = w_down.astype(BF16)
    wax = jnp.concatenate([lru_wa, lru_wx], axis=-1).astype(BF16)

    x2 = x.reshape(m, D_MODEL)
    for l in range(depth):
        u2, ab, abt = _in_proj(x2, attn_norm[l][None, :], w_main, w_ab, dn_conv_w[l], lru_conv_w[l],
                               lru_conv_b[l][None, :], l, seq)
        dn_p = (hpr[l], hpc[l], dn_norm_w[l][None, :])
        lru_p = (wax[l], lru_ba[l].reshape(1, LRU_WIDTH), lru_bx[l].reshape(1, LRU_WIDTH),
                 lru_lambda[l][None, :])
        ya, yb, yc = _mixer(u2.reshape(bsz, seq, N_MAIN), ab, abt, rope_cos, rope_sin, ret_tables, dn_p, lru_p)
        x2 = _merge(ya.reshape(m, DN_VW), yb.reshape(m, RET_VW), yc.reshape(m, LRU_WIDTH), u2, x2, wb, wo, l)
        x3 = _conv_ffn(x2.reshape(bsz, seq, D_MODEL), ffn_norm[l][None, :], wup, ffn_conv_w[l],
                       ffn_conv_b[l][None, :], wdn, final_norm[None, :], l, final=(l == depth - 1))
        x2 = x3.reshape(m, D_MODEL)
    return x2.reshape(bsz, seq, D_MODEL)
```

```python
import functools
import math

import numpy as np
import jax
import jax.numpy as jnp
from jax import lax
from jax.experimental import pallas as pl
from jax.experimental.pallas import tpu as pltpu

F32 = jnp.float32
BF16 = jnp.bfloat16
F32_MIN_NORMAL = float(np.finfo(np.float32).tiny)

D_MODEL = 1024
CHUNK = 64
EPS = 1e-6
N_BRANCH = 3
BRANCH_WIDTH = 512

DN_HEADS = 4
DN_DK = 128
DN_DV = 128
DN_CONV = 4
DN_QK = DN_HEADS * DN_DK
DN_VW = DN_HEADS * DN_DV
DN_QKV = 2 * DN_QK + DN_VW

RET_HEADS = 4
RET_DK = 64
RET_DV = 128
RET_QK = RET_HEADS * RET_DK
RET_VW = RET_HEADS * RET_DV
ROPE_BASE = 10000.0

LRU_WIDTH = 512
LRU_BLOCKS = 4
LRU_BLOCK = LRU_WIDTH // LRU_BLOCKS
LRU_CONV = 4
LRU_C = 8.0

D_FF = 2816
FFN_CONV = 3

SUBLANES = 8
LANES = 128
V7X_VMEM_BYTES = 64 * 1024 * 1024
VMEM_LIMIT_BYTES = V7X_VMEM_BYTES * 7 // 8

RET_W = 2 * RET_QK + 2 * RET_VW
LRU_W = 2 * LRU_WIDTH
OFF_DN = 0
OFF_RET = OFF_DN + DN_QKV
OFF_LRU = OFF_RET + RET_W
OFF_GATE = OFF_LRU + LRU_W
OFF_Z = OFF_GATE + N_BRANCH * D_MODEL
N_MAIN = OFF_Z + DN_VW
AB_ROWS = 16

TM_IN = 512
IN_COLS = 256
TT_MIX = 512
RET_CHUNK = 256
DN_BLK = 2 * CHUNK
TM_MERGE = 512
TT_FFN = 512
FFN_COLS = 256


def _params(sem):
    return pltpu.CompilerParams(dimension_semantics=sem, vmem_limit_bytes=VMEM_LIMIT_BYTES)


def _sigmoid(x):
    return 1.0 / (1.0 + jnp.exp(-x))


def _silu(x):
    return x * _sigmoid(x)


def _gelu_tanh(x):
    return 0.5 * x * (1.0 + jnp.tanh(math.sqrt(2.0 / math.pi) * (x + 0.044715 * (x * x * x))))


def _softplus(x):
    return jnp.maximum(x, 0.0) + jnp.log1p(jnp.exp(-jnp.abs(x)))


def _dot(a, b):
    return jnp.dot(a.astype(BF16), b.astype(BF16), preferred_element_type=F32)


def _dot_nt(a, b):
    return lax.dot_general(a.astype(BF16), b.astype(BF16), (((1,), (1,)), ((), ())),
                           preferred_element_type=F32)


def _dot_tn(a, b):
    return lax.dot_general(a.astype(BF16), b.astype(BF16), (((0,), (0,)), ((), ())),
                           preferred_element_type=F32)


def _iota2(shape, dim):
    return lax.broadcasted_iota(jnp.int32, shape, dim)


def _rows_to_slabs(scr, x):
    tt = x.shape[0]
    ns = tt // SUBLANES
    panels = x.shape[1] // LANES
    for p in range(panels):
        scr[p] = x[:, p * LANES:(p + 1) * LANES]
    return jnp.concatenate(
        [jnp.concatenate([scr[p, pl.ds(i, ns, stride=SUBLANES), :] for p in range(panels)], axis=1)
         for i in range(SUBLANES)], axis=0)


def _slabs_to_rows(scr, y):
    tt = y.shape[0]
    ns = tt // SUBLANES
    panels = y.shape[1] // LANES
    for p in range(panels):
        for i in range(SUBLANES):
            scr[p, pl.ds(i, ns, stride=SUBLANES), :] = y[i * ns:(i + 1) * ns, p * LANES:(p + 1) * LANES]
    return jnp.concatenate([scr[p] for p in range(panels)], axis=1)


def _causal_conv_slabs(tail_scr, w_ref, cols, xs, width):
    ns = xs.shape[0] // SUBLANES
    group0 = _iota2((ns, xs.shape[1]), 0) == 0
    slabs = [xs[i * ns:(i + 1) * ns, :] for i in range(SUBLANES)]
    back = [jnp.where(group0, tail_scr[i:i + 1, cols], pltpu.roll(slabs[i], 1, 0))
            for i in range(SUBLANES - (width - 1), SUBLANES)]
    for i in range(SUBLANES - (width - 1), SUBLANES):
        tail_scr[i:i + 1, cols] = slabs[i][ns - 1:ns, :]
    ext = back + slabs
    out = []
    for i in range(SUBLANES):
        y = ext[i] * w_ref[0:1, cols]
        for j in range(1, width):
            y = y + ext[i + j] * w_ref[j:j + 1, cols]
        out.append(y)
    return jnp.concatenate(out, axis=0)


def _in_proj_chunk_order():
    starts = list(range(0, N_MAIN, IN_COLS))
    heavy = [c for c in starts if c < OFF_DN + DN_QKV or OFF_LRU <= c < OFF_LRU + LRU_WIDTH]
    plain = [c for c in starts if c not in heavy]
    order, per = [], len(plain) // len(heavy)
    for i, c in enumerate(heavy):
        order.append(c)
        order.extend(plain[i * per:(i + 1) * per])
    order.extend(plain[len(heavy) * per:])
    return [(c, IN_COLS) for c in order]


def _in_proj_kernel(x_ref, g_ref, w_ref, wab_ref, dncw_ref, lrucw_ref, lrucb_ref,
                    u_ref, ab_ref, abt_ref, dn_tail, lru_tail, perm_scr, *, tiles_per_seq):
    @pl.when(pl.program_id(0) % tiles_per_seq == 0)
    def _():
        dn_tail[...] = jnp.zeros(dn_tail.shape, F32)
        lru_tail[...] = jnp.zeros(lru_tail.shape, F32)

    x = x_ref[...]
    hb = (x * lax.rsqrt(jnp.mean(x * x, axis=-1, keepdims=True) + EPS) * g_ref[...]).astype(BF16)

    n_conv = 0
    for c0, width in _in_proj_chunk_order():
        p = jnp.dot(hb, w_ref[:, c0:c0 + width], preferred_element_type=F32)
        is_dn = c0 < OFF_DN + DN_QKV
        is_lru = OFF_LRU <= c0 < OFF_LRU + LRU_WIDTH
        if is_dn or is_lru:
            ps = _rows_to_slabs(perm_scr.at[2 * n_conv], p)
            if is_dn:
                cols = slice(c0 - OFF_DN, c0 - OFF_DN + IN_COLS)
                ps = _silu(_causal_conv_slabs(dn_tail, dncw_ref, cols, ps, DN_CONV))
                if c0 < OFF_DN + 2 * DN_QK:
                    scale = DN_DK ** -0.5 if c0 < OFF_DN + DN_QK else 1.0
                    heads = [ps[:, h0:h0 + DN_DK] for h0 in range(0, IN_COLS, DN_DK)]
                    ps = jnp.concatenate(
                        [t * (lax.rsqrt(jnp.sum(t * t, axis=-1, keepdims=True) + EPS) * scale) for t in heads],
                        axis=1)
            else:
                cols = slice(c0 - OFF_LRU, c0 - OFF_LRU + IN_COLS)
                ps = _causal_conv_slabs(lru_tail, lrucw_ref, cols, ps, LRU_CONV) + lrucb_ref[:, cols]
            p = _slabs_to_rows(perm_scr.at[2 * n_conv + 1], ps)
            n_conv += 1
        elif OFF_RET + 2 * RET_QK + RET_VW <= c0 < OFF_RET + RET_W or c0 >= OFF_Z:
            p = _silu(p)
        elif OFF_LRU + LRU_WIDTH <= c0 < OFF_LRU + LRU_W:
            p = _gelu_tanh(p)
        u_ref[:, c0:c0 + width] = p.astype(BF16)
    ab = jnp.dot(hb, wab_ref[...], preferred_element_type=F32)
    ab_ref[...] = ab
    abt_ref[...] = ab.T[0:AB_ROWS, :]


def _in_proj(x2, g, w_main, w_ab, dn_conv_w, lru_conv_w, lru_conv_b, layer, seq):
    m = x2.shape[0]
    tm = min(TM_IN, seq)
    return pl.pallas_call(
        functools.partial(_in_proj_kernel, tiles_per_seq=seq // tm),
        grid=(m // tm,),
        in_specs=[
            pl.BlockSpec((tm, D_MODEL), lambda i: (i, 0)),
            pl.BlockSpec((1, D_MODEL), lambda i: (0, 0)),
            pl.BlockSpec((None, D_MODEL, N_MAIN), lambda i: (layer, 0, 0), pipeline_mode=pl.Buffered(1)),
            pl.BlockSpec((None, D_MODEL, LANES), lambda i: (layer, 0, 0)),
            pl.BlockSpec((DN_CONV, DN_QKV), lambda i: (0, 0)),
            pl.BlockSpec((LRU_CONV, LRU_WIDTH), lambda i: (0, 0)),
            pl.BlockSpec((1, LRU_WIDTH), lambda i: (0, 0)),
        ],
        out_specs=[
            pl.BlockSpec((tm, N_MAIN), lambda i: (i, 0)),
            pl.BlockSpec((tm, LANES), lambda i: (i, 0)),
            pl.BlockSpec((AB_ROWS, tm), lambda i: (0, i)),
        ],
        out_shape=[
            jax.ShapeDtypeStruct((m, N_MAIN), BF16),
            jax.ShapeDtypeStruct((m, LANES), F32),
            jax.ShapeDtypeStruct((AB_ROWS, m), F32),
        ],
        scratch_shapes=[
            pltpu.VMEM((SUBLANES, DN_QKV), F32),
            pltpu.VMEM((SUBLANES, LRU_WIDTH), F32),
            pltpu.VMEM((2 * (DN_QKV + LRU_WIDTH) // IN_COLS, IN_COLS // LANES, tm, LANES), F32),
        ],
        compiler_params=_params(("arbitrary",)),
        name="in_proj",
    )(x2, g, w_main, w_ab, dn_conv_w, lru_conv_w, lru_conv_b)


def _second_rows(x, half):
    return jnp.concatenate([x[r:r + half, :] for r in range(half, x.shape[0], 2 * half)], axis=0)


def _set_second_rows(x, rows, half):
    parts = []
    for j, r in enumerate(range(0, x.shape[0], 2 * half)):
        parts += [x[r:r + half, :], rows[j * half:(j + 1) * half, :]]
    return jnp.concatenate(parts, axis=0)


def _dn_kernel(qkv_ref, z_ref, ab_ref, abt_ref, hpr_ref, hpc_ref, normw_ref, o_ref, s_scr, *, tt):
    a_in = qkv_ref[0]
    a = a_in.astype(F32)

    ab = ab_ref[...]
    abt = abt_ref[...]
    g_cols = -jnp.exp(hpr_ref[0:1, :]) * _softplus(ab + hpr_ref[1:2, :])
    beta_cols = _sigmoid(ab)
    g_rows = -jnp.exp(hpc_ref[:, 0:1]) * _softplus(abt + hpc_ref[:, 1:2])

    ri = _iota2((DN_BLK, DN_BLK), 0)
    ci = _iota2((DN_BLK, DN_BLK), 1)
    xr = ri ^ ci
    lower = ri > ci
    causal = (xr < CHUNK) & (ri >= ci)
    eye = jnp.where(ri == ci, 1.0, 0.0).astype(F32)

    gc_cols, gc_rows = g_cols, g_rows
    t_col = _iota2(g_cols.shape, 0) % CHUNK
    t_row = _iota2(g_rows.shape, 1) % CHUNK
    s = 1
    while s < CHUNK:
        gc_cols = gc_cols + jnp.where(t_col >= s, pltpu.roll(gc_cols, s, 0), 0.0)
        gc_rows = gc_rows + jnp.where(t_row >= s, pltpu.roll(gc_rows, s, 1), 0.0)
        s *= 2
    normw = normw_ref[...]
    z = z_ref[0].astype(F32)

    heads = range(DN_HEADS)
    q, k, v, gcc, beta, eg, k_beta = [], [], [], [], [], [], []
    for h in heads:
        q.append(a[:, h * DN_DK:(h + 1) * DN_DK])
        k.append(a[:, DN_QK + h * DN_DK:DN_QK + (h + 1) * DN_DK])
        v.append(a[:, 2 * DN_QK + h * DN_DV:2 * DN_QK + (h + 1) * DN_DV])
        gcc.append(gc_cols[:, h:h + 1])
        beta.append(beta_cols[:, DN_HEADS + h:DN_HEADS + h + 1])
        eg.append(jnp.exp(gcc[h]))
        k_beta.append(k[h] * beta[h])
    rhs = [jnp.concatenate([v[h] * beta[h], k_beta[h] * eg[h]], axis=1) for h in heads]
    q_dec = [q[h] * eg[h] for h in heads]

    probs = [(h, b) for h in heads for b in range(tt // DN_BLK)]
    rows_of = lambda b: slice(b * DN_BLK, (b + 1) * DN_BLK)
    decay = [jnp.where(causal, jnp.exp(gcc[h][rows_of(b)] - gc_rows[h:h + 1, rows_of(b)]), 0.0)
             for h, b in probs]
    k_op = lambda h, b: a_in[rows_of(b), DN_QK + h * DN_DK:DN_QK + (h + 1) * DN_DK]
    q_op = lambda h, b: a_in[rows_of(b), h * DN_DK:(h + 1) * DN_DK]
    low = [jnp.where(lower, _dot_nt(k_beta[h][rows_of(b)], k_op(h, b)) * decay[i], 0.0)
           for i, (h, b) in enumerate(probs)]
    attn = [_dot_nt(q_op(h, b), k_op(h, b)) * decay[i] for i, (h, b) in enumerate(probs)]
    n = range(len(probs))
    d4 = [jnp.where(xr < 4, low[i], 0.0) for i in n]
    d4sq = [_dot(d4[i], d4[i]) for i in n]
    t_inv = [(eye - d4[i]) + _dot(eye - d4[i], d4sq[i]) for i in n]
    for lv in range(2, 6):
        half = 1 << lv
        off = [jnp.where((xr >> lv) == 1, low[i], 0.0) for i in n]
        if half < SUBLANES:
            off_t = [_dot(off[i], t_inv[i]) for i in n]
            t_inv = [t_inv[i] - _dot(t_inv[i], off_t[i]) for i in n]
        else:
            off_t = [_dot(_second_rows(off[i], half), t_inv[i]) for i in n]
            t_low = [_second_rows(t_inv[i], half) for i in n]
            upd = [t_low[i] - _dot(t_low[i], _set_second_rows(jnp.zeros_like(t_inv[i]), off_t[i], half))
                   for i in n]
            t_inv = [_set_second_rows(t_inv[i], upd[i], half) for i in n]
    uw = {hb: _dot(t_inv[i], rhs[hb[0]][rows_of(hb[1])]) for i, hb in enumerate(probs)}
    attn = {hb: attn[i] for i, hb in enumerate(probs)}

    state = [s_scr[h] for h in heads]
    per_blk = DN_BLK // CHUNK
    for c in range(tt // CHUNK):
        rows = slice(c * CHUNK, (c + 1) * CHUNK)
        b = c // per_blk
        loc = slice((c % per_blk) * CHUNK, (c % per_blk + 1) * CHUNK)
        g_last = [gcc[h][(c + 1) * CHUNK - 1:(c + 1) * CHUNK, :] for h in heads]
        v_new = [uw[h, b][loc, :DN_DV] - _dot(uw[h, b][loc, DN_DV:], state[h]) for h in heads]
        out = [_dot(q_dec[h][rows], state[h]) + _dot(attn[h, b][loc, loc], v_new[h]) for h in heads]
        k_dec = [k[h][rows] * jnp.exp(g_last[h] - gcc[h][rows]) for h in heads]
        state = [state[h] * jnp.exp(g_last[h]) + _dot_tn(k_dec[h], v_new[h]) for h in heads]
        for h in heads:
            o = out[h] * lax.rsqrt(jnp.mean(out[h] * out[h], axis=-1, keepdims=True) + EPS)
            o = o * normw * z[rows, h * DN_DV:(h + 1) * DN_DV]
            o_ref[0, rows, h * DN_DV:(h + 1) * DN_DV] = o.astype(BF16)
    for h in heads:
        s_scr[h] = state[h]


def _ret_kernel(r_ref, cos_ref, sin_ref, dmask_ref, xi_ref, zeta_ref, gch_ref, o_ref, s_scr, *, tt):
    r_in = r_ref[0]
    r = r_in[:, 0:2 * RET_QK].astype(F32)
    cos = cos_ref[0]
    sin_signed = sin_ref[0]
    first_half = (_iota2((tt, LANES), 1) % RET_DK) < (RET_DK // 2)

    def rotary(t):
        swapped = jnp.where(first_half, pltpu.roll(t, LANES - RET_DK // 2, 1),
                            pltpu.roll(t, RET_DK // 2, 1))
        return t * cos + swapped * sin_signed

    q = jnp.concatenate([rotary(r[:, i * LANES:(i + 1) * LANES]) for i in range(RET_QK // LANES)], axis=1)
    k = jnp.concatenate([rotary(r[:, RET_QK + i * LANES:RET_QK + (i + 1) * LANES])
                         for i in range(RET_QK // LANES)], axis=1) * (RET_DK ** -0.5)
    v = r_in[:, 2 * RET_QK:2 * RET_QK + RET_VW]
    gate = r_in[:, 2 * RET_QK + RET_VW:].astype(F32)

    heads = range(RET_HEADS)
    state = [s_scr[h] for h in heads]
    chunk = dmask_ref.shape[1]
    for r0 in range(0, tt, chunk):
        rows = slice(r0, r0 + chunk)
        qh = [q[rows, h * RET_DK:(h + 1) * RET_DK] for h in heads]
        kh = [k[rows, h * RET_DK:(h + 1) * RET_DK] for h in heads]
        vh = [v[rows, h * RET_DV:(h + 1) * RET_DV] for h in heads]
        scores = [_dot_nt(qh[h], kh[h]) * dmask_ref[h] for h in heads]
        cross = [_dot(qh[h], state[h]) * xi_ref[:, h:h + 1] for h in heads]
        inner = [_dot(scores[h], vh[h]) for h in heads]
        state = [state[h] * gch_ref[h:h + 1, :] + _dot_tn(kh[h] * zeta_ref[:, h:h + 1], vh[h]) for h in heads]
        for h in heads:
            out = cross[h] + inner[h]
            mu = jnp.mean(out, axis=-1, keepdims=True)
            cen = out - mu
            var = jnp.mean(cen * cen, axis=-1, keepdims=True)
            out = cen * lax.rsqrt(var + EPS) * gate[rows, h * RET_DV:(h + 1) * RET_DV]
            o_ref[0, rows, h * RET_DV:(h + 1) * RET_DV] = out.astype(BF16)
    for h in heads:
        s_scr[h] = state[h]


def _lru_kernel(l_ref, wax_ref, ba_ref, bx_ref, lam_ref, o_ref, h_scr, perm_scr, *, tt):
    l = l_ref[0]
    xc = _rows_to_slabs(perm_scr.at[0], l[:, 0:LRU_WIDTH].astype(F32))
    rs, gs = [], []
    for n in range(LRU_BLOCKS):
        xb = xc[:, n * LRU_BLOCK:(n + 1) * LRU_BLOCK].astype(BF16)
        both = jnp.dot(xb, wax_ref[n], preferred_element_type=F32)
        rs.append(both[:, :LRU_BLOCK])
        gs.append(both[:, LRU_BLOCK:])
    rgate = _sigmoid(jnp.concatenate(rs, axis=1) + ba_ref[...])
    igate = _sigmoid(jnp.concatenate(gs, axis=1) + bx_ref[...])
    log_a = (-LRU_C) * rgate * _softplus(-lam_ref[...])
    a = jnp.exp(log_a)
    one_minus = -jnp.tanh(log_a) * (a * a + 1.0)
    bv = one_minus * lax.rsqrt(jnp.maximum(one_minus, F32_MIN_NORMAL)) * (igate * xc)

    ns = tt // SUBLANES
    hs = [bv[0:ns, :]]
    ps = [a[0:ns, :]]
    for i in range(1, SUBLANES):
        ai = a[i * ns:(i + 1) * ns, :]
        hs.append(ai * hs[-1] + bv[i * ns:(i + 1) * ns, :])
        ps.append(ai * ps[-1])
    pg, hg = ps[-1], hs[-1]
    row = _iota2((ns, LRU_WIDTH), 0)
    s = 1
    while s < ns:
        keep = row >= s
        p_prev = jnp.where(keep, pltpu.roll(pg, s, 0), 1.0)
        h_prev = jnp.where(keep, pltpu.roll(hg, s, 0), 0.0)
        hg = pg * h_prev + hg
        pg = pg * p_prev
        s *= 2
    h0 = h_scr[...]
    after = pg * h0 + hg
    h_scr[...] = after[ns - 1:ns, :]
    before = jnp.where(row == 0, h0, pltpu.roll(after, 1, 0))
    hseq = _slabs_to_rows(perm_scr.at[1], jnp.concatenate([hs[i] + ps[i] * before for i in range(SUBLANES)],
                                                          axis=0))
    o_ref[0] = (hseq * l[:, LRU_WIDTH:].astype(F32)).astype(BF16)


def _rope_kernel(pos_ref, inv_ref, cos_ref, sin_ref):
    ang = pos_ref[0].astype(F32) * inv_ref[...]
    first_half = (_iota2(ang.shape, 1) % RET_DK) < (RET_DK // 2)
    sin = jnp.sin(ang)
    cos_ref[0] = jnp.cos(ang)
    sin_ref[0] = jnp.where(first_half, -sin, sin)


def _rope_tables(positions, inv_row):
    b, t = positions.shape
    tt = min(TT_FFN, t)
    spec = pl.BlockSpec((1, tt, LANES), lambda i, j: (i, j, 0))
    shape = jax.ShapeDtypeStruct((b, t, LANES), F32)
    pos3 = jnp.broadcast_to(positions[:, :, None], (b, t, LANES))
    return pl.pallas_call(
        _rope_kernel,
        grid=(b, t // tt),
        in_specs=[spec, pl.BlockSpec((1, LANES), lambda i, j: (0, 0))],
        out_specs=[spec, spec],
        out_shape=[shape, shape],
        compiler_params=_params(("parallel", "parallel")),
        name="rope_tables",
    )(pos3, inv_row)


N_DN_IN, N_RET_IN, N_LRU_IN = 7, 7, 5


def _mixer_kernel(*refs, tt):
    dn_in = refs[:N_DN_IN]
    ret_in = refs[N_DN_IN:N_DN_IN + N_RET_IN]
    lru_in = refs[N_DN_IN + N_RET_IN:N_DN_IN + N_RET_IN + N_LRU_IN]
    ya_ref, yb_ref, yc_ref, dn_state, ret_state, lru_state, lru_perm = refs[N_DN_IN + N_RET_IN + N_LRU_IN:]

    @pl.when(pl.program_id(1) == 0)
    def _():
        for scr in (dn_state, ret_state, lru_state):
            scr[...] = jnp.zeros(scr.shape, F32)

    _ret_kernel(*ret_in, yb_ref, ret_state, tt=tt)
    _dn_kernel(*dn_in, ya_ref, dn_state, tt=tt)
    _lru_kernel(*lru_in, yc_ref, lru_state, lru_perm, tt=tt)


def _mixer(u3, ab, abt, rope_cos, rope_sin, ret_tables, dn_p, lru_p):
    b, t, _ = u3.shape
    tt = min(TT_MIX, t)
    nt = t // tt
    full2 = lambda i, j: (0, 0)
    full3 = lambda i, j: (0, 0, 0)
    rc = ret_tables[0].shape[1]
    dn_specs = [
        pl.BlockSpec((1, tt, DN_QKV), lambda i, j: (i, j, OFF_DN // DN_QKV)),
        pl.BlockSpec((1, tt, DN_VW), lambda i, j: (i, j, OFF_Z // DN_VW)),
        pl.BlockSpec((tt, LANES), lambda i, j: (i * nt + j, 0)),
        pl.BlockSpec((AB_ROWS, tt), lambda i, j: (0, i * nt + j)),
        pl.BlockSpec((SUBLANES, LANES), full2),
        pl.BlockSpec((AB_ROWS, LANES), full2),
        pl.BlockSpec((1, DN_DV), full2),
    ]
    ret_specs = [
        pl.BlockSpec((1, tt, RET_W), lambda i, j: (i, j, OFF_RET // RET_W)),
        pl.BlockSpec((1, tt, LANES), lambda i, j: (i, j, 0)),
        pl.BlockSpec((1, tt, LANES), lambda i, j: (i, j, 0)),
        pl.BlockSpec((RET_HEADS, rc, rc), full3),
        pl.BlockSpec((rc, LANES), full2),
        pl.BlockSpec((rc, LANES), full2),
        pl.BlockSpec((SUBLANES, LANES), full2),
    ]
    lru_specs = [
        pl.BlockSpec((1, tt, LRU_W), lambda i, j: (i, j, OFF_LRU // LRU_W)),
        pl.BlockSpec((LRU_BLOCKS, LRU_BLOCK, 2 * LRU_BLOCK), full3),
        pl.BlockSpec((1, LRU_WIDTH), full2),
        pl.BlockSpec((1, LRU_WIDTH), full2),
        pl.BlockSpec((1, LRU_WIDTH), full2),
    ]
    assert (len(dn_specs), len(ret_specs), len(lru_specs)) == (N_DN_IN, N_RET_IN, N_LRU_IN)
    out_spec = pl.BlockSpec((1, tt, BRANCH_WIDTH), lambda i, j: (i, j, 0))
    out_shape = jax.ShapeDtypeStruct((b, t, BRANCH_WIDTH), BF16)
    return pl.pallas_call(
        functools.partial(_mixer_kernel, tt=tt),
        grid=(b, nt),
        in_specs=dn_specs + ret_specs + lru_specs,
        out_specs=[out_spec] * N_BRANCH,
        out_shape=[out_shape] * N_BRANCH,
        scratch_shapes=[
            pltpu.VMEM((DN_HEADS, DN_DK, DN_DV), F32),
            pltpu.VMEM((RET_HEADS, RET_DK, RET_DV), F32),
            pltpu.VMEM((1, LRU_WIDTH), F32),
            pltpu.VMEM((2, LRU_WIDTH // LANES, tt, LANES), F32),
        ],
        compiler_params=_params(("parallel", "arbitrary")),
        name="mixer",
    )(u3, u3, ab, abt, *dn_p, u3, rope_cos, rope_sin, *ret_tables, u3, *lru_p)


def _merge_kernel(ya_ref, yb_ref, yc_ref, g0_ref, g1_ref, g2_ref, x_ref, wb_ref, wo_ref, o_ref):
    merged = None
    for y_ref, g_ref, n in ((ya_ref, g0_ref, 0), (yb_ref, g1_ref, 1), (yc_ref, g2_ref, 2)):
        term = _sigmoid(g_ref[...].astype(F32)) * jnp.dot(y_ref[...], wb_ref[n], preferred_element_type=F32)
        merged = term if merged is None else merged + term
    o_ref[...] = x_ref[...] + jnp.dot(merged.astype(BF16), wo_ref[...], preferred_element_type=F32)


def _merge(ya, yb, yc, u2, x2, wb, wo, layer):
    m = x2.shape[0]
    tm = min(TM_MERGE, m)
    gate_blk = OFF_GATE // D_MODEL
    yspec = pl.BlockSpec((tm, BRANCH_WIDTH), lambda i: (i, 0))
    return pl.pallas_call(
        _merge_kernel,
        grid=(m // tm,),
        in_specs=[
            yspec, yspec, yspec,
            pl.BlockSpec((tm, D_MODEL), lambda i: (i, gate_blk)),
            pl.BlockSpec((tm, D_MODEL), lambda i: (i, gate_blk + 1)),
            pl.BlockSpec((tm, D_MODEL), lambda i: (i, gate_blk + 2)),
            pl.BlockSpec((tm, D_MODEL), lambda i: (i, 0)),
            pl.BlockSpec((None, N_BRANCH, BRANCH_WIDTH, D_MODEL), lambda i: (layer, 0, 0, 0)),
            pl.BlockSpec((None, D_MODEL, D_MODEL), lambda i: (layer, 0, 0)),
        ],
        out_specs=pl.BlockSpec((tm, D_MODEL), lambda i: (i, 0)),
        out_shape=jax.ShapeDtypeStruct((m, D_MODEL), F32),
        compiler_params=_params(("parallel",)),
        name="merge",
    )(ya, yb, yc, u2, u2, u2, x2, wb, wo)


def _ffn_kernel(x_ref, g_ref, wup_ref, cw_ref, cb_ref, wdn_ref, fg_ref, o_ref,
                tail_scr, act_scr, perm_scr, *, tt, final):
    @pl.when(pl.program_id(1) == 0)
    def _():
        tail_scr[...] = jnp.zeros(tail_scr.shape, F32)

    x = _rows_to_slabs(perm_scr, x_ref[0])
    hb = (x * lax.rsqrt(jnp.mean(x * x, axis=-1, keepdims=True) + EPS) * g_ref[...]).astype(BF16)

    def conv(pre, col0):
        cols = slice(col0, col0 + FFN_COLS)
        return _causal_conv_slabs(tail_scr, cw_ref, cols, pre, FFN_CONV) + cb_ref[:, cols]

    for c in range(D_FF // FFN_COLS):
        c0 = c * FFN_COLS
        gate = conv(jnp.dot(hb, wup_ref[:, c0:c0 + FFN_COLS], preferred_element_type=F32), c0)
        val = conv(jnp.dot(hb, wup_ref[:, D_FF + c0:D_FF + c0 + FFN_COLS], preferred_element_type=F32),
                   D_FF + c0)
        act_scr[:, c0:c0 + FFN_COLS] = (_silu(gate) * val).astype(BF16)

    y = x + jnp.dot(act_scr[...], wdn_ref[...], preferred_element_type=F32)
    if final:
        y = y * lax.rsqrt(jnp.mean(y * y, axis=-1, keepdims=True) + EPS) * fg_ref[...]
    o_ref[0] = _slabs_to_rows(perm_scr, y)


def _conv_ffn(x3, g, w_up, conv_w, conv_b, w_down, final_g, layer, final):
    b, t, _ = x3.shape
    tt = min(TT_FFN, t)
    full2 = lambda i, j: (0, 0)
    of_layer = lambda i, j: (layer, 0, 0)
    return pl.pallas_call(
        functools.partial(_ffn_kernel, tt=tt, final=final),
        grid=(b, t // tt),
        in_specs=[
            pl.BlockSpec((1, tt, D_MODEL), lambda i, j: (i, j, 0)),
            pl.BlockSpec((1, D_MODEL), full2),
            pl.BlockSpec((None, D_MODEL, 2 * D_FF), of_layer, pipeline_mode=pl.Buffered(1)),
            pl.BlockSpec((FFN_CONV, 2 * D_FF), full2),
            pl.BlockSpec((1, 2 * D_FF), full2),
            pl.BlockSpec((None, D_FF, D_MODEL), of_layer, pipeline_mode=pl.Buffered(1)),
            pl.BlockSpec((1, D_MODEL), full2),
        ],
        out_specs=pl.BlockSpec((1, tt, D_MODEL), lambda i, j: (i, j, 0)),
        out_shape=jax.ShapeDtypeStruct((b, t, D_MODEL), F32),
        scratch_shapes=[
            pltpu.VMEM((SUBLANES, 2 * D_FF), F32),
            pltpu.VMEM((tt, D_FF), BF16),
            pltpu.VMEM((D_MODEL // LANES, tt, LANES), F32),
        ],
        compiler_params=_params(("parallel", "arbitrary")),
        name="conv_ffn",
    )(x3, g, w_up, conv_w, conv_b, w_down, final_g)


def _retention_tables(chunk):
    log_gamma = np.log(1.0 - 2.0 ** (-5.0 - np.arange(RET_HEADS, dtype=np.float64)))
    idx = np.arange(chunk, dtype=np.float64)
    dist = idx[:, None] - idx[None, :]
    causal = dist >= 0
    dmask = np.where(causal, np.exp(np.where(causal, dist, 0.0) * log_gamma[:, None, None]), 0.0)
    xi = np.zeros((chunk, LANES), np.float64)
    zeta = np.zeros((chunk, LANES), np.float64)
    xi[:, :RET_HEADS] = np.exp((idx[:, None] + 1.0) * log_gamma[None, :])
    zeta[:, :RET_HEADS] = np.exp((chunk - 1.0 - idx[:, None]) * log_gamma[None, :])
    gch = np.zeros((SUBLANES, LANES), np.float64)
    gch[:RET_HEADS, :] = np.exp(chunk * log_gamma)[:, None]
    return tuple(jnp.asarray(t, F32) for t in (dmask, xi, zeta, gch))


def _rotary_inv_row():
    half = RET_DK // 2
    inv = (ROPE_BASE ** (-np.arange(half, dtype=np.float32) / half)).astype(np.float32)
    return jnp.asarray(np.tile(inv, LANES // half)[None, :], F32)


def _split_w_in(w_in):
    sizes = (DN_QKV, DN_HEADS, DN_HEADS, DN_VW, RET_QK, RET_QK, RET_VW, RET_VW, LRU_WIDTH, LRU_WIDTH,
             N_BRANCH * D_MODEL)
    offs = np.concatenate([[0], np.cumsum(sizes)])
    w_in = w_in.astype(BF16)
    part = lambda n: w_in[..., int(offs[n]):int(offs[n + 1])]
    main = jnp.concatenate([part(0), part(4), part(5), part(6), part(7), part(8), part(9), part(10), part(3)],
                           axis=-1)
    ab = jnp.concatenate([part(1), part(2)], axis=-1)
    w_ab = jnp.pad(ab, ((0, 0), (0, 0), (0, LANES - 2 * DN_HEADS)))
    return main, w_ab


def kernel(x, positions, attn_norm, w_in, dn_conv_w, dn_a_log, dn_dt_bias, dn_norm_w, lru_conv_w, lru_conv_b,
           lru_wa, lru_ba, lru_wx, lru_bx, lru_lambda, w_branch, w_out, ffn_norm, w_up, ffn_conv_w, ffn_conv_b,
           w_down, final_norm):
    bsz, seq, _ = x.shape
    m = bsz * seq
    depth = w_in.shape[0]
    ret_tables = _retention_tables(min(RET_CHUNK, seq))
    rope_cos, rope_sin = _rope_tables(positions, _rotary_inv_row())
    w_main, w_ab = _split_w_in(w_in)
    hp = jnp.stack([dn_a_log, dn_dt_bias], axis=1).astype(F32)
    hpr = jnp.pad(hp, ((0, 0), (0, SUBLANES - 2), (0, LANES - DN_HEADS)))
    hpc = jnp.pad(jnp.swapaxes(hp, 1, 2), ((0, 0), (0, AB_ROWS - DN_HEADS), (0, LANES - 2)))
    wb = w_branch.astype(BF16)
    wo = w_out.astype(BF16)
    wup = w_up.astype(BF16)
    wdn = w_down.astype(BF16)
    wax = jnp.concatenate([lru_wa, lru_wx], axis=-1).astype(BF16)

    x2 = x.reshape(m, D_MODEL)
    for l in range(depth):
        u2, ab, abt = _in_proj(x2, attn_norm[l][None, :], w_main, w_ab, dn_conv_w[l], lru_conv_w[l],
                               lru_conv_b[l][None, :], l, seq)
        dn_p = (hpr[l], hpc[l], dn_norm_w[l][None, :])
        lru_p = (wax[l], lru_ba[l].reshape(1, LRU_WIDTH), lru_bx[l].reshape(1, LRU_WIDTH),
                 lru_lambda[l][None, :])
        ya, yb, yc = _mixer(u2.reshape(bsz, seq, N_MAIN), ab, abt, rope_cos, rope_sin, ret_tables, dn_p, lru_p)
        x2 = _merge(ya.reshape(m, DN_VW), yb.reshape(m, RET_VW), yc.reshape(m, LRU_WIDTH), u2, x2, wb, wo, l)
        x3 = _conv_ffn(x2.reshape(bsz, seq, D_MODEL), ffn_norm[l][None, :], wup, ffn_conv_w[l],
                       ffn_conv_b[l][None, :], wdn, final_norm[None, :], l, final=(l == depth - 1))
        x2 = x3.reshape(m, D_MODEL)
    return x2.reshape(bsz, seq, D_MODEL)
```

```python
import functools
import math

import numpy as np
import jax
import jax.numpy as jnp
from jax import lax
from jax.experimental import pallas as pl
from jax.experimental.pallas import tpu as pltpu

F32 = jnp.float32
BF16 = jnp.bfloat16
F32_MIN_NORMAL = float(np.finfo(np.float32).tiny)

D_MODEL = 1024
CHUNK = 64
EPS = 1e-6
N_BRANCH = 3
BRANCH_WIDTH = 512

DN_HEADS = 4
DN_DK = 128
DN_DV = 128
DN_CONV = 4
DN_QK = DN_HEADS * DN_DK
DN_VW = DN_HEADS * DN_DV
DN_QKV = 2 * DN_QK + DN_VW

RET_HEADS = 4
RET_DK = 64
RET_DV = 128
RET_QK = RET_HEADS * RET_DK
RET_VW = RET_HEADS * RET_DV
ROPE_BASE = 10000.0

LRU_WIDTH = 512
LRU_BLOCKS = 4
LRU_BLOCK = LRU_WIDTH // LRU_BLOCKS
LRU_CONV = 4
LRU_C = 8.0

D_FF = 2816
FFN_CONV = 3

SUBLANES = 8
LANES = 128
V7X_VMEM_BYTES = 64 * 1024 * 1024
VMEM_LIMIT_BYTES = V7X_VMEM_BYTES * 7 // 8

RET_W = 2 * RET_QK + 2 * RET_VW
LRU_W = 2 * LRU_WIDTH
OFF_DN = 0
OFF_RET = OFF_DN + DN_QKV
OFF_LRU = OFF_RET + RET_W
OFF_GATE = OFF_LRU + LRU_W
OFF_Z = OFF_GATE + N_BRANCH * D_MODEL
N_MAIN = OFF_Z + DN_VW
AB_ROWS = 16

TM_IN = 512
IN_COLS = 256
TT_MIX = 512
RET_CHUNK = 256
DN_BLK = 2 * CHUNK
TM_MERGE = 512
TT_FFN = 512
TT_ROPE = 2048
FFN_COLS = 256


def _params(sem):
    return pltpu.CompilerParams(dimension_semantics=sem, vmem_limit_bytes=VMEM_LIMIT_BYTES)


def _sigmoid(x):
    return 1.0 / (1.0 + jnp.exp(-x))


def _silu(x):
    return x * _sigmoid(x)


def _gelu_tanh(x):
    return 0.5 * x * (1.0 + jnp.tanh(math.sqrt(2.0 / math.pi) * (x + 0.044715 * (x * x * x))))


def _softplus(x):
    return jnp.maximum(x, 0.0) + jnp.log1p(jnp.exp(-jnp.abs(x)))


def _dot(a, b):
    return jnp.dot(a.astype(BF16), b.astype(BF16), preferred_element_type=F32)


def _dot_nt(a, b):
    return lax.dot_general(a.astype(BF16), b.astype(BF16), (((1,), (1,)), ((), ())),
                           preferred_element_type=F32)


def _dot_tn(a, b):
    return lax.dot_general(a.astype(BF16), b.astype(BF16), (((0,), (0,)), ((), ())),
                           preferred_element_type=F32)


def _iota2(shape, dim):
    return lax.broadcasted_iota(jnp.int32, shape, dim)


def _rows_to_slabs(scr, x):
    tt = x.shape[0]
    ns = tt // SUBLANES
    panels = x.shape[1] // LANES
    for p in range(panels):
        scr[p] = x[:, p * LANES:(p + 1) * LANES]
    return jnp.concatenate(
        [jnp.concatenate([scr[p, pl.ds(i, ns, stride=SUBLANES), :] for p in range(panels)], axis=1)
         for i in range(SUBLANES)], axis=0)


def _slabs_to_rows(scr, y):
    tt = y.shape[0]
    ns = tt // SUBLANES
    panels = y.shape[1] // LANES
    for p in range(panels):
        for i in range(SUBLANES):
            scr[p, pl.ds(i, ns, stride=SUBLANES), :] = y[i * ns:(i + 1) * ns, p * LANES:(p + 1) * LANES]
    return jnp.concatenate([scr[p] for p in range(panels)], axis=1)


def _causal_conv_slabs(tail_scr, w_ref, cols, xs, width):
    ns = xs.shape[0] // SUBLANES
    group0 = _iota2((ns, xs.shape[1]), 0) == 0
    slabs = [xs[i * ns:(i + 1) * ns, :] for i in range(SUBLANES)]
    back = [jnp.where(group0, tail_scr[i:i + 1, cols], pltpu.roll(slabs[i], 1, 0))
            for i in range(SUBLANES - (width - 1), SUBLANES)]
    for i in range(SUBLANES - (width - 1), SUBLANES):
        tail_scr[i:i + 1, cols] = slabs[i][ns - 1:ns, :]
    ext = back + slabs
    out = []
    for i in range(SUBLANES):
        y = ext[i] * w_ref[0:1, cols]
        for j in range(1, width):
            y = y + ext[i + j] * w_ref[j:j + 1, cols]
        out.append(y)
    return jnp.concatenate(out, axis=0)


def _in_proj_chunk_order():
    starts = list(range(0, N_MAIN, IN_COLS))
    heavy = [c for c in starts if c < OFF_DN + DN_QKV or OFF_LRU <= c < OFF_LRU + LRU_WIDTH]
    plain = [c for c in starts if c not in heavy]
    order, per = [], len(plain) // len(heavy)
    for i, c in enumerate(heavy):
        order.append(c)
        order.extend(plain[i * per:(i + 1) * per])
    order.extend(plain[len(heavy) * per:])
    assert sorted(order) == starts
    return [(c, IN_COLS) for c in order]


def _in_proj_kernel(x_ref, g_ref, w_ref, wab_ref, dncw_ref, lrucw_ref, lrucb_ref,
                    u_ref, ab_ref, abt_ref, dn_tail, lru_tail, perm_scr, *, tiles_per_seq):
    @pl.when(pl.program_id(0) % tiles_per_seq == 0)
    def _():
        dn_tail[...] = jnp.zeros(dn_tail.shape, F32)
        lru_tail[...] = jnp.zeros(lru_tail.shape, F32)

    x = x_ref[...]
    hb = (x * lax.rsqrt(jnp.mean(x * x, axis=-1, keepdims=True) + EPS) * g_ref[...]).astype(BF16)

    n_conv = 0
    for c0, width in _in_proj_chunk_order():
        p = jnp.dot(hb, w_ref[:, c0:c0 + width], preferred_element_type=F32)
        is_dn = c0 < OFF_DN + DN_QKV
        is_lru = OFF_LRU <= c0 < OFF_LRU + LRU_WIDTH
        if is_dn or is_lru:
            ps = _rows_to_slabs(perm_scr.at[2 * n_conv], p)
            if is_dn:
                cols = slice(c0 - OFF_DN, c0 - OFF_DN + IN_COLS)
                ps = _silu(_causal_conv_slabs(dn_tail, dncw_ref, cols, ps, DN_CONV))
                if c0 < OFF_DN + 2 * DN_QK:
                    scale = DN_DK ** -0.5 if c0 < OFF_DN + DN_QK else 1.0
                    heads = [ps[:, h0:h0 + DN_DK] for h0 in range(0, IN_COLS, DN_DK)]
                    ps = jnp.concatenate(
                        [t * (lax.rsqrt(jnp.sum(t * t, axis=-1, keepdims=True) + EPS) * scale) for t in heads],
                        axis=1)
            else:
                cols = slice(c0 - OFF_LRU, c0 - OFF_LRU + IN_COLS)
                ps = _causal_conv_slabs(lru_tail, lrucw_ref, cols, ps, LRU_CONV) + lrucb_ref[:, cols]
            p = _slabs_to_rows(perm_scr.at[2 * n_conv + 1], ps)
            n_conv += 1
        elif OFF_RET + 2 * RET_QK + RET_VW <= c0 < OFF_RET + RET_W or c0 >= OFF_Z:
            p = _silu(p)
        elif OFF_LRU + LRU_WIDTH <= c0 < OFF_LRU + LRU_W:
            p = _gelu_tanh(p)
        u_ref[:, c0:c0 + width] = p.astype(BF16)
    ab = jnp.dot(hb, wab_ref[...], preferred_element_type=F32)
    ab_ref[...] = ab
    abt_ref[...] = ab.T[0:AB_ROWS, :]


def _in_proj(x2, g, w_main, w_ab, dn_conv_w, lru_conv_w, lru_conv_b, layer, seq):
    m = x2.shape[0]
    tm = min(TM_IN, seq)
    return pl.pallas_call(
        functools.partial(_in_proj_kernel, tiles_per_seq=seq // tm),
        grid=(m // tm,),
        in_specs=[
            pl.BlockSpec((tm, D_MODEL), lambda i: (i, 0)),
            pl.BlockSpec((1, D_MODEL), lambda i: (0, 0)),
            pl.BlockSpec((None, D_MODEL, N_MAIN), lambda i: (layer, 0, 0), pipeline_mode=pl.Buffered(1)),
            pl.BlockSpec((None, D_MODEL, LANES), lambda i: (layer, 0, 0)),
            pl.BlockSpec((DN_CONV, DN_QKV), lambda i: (0, 0)),
            pl.BlockSpec((LRU_CONV, LRU_WIDTH), lambda i: (0, 0)),
            pl.BlockSpec((1, LRU_WIDTH), lambda i: (0, 0)),
        ],
        out_specs=[
            pl.BlockSpec((tm, N_MAIN), lambda i: (i, 0)),
            pl.BlockSpec((tm, LANES), lambda i: (i, 0)),
            pl.BlockSpec((AB_ROWS, tm), lambda i: (0, i)),
        ],
        out_shape=[
            jax.ShapeDtypeStruct((m, N_MAIN), BF16),
            jax.ShapeDtypeStruct((m, LANES), F32),
            jax.ShapeDtypeStruct((AB_ROWS, m), F32),
        ],
        scratch_shapes=[
            pltpu.VMEM((SUBLANES, DN_QKV), F32),
            pltpu.VMEM((SUBLANES, LRU_WIDTH), F32),
            pltpu.VMEM((2 * (DN_QKV + LRU_WIDTH) // IN_COLS, IN_COLS // LANES, tm, LANES), F32),
        ],
        compiler_params=_params(("arbitrary",)),
        name="in_proj",
    )(x2, g, w_main, w_ab, dn_conv_w, lru_conv_w, lru_conv_b)


def _second_rows(x, half):
    return jnp.concatenate([x[r:r + half, :] for r in range(half, x.shape[0], 2 * half)], axis=0)


def _set_second_rows(x, rows, half):
    parts = []
    for j, r in enumerate(range(0, x.shape[0], 2 * half)):
        parts += [x[r:r + half, :], rows[j * half:(j + 1) * half, :]]
    return jnp.concatenate(parts, axis=0)


def _dn_kernel(qkv_ref, z_ref, ab_ref, abt_ref, hpr_ref, hpc_ref, normw_ref, o_ref, s_scr, *, tt):
    a_in = qkv_ref[0]
    a = a_in.astype(F32)

    ab = ab_ref[...]
    abt = abt_ref[...]
    g_cols = -jnp.exp(hpr_ref[0:1, :]) * _softplus(ab + hpr_ref[1:2, :])
    beta_cols = _sigmoid(ab)
    g_rows = -jnp.exp(hpc_ref[:, 0:1]) * _softplus(abt + hpc_ref[:, 1:2])

    ri = _iota2((DN_BLK, DN_BLK), 0)
    ci = _iota2((DN_BLK, DN_BLK), 1)
    xr = ri ^ ci
    lower = ri > ci
    causal = (xr < CHUNK) & (ri >= ci)
    eye = jnp.where(ri == ci, 1.0, 0.0).astype(F32)

    gc_cols, gc_rows = g_cols, g_rows
    t_col = _iota2(g_cols.shape, 0) % CHUNK
    t_row = _iota2(g_rows.shape, 1) % CHUNK
    s = 1
    while s < CHUNK:
        gc_cols = gc_cols + jnp.where(t_col >= s, pltpu.roll(gc_cols, s, 0), 0.0)
        gc_rows = gc_rows + jnp.where(t_row >= s, pltpu.roll(gc_rows, s, 1), 0.0)
        s *= 2
    normw = normw_ref[...]
    z = z_ref[0].astype(F32)

    heads = range(DN_HEADS)
    q, k, v, gcc, beta, eg, k_beta = [], [], [], [], [], [], []
    for h in heads:
        q.append(a[:, h * DN_DK:(h + 1) * DN_DK])
        k.append(a[:, DN_QK + h * DN_DK:DN_QK + (h + 1) * DN_DK])
        v.append(a[:, 2 * DN_QK + h * DN_DV:2 * DN_QK + (h + 1) * DN_DV])
        gcc.append(gc_cols[:, h:h + 1])
        beta.append(beta_cols[:, DN_HEADS + h:DN_HEADS + h + 1])
        eg.append(jnp.exp(gcc[h]))
        k_beta.append(k[h] * beta[h])
    rhs = [jnp.concatenate([v[h] * beta[h], k_beta[h] * eg[h]], axis=1) for h in heads]
    q_dec = [q[h] * eg[h] for h in heads]

    probs = [(h, b) for h in heads for b in range(tt // DN_BLK)]
    rows_of = lambda b: slice(b * DN_BLK, (b + 1) * DN_BLK)
    decay = [jnp.where(causal, jnp.exp(gcc[h][rows_of(b)] - gc_rows[h:h + 1, rows_of(b)]), 0.0)
             for h, b in probs]
    k_op = lambda h, b: a_in[rows_of(b), DN_QK + h * DN_DK:DN_QK + (h + 1) * DN_DK]
    q_op = lambda h, b: a_in[rows_of(b), h * DN_DK:(h + 1) * DN_DK]
    low = [jnp.where(lower, _dot_nt(k_beta[h][rows_of(b)], k_op(h, b)) * decay[i], 0.0)
           for i, (h, b) in enumerate(probs)]
    attn = [_dot_nt(q_op(h, b), k_op(h, b)) * decay[i] for i, (h, b) in enumerate(probs)]
    n = range(len(probs))
    d4 = [jnp.where(xr < 4, low[i], 0.0) for i in n]
    d4sq = [_dot(d4[i], d4[i]) for i in n]
    t_inv = [(eye - d4[i]) + _dot(eye - d4[i], d4sq[i]) for i in n]
    for lv in range(2, 6):
        half = 1 << lv
        off = [jnp.where((xr >> lv) == 1, low[i], 0.0) for i in n]
        if half < SUBLANES:
            off_t = [_dot(off[i], t_inv[i]) for i in n]
            t_inv = [t_inv[i] - _dot(t_inv[i], off_t[i]) for i in n]
        else:
            off_t = [_dot(_second_rows(off[i], half), t_inv[i]) for i in n]
            t_low = [_second_rows(t_inv[i], half) for i in n]
            upd = [t_low[i] - _dot(t_low[i], _set_second_rows(jnp.zeros_like(t_inv[i]), off_t[i], half))
                   for i in n]
            t_inv = [_set_second_rows(t_inv[i], upd[i], half) for i in n]
    uw = {hb: _dot(t_inv[i], rhs[hb[0]][rows_of(hb[1])]) for i, hb in enumerate(probs)}
    attn = {hb: attn[i] for i, hb in enumerate(probs)}

    state = [s_scr[h] for h in heads]
    per_blk = DN_BLK // CHUNK
    for c in range(tt // CHUNK):
        rows = slice(c * CHUNK, (c + 1) * CHUNK)
        b = c // per_blk
        loc = slice((c % per_blk) * CHUNK, (c % per_blk + 1) * CHUNK)
        g_last = [gcc[h][(c + 1) * CHUNK - 1:(c + 1) * CHUNK, :] for h in heads]
        v_new = [uw[h, b][loc, :DN_DV] - _dot(uw[h, b][loc, DN_DV:], state[h]) for h in heads]
        out = [_dot(q_dec[h][rows], state[h]) + _dot(attn[h, b][loc, loc], v_new[h]) for h in heads]
        k_dec = [k[h][rows] * jnp.exp(g_last[h] - gcc[h][rows]) for h in heads]
        state = [state[h] * jnp.exp(g_last[h]) + _dot_tn(k_dec[h], v_new[h]) for h in heads]
        for h in heads:
            o = out[h] * lax.rsqrt(jnp.mean(out[h] * out[h], axis=-1, keepdims=True) + EPS)
            o = o * normw * z[rows, h * DN_DV:(h + 1) * DN_DV]
            o_ref[0, rows, h * DN_DV:(h + 1) * DN_DV] = o.astype(BF16)
    for h in heads:
        s_scr[h] = state[h]


def _ret_kernel(r_ref, cos_ref, sin_ref, dmask_ref, xi_ref, zeta_ref, gch_ref, o_ref, s_scr, *, tt):
    r_in = r_ref[0]
    r = r_in[:, 0:2 * RET_QK].astype(F32)
    cos = cos_ref[0]
    sin_signed = sin_ref[0]
    first_half = (_iota2((tt, LANES), 1) % RET_DK) < (RET_DK // 2)

    def rotary(t):
        swapped = jnp.where(first_half, pltpu.roll(t, LANES - RET_DK // 2, 1),
                            pltpu.roll(t, RET_DK // 2, 1))
        return t * cos + swapped * sin_signed

    q = jnp.concatenate([rotary(r[:, i * LANES:(i + 1) * LANES]) for i in range(RET_QK // LANES)], axis=1)
    k = jnp.concatenate([rotary(r[:, RET_QK + i * LANES:RET_QK + (i + 1) * LANES])
                         for i in range(RET_QK // LANES)], axis=1) * (RET_DK ** -0.5)
    v = r_in[:, 2 * RET_QK:2 * RET_QK + RET_VW]
    gate = r_in[:, 2 * RET_QK + RET_VW:].astype(F32)

    heads = range(RET_HEADS)
    state = [s_scr[h] for h in heads]
    chunk = dmask_ref.shape[1]
    for r0 in range(0, tt, chunk):
        rows = slice(r0, r0 + chunk)
        qh = [q[rows, h * RET_DK:(h + 1) * RET_DK] for h in heads]
        kh = [k[rows, h * RET_DK:(h + 1) * RET_DK] for h in heads]
        vh = [v[rows, h * RET_DV:(h + 1) * RET_DV] for h in heads]
        scores = [_dot_nt(qh[h], kh[h]) * dmask_ref[h] for h in heads]
        cross = [_dot(qh[h], state[h]) * xi_ref[:, h:h + 1] for h in heads]
        inner = [_dot(scores[h], vh[h]) for h in heads]
        state = [state[h] * gch_ref[h:h + 1, :] + _dot_tn(kh[h] * zeta_ref[:, h:h + 1], vh[h]) for h in heads]
        for h in heads:
            out = cross[h] + inner[h]
            mu = jnp.mean(out, axis=-1, keepdims=True)
            cen = out - mu
            var = jnp.mean(cen * cen, axis=-1, keepdims=True)
            out = cen * lax.rsqrt(var + EPS) * gate[rows, h * RET_DV:(h + 1) * RET_DV]
            o_ref[0, rows, h * RET_DV:(h + 1) * RET_DV] = out.astype(BF16)
    for h in heads:
        s_scr[h] = state[h]


def _lru_kernel(l_ref, wax_ref, ba_ref, bx_ref, lam_ref, o_ref, h_scr, perm_scr, *, tt):
    l = l_ref[0]
    xc = _rows_to_slabs(perm_scr.at[0], l[:, 0:LRU_WIDTH].astype(F32))
    rs, gs = [], []
    for n in range(LRU_BLOCKS):
        xb = xc[:, n * LRU_BLOCK:(n + 1) * LRU_BLOCK].astype(BF16)
        both = jnp.dot(xb, wax_ref[n], preferred_element_type=F32)
        rs.append(both[:, :LRU_BLOCK])
        gs.append(both[:, LRU_BLOCK:])
    rgate = _sigmoid(jnp.concatenate(rs, axis=1) + ba_ref[...])
    igate = _sigmoid(jnp.concatenate(gs, axis=1) + bx_ref[...])
    log_a = (-LRU_C) * rgate * _softplus(-lam_ref[...])
    a = jnp.exp(log_a)
    one_minus = -jnp.tanh(log_a) * (a * a + 1.0)
    bv = one_minus * lax.rsqrt(jnp.maximum(one_minus, F32_MIN_NORMAL)) * (igate * xc)

    ns = tt // SUBLANES
    hs = [bv[0:ns, :]]
    ps = [a[0:ns, :]]
    for i in range(1, SUBLANES):
        ai = a[i * ns:(i + 1) * ns, :]
        hs.append(ai * hs[-1] + bv[i * ns:(i + 1) * ns, :])
        ps.append(ai * ps[-1])
    pg, hg = ps[-1], hs[-1]
    row = _iota2((ns, LRU_WIDTH), 0)
    s = 1
    while s < ns:
        keep = row >= s
        p_prev = jnp.where(keep, pltpu.roll(pg, s, 0), 1.0)
        h_prev = jnp.where(keep, pltpu.roll(hg, s, 0), 0.0)
        hg = pg * h_prev + hg
        pg = pg * p_prev
        s *= 2
    h0 = h_scr[...]
    after = pg * h0 + hg
    h_scr[...] = after[ns - 1:ns, :]
    before = jnp.where(row == 0, h0, pltpu.roll(after, 1, 0))
    hseq = _slabs_to_rows(perm_scr.at[1], jnp.concatenate([hs[i] + ps[i] * before for i in range(SUBLANES)],
                                                          axis=0))
    o_ref[0] = (hseq * l[:, LRU_WIDTH:].astype(F32)).astype(BF16)


def _rope_kernel(pos_ref, inv_ref, cos_ref, sin_ref):
    ang = pos_ref[0].astype(F32) * inv_ref[...]
    first_half = (_iota2(ang.shape, 1) % RET_DK) < (RET_DK // 2)
    sin = jnp.sin(ang)
    cos_ref[0] = jnp.cos(ang)
    sin_ref[0] = jnp.where(first_half, -sin, sin)


def _rope_tables(positions, inv_row):
    b, t = positions.shape
    tt = min(TT_ROPE, t)
    spec = pl.BlockSpec((1, tt, LANES), lambda i, j: (i, j, 0))
    shape = jax.ShapeDtypeStruct((b, t, LANES), F32)
    pos3 = jnp.broadcast_to(positions[:, :, None], (b, t, LANES))
    return pl.pallas_call(
        _rope_kernel,
        grid=(b, t // tt),
        in_specs=[spec, pl.BlockSpec((1, LANES), lambda i, j: (0, 0))],
        out_specs=[spec, spec],
        out_shape=[shape, shape],
        compiler_params=_params(("parallel", "parallel")),
        name="rope_tables",
    )(pos3, inv_row)


N_DN_IN, N_RET_IN, N_LRU_IN = 7, 7, 5


def _mixer_kernel(*refs, tt):
    dn_in = refs[:N_DN_IN]
    ret_in = refs[N_DN_IN:N_DN_IN + N_RET_IN]
    lru_in = refs[N_DN_IN + N_RET_IN:N_DN_IN + N_RET_IN + N_LRU_IN]
    ya_ref, yb_ref, yc_ref, dn_state, ret_state, lru_state, lru_perm = refs[N_DN_IN + N_RET_IN + N_LRU_IN:]

    @pl.when(pl.program_id(1) == 0)
    def _():
        for scr in (dn_state, ret_state, lru_state):
            scr[...] = jnp.zeros(scr.shape, F32)

    _ret_kernel(*ret_in, yb_ref, ret_state, tt=tt)
    _dn_kernel(*dn_in, ya_ref, dn_state, tt=tt)
    _lru_kernel(*lru_in, yc_ref, lru_state, lru_perm, tt=tt)


def _mixer(u3, ab, abt, rope_cos, rope_sin, ret_tables, dn_p, lru_p):
    b, t, _ = u3.shape
    tt = min(TT_MIX, t)
    nt = t // tt
    full2 = lambda i, j: (0, 0)
    full3 = lambda i, j: (0, 0, 0)
    rc = ret_tables[0].shape[1]
    dn_specs = [
        pl.BlockSpec((1, tt, DN_QKV), lambda i, j: (i, j, OFF_DN // DN_QKV)),
        pl.BlockSpec((1, tt, DN_VW), lambda i, j: (i, j, OFF_Z // DN_VW)),
        pl.BlockSpec((tt, LANES), lambda i, j: (i * nt + j, 0)),
        pl.BlockSpec((AB_ROWS, tt), lambda i, j: (0, i * nt + j)),
        pl.BlockSpec((SUBLANES, LANES), full2),
        pl.BlockSpec((AB_ROWS, LANES), full2),
        pl.BlockSpec((1, DN_DV), full2),
    ]
    ret_specs = [
        pl.BlockSpec((1, tt, RET_W), lambda i, j: (i, j, OFF_RET // RET_W)),
        pl.BlockSpec((1, tt, LANES), lambda i, j: (i, j, 0)),
        pl.BlockSpec((1, tt, LANES), lambda i, j: (i, j, 0)),
        pl.BlockSpec((RET_HEADS, rc, rc), full3),
        pl.BlockSpec((rc, LANES), full2),
        pl.BlockSpec((rc, LANES), full2),
        pl.BlockSpec((SUBLANES, LANES), full2),
    ]
    lru_specs = [
        pl.BlockSpec((1, tt, LRU_W), lambda i, j: (i, j, OFF_LRU // LRU_W)),
        pl.BlockSpec((LRU_BLOCKS, LRU_BLOCK, 2 * LRU_BLOCK), full3),
        pl.BlockSpec((1, LRU_WIDTH), full2),
        pl.BlockSpec((1, LRU_WIDTH), full2),
        pl.BlockSpec((1, LRU_WIDTH), full2),
    ]
    assert (len(dn_specs), len(ret_specs), len(lru_specs)) == (N_DN_IN, N_RET_IN, N_LRU_IN)
    out_spec = pl.BlockSpec((1, tt, BRANCH_WIDTH), lambda i, j: (i, j, 0))
    out_shape = jax.ShapeDtypeStruct((b, t, BRANCH_WIDTH), BF16)
    return pl.pallas_call(
        functools.partial(_mixer_kernel, tt=tt),
        grid=(b, nt),
        in_specs=dn_specs + ret_specs + lru_specs,
        out_specs=[out_spec] * N_BRANCH,
        out_shape=[out_shape] * N_BRANCH,
        scratch_shapes=[
            pltpu.VMEM((DN_HEADS, DN_DK, DN_DV), F32),
            pltpu.VMEM((RET_HEADS, RET_DK, RET_DV), F32),
            pltpu.VMEM((1, LRU_WIDTH), F32),
            pltpu.VMEM((2, LRU_WIDTH // LANES, tt, LANES), F32),
        ],
        compiler_params=_params(("parallel", "arbitrary")),
        name="mixer",
    )(u3, u3, ab, abt, *dn_p, u3, rope_cos, rope_sin, *ret_tables, u3, *lru_p)


def _merge_kernel(ya_ref, yb_ref, yc_ref, g0_ref, g1_ref, g2_ref, x_ref, wb_ref, wo_ref, o_ref):
    merged = None
    for y_ref, g_ref, n in ((ya_ref, g0_ref, 0), (yb_ref, g1_ref, 1), (yc_ref, g2_ref, 2)):
        term = _sigmoid(g_ref[...].astype(F32)) * jnp.dot(y_ref[...], wb_ref[n], preferred_element_type=F32)
        merged = term if merged is None else merged + term
    o_ref[...] = x_ref[...] + jnp.dot(merged.astype(BF16), wo_ref[...], preferred_element_type=F32)


def _merge(ya, yb, yc, u2, x2, wb, wo, layer):
    m = x2.shape[0]
    tm = min(TM_MERGE, m)
    gate_blk = OFF_GATE // D_MODEL
    yspec = pl.BlockSpec((tm, BRANCH_WIDTH), lambda i: (i, 0))
    return pl.pallas_call(
        _merge_kernel,
        grid=(m // tm,),
        in_specs=[
            yspec, yspec, yspec,
            pl.BlockSpec((tm, D_MODEL), lambda i: (i, gate_blk)),
            pl.BlockSpec((tm, D_MODEL), lambda i: (i, gate_blk + 1)),
            pl.BlockSpec((tm, D_MODEL), lambda i: (i, gate_blk + 2)),
            pl.BlockSpec((tm, D_MODEL), lambda i: (i, 0)),
            pl.BlockSpec((None, N_BRANCH, BRANCH_WIDTH, D_MODEL), lambda i: (layer, 0, 0, 0)),
            pl.BlockSpec((None, D_MODEL, D_MODEL), lambda i: (layer, 0, 0)),
        ],
        out_specs=pl.BlockSpec((tm, D_MODEL), lambda i: (i, 0)),
        out_shape=jax.ShapeDtypeStruct((m, D_MODEL), F32),
        compiler_params=_params(("parallel",)),
        name="merge",
    )(ya, yb, yc, u2, u2, u2, x2, wb, wo)


def _ffn_kernel(x_ref, g_ref, wup_ref, cw_ref, cb_ref, wdn_ref, fg_ref, o_ref,
                tail_scr, act_scr, perm_scr, *, tt, final):
    @pl.when(pl.program_id(1) == 0)
    def _():
        tail_scr[...] = jnp.zeros(tail_scr.shape, F32)

    x = _rows_to_slabs(perm_scr, x_ref[0])
    hb = (x * lax.rsqrt(jnp.mean(x * x, axis=-1, keepdims=True) + EPS) * g_ref[...]).astype(BF16)

    def conv(pre, col0):
        cols = slice(col0, col0 + FFN_COLS)
        return _causal_conv_slabs(tail_scr, cw_ref, cols, pre, FFN_CONV) + cb_ref[:, cols]

    for c in range(D_FF // FFN_COLS):
        c0 = c * FFN_COLS
        gate = conv(jnp.dot(hb, wup_ref[:, c0:c0 + FFN_COLS], preferred_element_type=F32), c0)
        val = conv(jnp.dot(hb, wup_ref[:, D_FF + c0:D_FF + c0 + FFN_COLS], preferred_element_type=F32),
                   D_FF + c0)
        act_scr[:, c0:c0 + FFN_COLS] = (_silu(gate) * val).astype(BF16)

    y = x + jnp.dot(act_scr[...], wdn_ref[...], preferred_element_type=F32)
    if final:
        y = y * lax.rsqrt(jnp.mean(y * y, axis=-1, keepdims=True) + EPS) * fg_ref[...]
    o_ref[0] = _slabs_to_rows(perm_scr, y)


def _conv_ffn(x3, g, w_up, conv_w, conv_b, w_down, final_g, layer, final):
    b, t, _ = x3.shape
    tt = min(TT_FFN, t)
    full2 = lambda i, j: (0, 0)
    of_layer = lambda i, j: (layer, 0, 0)
    return pl.pallas_call(
        functools.partial(_ffn_kernel, tt=tt, final=final),
        grid=(b, t // tt),
        in_specs=[
            pl.BlockSpec((1, tt, D_MODEL), lambda i, j: (i, j, 0)),
            pl.BlockSpec((1, D_MODEL), full2),
            pl.BlockSpec((None, D_MODEL, 2 * D_FF), of_layer, pipeline_mode=pl.Buffered(1)),
            pl.BlockSpec((FFN_CONV, 2 * D_FF), full2),
            pl.BlockSpec((1, 2 * D_FF), full2),
            pl.BlockSpec((None, D_FF, D_MODEL), of_layer, pipeline_mode=pl.Buffered(1)),
            pl.BlockSpec((1, D_MODEL), full2),
        ],
        out_specs=pl.BlockSpec((1, tt, D_MODEL), lambda i, j: (i, j, 0)),
        out_shape=jax.ShapeDtypeStruct((b, t, D_MODEL), F32),
        scratch_shapes=[
            pltpu.VMEM((SUBLANES, 2 * D_FF), F32),
            pltpu.VMEM((tt, D_FF), BF16),
            pltpu.VMEM((D_MODEL // LANES, tt, LANES), F32),
        ],
        compiler_params=_params(("parallel", "arbitrary")),
        name="conv_ffn",
    )(x3, g, w_up, conv_w, conv_b, w_down, final_g)


def _retention_tables(chunk):
    log_gamma = np.log(1.0 - 2.0 ** (-5.0 - np.arange(RET_HEADS, dtype=np.float64)))
    idx = np.arange(chunk, dtype=np.float64)
    dist = idx[:, None] - idx[None, :]
    causal = dist >= 0
    dmask = np.where(causal, np.exp(np.where(causal, dist, 0.0) * log_gamma[:, None, None]), 0.0)
    xi = np.zeros((chunk, LANES), np.float64)
    zeta = np.zeros((chunk, LANES), np.float64)
    xi[:, :RET_HEADS] = np.exp((idx[:, None] + 1.0) * log_gamma[None, :])
    zeta[:, :RET_HEADS] = np.exp((chunk - 1.0 - idx[:, None]) * log_gamma[None, :])
    gch = np.zeros((SUBLANES, LANES), np.float64)
    gch[:RET_HEADS, :] = np.exp(chunk * log_gamma)[:, None]
    return tuple(jnp.asarray(t, F32) for t in (dmask, xi, zeta, gch))


def _rotary_inv_row():
    half = RET_DK // 2
    inv = (ROPE_BASE ** (-np.arange(half, dtype=np.float32) / half)).astype(np.float32)
    return jnp.asarray(np.tile(inv, LANES // half)[None, :], F32)


def _split_w_in(w_in):
    sizes = (DN_QKV, DN_HEADS, DN_HEADS, DN_VW, RET_QK, RET_QK, RET_VW, RET_VW, LRU_WIDTH, LRU_WIDTH,
             N_BRANCH * D_MODEL)
    offs = np.concatenate([[0], np.cumsum(sizes)])
    w_in = w_in.astype(BF16)
    part = lambda n: w_in[..., int(offs[n]):int(offs[n + 1])]
    main = jnp.concatenate([part(0), part(4), part(5), part(6), part(7), part(8), part(9), part(10), part(3)],
                           axis=-1)
    ab = jnp.concatenate([part(1), part(2)], axis=-1)
    w_ab = jnp.pad(ab, ((0, 0), (0, 0), (0, LANES - 2 * DN_HEADS)))
    return main, w_ab


def kernel(x, positions, attn_norm, w_in, dn_conv_w, dn_a_log, dn_dt_bias, dn_norm_w, lru_conv_w, lru_conv_b,
           lru_wa, lru_ba, lru_wx, lru_bx, lru_lambda, w_branch, w_out, ffn_norm, w_up, ffn_conv_w, ffn_conv_b,
           w_down, final_norm):
    bsz, seq, _ = x.shape
    m = bsz * seq
    depth = w_in.shape[0]
    ret_tables = _retention_tables(min(RET_CHUNK, seq))
    rope_cos, rope_sin = _rope_tables(positions, _rotary_inv_row())
    w_main, w_ab = _split_w_in(w_in)
    hp = jnp.stack([dn_a_log, dn_dt_bias], axis=1).astype(F32)
    hpr = jnp.pad(hp, ((0, 0), (0, SUBLANES - 2), (0, LANES - DN_HEADS)))
    hpc = jnp.pad(jnp.swapaxes(hp, 1, 2), ((0, 0), (0, AB_ROWS - DN_HEADS), (0, LANES - 2)))
    wb = w_branch.astype(BF16)
    wo = w_out.astype(BF16)
    wup = w_up.astype(BF16)
    wdn = w_down.astype(BF16)
    wax = jnp.concatenate([lru_wa, lru_wx], axis=-1).astype(BF16)

    x2 = x.reshape(m, D_MODEL)
    for l in range(depth):
        u2, ab, abt = _in_proj(x2, attn_norm[l][None, :], w_main, w_ab, dn_conv_w[l], lru_conv_w[l],
                               lru_conv_b[l][None, :], l, seq)
        dn_p = (hpr[l], hpc[l], dn_norm_w[l][None, :])
        lru_p = (wax[l], lru_ba[l].reshape(1, LRU_WIDTH), lru_bx[l].reshape(1, LRU_WIDTH),
                 lru_lambda[l][None, :])
        ya, yb, yc = _mixer(u2.reshape(bsz, seq, N_MAIN), ab, abt, rope_cos, rope_sin, ret_tables, dn_p, lru_p)
        x2 = _merge(ya.reshape(m, DN_VW), yb.reshape(m, RET_VW), yc.reshape(m, LRU_WIDTH), u2, x2, wb, wo, l)
        x3 = _conv_ffn(x2.reshape(bsz, seq, D_MODEL), ffn_norm[l][None, :], wup, ffn_conv_w[l],
                       ffn_conv_b[l][None, :], wdn, final_norm[None, :], l, final=(l == depth - 1))
        x2 = x3.reshape(m, D_MODEL)
    return x2.reshape(bsz, seq, D_MODEL)
```

```python
import functools
import math

import numpy as np
import jax
import jax.numpy as jnp
from jax import lax
from jax.experimental import pallas as pl
from jax.experimental.pallas import tpu as pltpu

F32 = jnp.float32
BF16 = jnp.bfloat16
F32_MIN_NORMAL = float(np.finfo(np.float32).tiny)

D_MODEL = 1024
CHUNK = 64
EPS = 1e-6
N_BRANCH = 3
BRANCH_WIDTH = 512

DN_HEADS = 4
DN_DK = 128
DN_DV = 128
DN_CONV = 4
DN_QK = DN_HEADS * DN_DK
DN_VW = DN_HEADS * DN_DV
DN_QKV = 2 * DN_QK + DN_VW

RET_HEADS = 4
RET_DK = 64
RET_DV = 128
RET_QK = RET_HEADS * RET_DK
RET_VW = RET_HEADS * RET_DV
ROPE_BASE = 10000.0

LRU_WIDTH = 512
LRU_BLOCKS = 4
LRU_BLOCK = LRU_WIDTH // LRU_BLOCKS
LRU_CONV = 4
LRU_C = 8.0

D_FF = 2816
FFN_CONV = 3

SUBLANES = 8
LANES = 128
V7X_VMEM_BYTES = 64 * 1024 * 1024
VMEM_LIMIT_BYTES = V7X_VMEM_BYTES * 7 // 8

RET_W = 2 * RET_QK + 2 * RET_VW
LRU_W = 2 * LRU_WIDTH
OFF_DN = 0
OFF_RET = OFF_DN + DN_QKV
OFF_LRU = OFF_RET + RET_W
OFF_GATE = OFF_LRU + LRU_W
OFF_Z = OFF_GATE + N_BRANCH * D_MODEL
N_MAIN = OFF_Z + DN_VW
AB_ROWS = 16

TM_IN = 512
IN_COLS = 256
TT_MIX = 512
RET_CHUNK = 256
DN_BLK = 2 * CHUNK
TT_FFN = 512
TT_ROPE = 2048
FFN_COLS = 256


def _params(sem):
    return pltpu.CompilerParams(dimension_semantics=sem, vmem_limit_bytes=VMEM_LIMIT_BYTES)


def _sigmoid(x):
    return 1.0 / (1.0 + jnp.exp(-x))


def _silu(x):
    return x * _sigmoid(x)


def _gelu_tanh(x):
    return 0.5 * x * (1.0 + jnp.tanh(math.sqrt(2.0 / math.pi) * (x + 0.044715 * (x * x * x))))


def _softplus(x):
    return jnp.maximum(x, 0.0) + jnp.log1p(jnp.exp(-jnp.abs(x)))


def _dot(a, b):
    return jnp.dot(a.astype(BF16), b.astype(BF16), preferred_element_type=F32)


def _dot_nt(a, b):
    return lax.dot_general(a.astype(BF16), b.astype(BF16), (((1,), (1,)), ((), ())),
                           preferred_element_type=F32)


def _dot_tn(a, b):
    return lax.dot_general(a.astype(BF16), b.astype(BF16), (((0,), (0,)), ((), ())),
                           preferred_element_type=F32)


def _iota2(shape, dim):
    return lax.broadcasted_iota(jnp.int32, shape, dim)


def _rows_to_slabs(scr, x):
    tt = x.shape[0]
    ns = tt // SUBLANES
    panels = x.shape[1] // LANES
    for p in range(panels):
        scr[p] = x[:, p * LANES:(p + 1) * LANES]
    return jnp.concatenate(
        [jnp.concatenate([scr[p, pl.ds(i, ns, stride=SUBLANES), :] for p in range(panels)], axis=1)
         for i in range(SUBLANES)], axis=0)


def _slabs_to_rows(scr, y):
    tt = y.shape[0]
    ns = tt // SUBLANES
    panels = y.shape[1] // LANES
    for p in range(panels):
        for i in range(SUBLANES):
            scr[p, pl.ds(i, ns, stride=SUBLANES), :] = y[i * ns:(i + 1) * ns, p * LANES:(p + 1) * LANES]
    return jnp.concatenate([scr[p] for p in range(panels)], axis=1)


def _causal_conv_slabs(tail_scr, w_ref, cols, xs, width):
    ns = xs.shape[0] // SUBLANES
    group0 = _iota2((ns, xs.shape[1]), 0) == 0
    slabs = [xs[i * ns:(i + 1) * ns, :] for i in range(SUBLANES)]
    back = [jnp.where(group0, tail_scr[i:i + 1, cols], pltpu.roll(slabs[i], 1, 0))
            for i in range(SUBLANES - (width - 1), SUBLANES)]
    for i in range(SUBLANES - (width - 1), SUBLANES):
        tail_scr[i:i + 1, cols] = slabs[i][ns - 1:ns, :]
    ext = back + slabs
    out = []
    for i in range(SUBLANES):
        y = ext[i] * w_ref[0:1, cols]
        for j in range(1, width):
            y = y + ext[i + j] * w_ref[j:j + 1, cols]
        out.append(y)
    return jnp.concatenate(out, axis=0)


def _in_proj_chunk_order():
    starts = list(range(0, N_MAIN, IN_COLS))
    heavy = [c for c in starts if c < OFF_DN + DN_QKV or OFF_LRU <= c < OFF_LRU + LRU_WIDTH]
    plain = [c for c in starts if c not in heavy]
    order, per = [], len(plain) // len(heavy)
    for i, c in enumerate(heavy):
        order.append(c)
        order.extend(plain[i * per:(i + 1) * per])
    order.extend(plain[len(heavy) * per:])
    assert sorted(order) == starts
    return [(c, IN_COLS) for c in order]


def _in_proj_kernel(x_ref, g_ref, w_ref, wab_ref, dncw_ref, lrucw_ref, lrucb_ref,
                    u_ref, ab_ref, abt_ref, dn_tail, lru_tail, perm_scr, *, tiles_per_seq):
    @pl.when(pl.program_id(0) % tiles_per_seq == 0)
    def _():
        dn_tail[...] = jnp.zeros(dn_tail.shape, F32)
        lru_tail[...] = jnp.zeros(lru_tail.shape, F32)

    x = x_ref[...]
    hb = (x * lax.rsqrt(jnp.mean(x * x, axis=-1, keepdims=True) + EPS) * g_ref[...]).astype(BF16)

    n_conv = 0
    for c0, width in _in_proj_chunk_order():
        p = jnp.dot(hb, w_ref[:, c0:c0 + width], preferred_element_type=F32)
        is_dn = c0 < OFF_DN + DN_QKV
        is_lru = OFF_LRU <= c0 < OFF_LRU + LRU_WIDTH
        if is_dn or is_lru:
            ps = _rows_to_slabs(perm_scr.at[2 * n_conv], p)
            if is_dn:
                cols = slice(c0 - OFF_DN, c0 - OFF_DN + IN_COLS)
                ps = _silu(_causal_conv_slabs(dn_tail, dncw_ref, cols, ps, DN_CONV))
                if c0 < OFF_DN + 2 * DN_QK:
                    scale = DN_DK ** -0.5 if c0 < OFF_DN + DN_QK else 1.0
                    heads = [ps[:, h0:h0 + DN_DK] for h0 in range(0, IN_COLS, DN_DK)]
                    ps = jnp.concatenate(
                        [t * (lax.rsqrt(jnp.sum(t * t, axis=-1, keepdims=True) + EPS) * scale) for t in heads],
                        axis=1)
            else:
                cols = slice(c0 - OFF_LRU, c0 - OFF_LRU + IN_COLS)
                ps = _causal_conv_slabs(lru_tail, lrucw_ref, cols, ps, LRU_CONV) + lrucb_ref[:, cols]
            p = _slabs_to_rows(perm_scr.at[2 * n_conv + 1], ps)
            n_conv += 1
        elif OFF_RET + 2 * RET_QK + RET_VW <= c0 < OFF_RET + RET_W or c0 >= OFF_Z:
            p = _silu(p)
        elif OFF_LRU + LRU_WIDTH <= c0 < OFF_LRU + LRU_W:
            p = _gelu_tanh(p)
        u_ref[:, c0:c0 + width] = p.astype(BF16)
    ab = jnp.dot(hb, wab_ref[...], preferred_element_type=F32)
    ab_ref[...] = ab
    abt_ref[...] = ab.T[0:AB_ROWS, :]


def _in_proj(x2, g, w_main, w_ab, dn_conv_w, lru_conv_w, lru_conv_b, layer, seq):
    m = x2.shape[0]
    tm = min(TM_IN, seq)
    return pl.pallas_call(
        functools.partial(_in_proj_kernel, tiles_per_seq=seq // tm),
        grid=(m // tm,),
        in_specs=[
            pl.BlockSpec((tm, D_MODEL), lambda i: (i, 0)),
            pl.BlockSpec((1, D_MODEL), lambda i: (0, 0)),
            pl.BlockSpec((None, D_MODEL, N_MAIN), lambda i: (layer, 0, 0), pipeline_mode=pl.Buffered(1)),
            pl.BlockSpec((None, D_MODEL, LANES), lambda i: (layer, 0, 0)),
            pl.BlockSpec((DN_CONV, DN_QKV), lambda i: (0, 0)),
            pl.BlockSpec((LRU_CONV, LRU_WIDTH), lambda i: (0, 0)),
            pl.BlockSpec((1, LRU_WIDTH), lambda i: (0, 0)),
        ],
        out_specs=[
            pl.BlockSpec((tm, N_MAIN), lambda i: (i, 0)),
            pl.BlockSpec((tm, LANES), lambda i: (i, 0)),
            pl.BlockSpec((AB_ROWS, tm), lambda i: (0, i)),
        ],
        out_shape=[
            jax.ShapeDtypeStruct((m, N_MAIN), BF16),
            jax.ShapeDtypeStruct((m, LANES), F32),
            jax.ShapeDtypeStruct((AB_ROWS, m), F32),
        ],
        scratch_shapes=[
            pltpu.VMEM((SUBLANES, DN_QKV), F32),
            pltpu.VMEM((SUBLANES, LRU_WIDTH), F32),
            pltpu.VMEM((2 * (DN_QKV + LRU_WIDTH) // IN_COLS, IN_COLS // LANES, tm, LANES), F32),
        ],
        compiler_params=_params(("arbitrary",)),
        name="in_proj",
    )(x2, g, w_main, w_ab, dn_conv_w, lru_conv_w, lru_conv_b)


def _second_rows(x, half):
    return jnp.concatenate([x[r:r + half, :] for r in range(half, x.shape[0], 2 * half)], axis=0)


def _set_second_rows(x, rows, half):
    parts = []
    for j, r in enumerate(range(0, x.shape[0], 2 * half)):
        parts += [x[r:r + half, :], rows[j * half:(j + 1) * half, :]]
    return jnp.concatenate(parts, axis=0)


def _dn_kernel(qkv_ref, z_ref, ab_ref, abt_ref, hpr_ref, hpc_ref, normw_ref, o_ref, s_scr, *, tt):
    a_in = qkv_ref[0]
    a = a_in.astype(F32)

    ab = ab_ref[...]
    abt = abt_ref[...]
    g_cols = -jnp.exp(hpr_ref[0:1, :]) * _softplus(ab + hpr_ref[1:2, :])
    beta_cols = _sigmoid(ab)
    g_rows = -jnp.exp(hpc_ref[:, 0:1]) * _softplus(abt + hpc_ref[:, 1:2])

    ri = _iota2((DN_BLK, DN_BLK), 0)
    ci = _iota2((DN_BLK, DN_BLK), 1)
    xr = ri ^ ci
    lower = ri > ci
    causal = (xr < CHUNK) & (ri >= ci)
    eye = jnp.where(ri == ci, 1.0, 0.0).astype(F32)

    gc_cols, gc_rows = g_cols, g_rows
    t_col = _iota2(g_cols.shape, 0) % CHUNK
    t_row = _iota2(g_rows.shape, 1) % CHUNK
    s = 1
    while s < CHUNK:
        gc_cols = gc_cols + jnp.where(t_col >= s, pltpu.roll(gc_cols, s, 0), 0.0)
        gc_rows = gc_rows + jnp.where(t_row >= s, pltpu.roll(gc_rows, s, 1), 0.0)
        s *= 2
    normw = normw_ref[...]
    z = z_ref[0].astype(F32)

    heads = range(DN_HEADS)
    q, k, v, gcc, beta, eg, k_beta = [], [], [], [], [], [], []
    for h in heads:
        q.append(a[:, h * DN_DK:(h + 1) * DN_DK])
        k.append(a[:, DN_QK + h * DN_DK:DN_QK + (h + 1) * DN_DK])
        v.append(a[:, 2 * DN_QK + h * DN_DV:2 * DN_QK + (h + 1) * DN_DV])
        gcc.append(gc_cols[:, h:h + 1])
        beta.append(beta_cols[:, DN_HEADS + h:DN_HEADS + h + 1])
        eg.append(jnp.exp(gcc[h]))
        k_beta.append(k[h] * beta[h])
    rhs = [jnp.concatenate([v[h] * beta[h], k_beta[h] * eg[h]], axis=1) for h in heads]
    q_dec = [q[h] * eg[h] for h in heads]

    probs = [(h, b) for h in heads for b in range(tt // DN_BLK)]
    rows_of = lambda b: slice(b * DN_BLK, (b + 1) * DN_BLK)
    decay = [jnp.where(causal, jnp.exp(gcc[h][rows_of(b)] - gc_rows[h:h + 1, rows_of(b)]), 0.0)
             for h, b in probs]
    k_op = lambda h, b: a_in[rows_of(b), DN_QK + h * DN_DK:DN_QK + (h + 1) * DN_DK]
    q_op = lambda h, b: a_in[rows_of(b), h * DN_DK:(h + 1) * DN_DK]
    low = [jnp.where(lower, _dot_nt(k_beta[h][rows_of(b)], k_op(h, b)) * decay[i], 0.0)
           for i, (h, b) in enumerate(probs)]
    attn = [_dot_nt(q_op(h, b), k_op(h, b)) * decay[i] for i, (h, b) in enumerate(probs)]
    n = range(len(probs))
    d4 = [jnp.where(xr < 4, low[i], 0.0) for i in n]
    d4sq = [_dot(d4[i], d4[i]) for i in n]
    t_inv = [(eye - d4[i]) + _dot(eye - d4[i], d4sq[i]) for i in n]
    for lv in range(2, 6):
        half = 1 << lv
        off = [jnp.where((xr >> lv) == 1, low[i], 0.0) for i in n]
        if half < SUBLANES:
            off_t = [_dot(off[i], t_inv[i]) for i in n]
            t_inv = [t_inv[i] - _dot(t_inv[i], off_t[i]) for i in n]
        else:
            off_t = [_dot(_second_rows(off[i], half), t_inv[i]) for i in n]
            t_low = [_second_rows(t_inv[i], half) for i in n]
            upd = [t_low[i] - _dot(t_low[i], _set_second_rows(jnp.zeros_like(t_inv[i]), off_t[i], half))
                   for i in n]
            t_inv = [_set_second_rows(t_inv[i], upd[i], half) for i in n]
    uw = {hb: _dot(t_inv[i], rhs[hb[0]][rows_of(hb[1])]) for i, hb in enumerate(probs)}
    attn = {hb: attn[i] for i, hb in enumerate(probs)}

    state = [s_scr[h] for h in heads]
    per_blk = DN_BLK // CHUNK
    for c in range(tt // CHUNK):
        rows = slice(c * CHUNK, (c + 1) * CHUNK)
        b = c // per_blk
        loc = slice((c % per_blk) * CHUNK, (c % per_blk + 1) * CHUNK)
        g_last = [gcc[h][(c + 1) * CHUNK - 1:(c + 1) * CHUNK, :] for h in heads]
        v_new = [uw[h, b][loc, :DN_DV] - _dot(uw[h, b][loc, DN_DV:], state[h]) for h in heads]
        out = [_dot(q_dec[h][rows], state[h]) + _dot(attn[h, b][loc, loc], v_new[h]) for h in heads]
        k_dec = [k[h][rows] * jnp.exp(g_last[h] - gcc[h][rows]) for h in heads]
        state = [state[h] * jnp.exp(g_last[h]) + _dot_tn(k_dec[h], v_new[h]) for h in heads]
        for h in heads:
            o = out[h] * lax.rsqrt(jnp.mean(out[h] * out[h], axis=-1, keepdims=True) + EPS)
            o = o * normw * z[rows, h * DN_DV:(h + 1) * DN_DV]
            o_ref[0, rows, h * DN_DV:(h + 1) * DN_DV] = o.astype(BF16)
    for h in heads:
        s_scr[h] = state[h]


def _ret_kernel(r_ref, cos_ref, sin_ref, dmask_ref, xi_ref, zeta_ref, gch_ref, o_ref, s_scr, *, tt):
    r_in = r_ref[0]
    r = r_in[:, 0:2 * RET_QK].astype(F32)
    cos = cos_ref[0]
    sin_signed = sin_ref[0]
    first_half = (_iota2((tt, LANES), 1) % RET_DK) < (RET_DK // 2)

    def rotary(t):
        swapped = jnp.where(first_half, pltpu.roll(t, LANES - RET_DK // 2, 1),
                            pltpu.roll(t, RET_DK // 2, 1))
        return t * cos + swapped * sin_signed

    q = jnp.concatenate([rotary(r[:, i * LANES:(i + 1) * LANES]) for i in range(RET_QK // LANES)], axis=1)
    k = jnp.concatenate([rotary(r[:, RET_QK + i * LANES:RET_QK + (i + 1) * LANES])
                         for i in range(RET_QK // LANES)], axis=1) * (RET_DK ** -0.5)
    v = r_in[:, 2 * RET_QK:2 * RET_QK + RET_VW]
    gate = r_in[:, 2 * RET_QK + RET_VW:].astype(F32)

    heads = range(RET_HEADS)
    state = [s_scr[h] for h in heads]
    chunk = dmask_ref.shape[1]
    for r0 in range(0, tt, chunk):
        rows = slice(r0, r0 + chunk)
        qh = [q[rows, h * RET_DK:(h + 1) * RET_DK] for h in heads]
        kh = [k[rows, h * RET_DK:(h + 1) * RET_DK] for h in heads]
        vh = [v[rows, h * RET_DV:(h + 1) * RET_DV] for h in heads]
        scores = [_dot_nt(qh[h], kh[h]) * dmask_ref[h] for h in heads]
        cross = [_dot(qh[h], state[h]) * xi_ref[:, h:h + 1] for h in heads]
        inner = [_dot(scores[h], vh[h]) for h in heads]
        state = [state[h] * gch_ref[h:h + 1, :] + _dot_tn(kh[h] * zeta_ref[:, h:h + 1], vh[h]) for h in heads]
        for h in heads:
            out = cross[h] + inner[h]
            mu = jnp.mean(out, axis=-1, keepdims=True)
            cen = out - mu
            var = jnp.mean(cen * cen, axis=-1, keepdims=True)
            out = cen * lax.rsqrt(var + EPS) * gate[rows, h * RET_DV:(h + 1) * RET_DV]
            o_ref[0, rows, h * RET_DV:(h + 1) * RET_DV] = out.astype(BF16)
    for h in heads:
        s_scr[h] = state[h]


def _lru_kernel(l_ref, wax_ref, ba_ref, bx_ref, lam_ref, o_ref, h_scr, perm_scr, *, tt):
    l = l_ref[0]
    xc = _rows_to_slabs(perm_scr.at[0], l[:, 0:LRU_WIDTH].astype(F32))
    rs, gs = [], []
    for n in range(LRU_BLOCKS):
        xb = xc[:, n * LRU_BLOCK:(n + 1) * LRU_BLOCK].astype(BF16)
        both = jnp.dot(xb, wax_ref[n], preferred_element_type=F32)
        rs.append(both[:, :LRU_BLOCK])
        gs.append(both[:, LRU_BLOCK:])
    rgate = _sigmoid(jnp.concatenate(rs, axis=1) + ba_ref[...])
    igate = _sigmoid(jnp.concatenate(gs, axis=1) + bx_ref[...])
    log_a = (-LRU_C) * rgate * _softplus(-lam_ref[...])
    a = jnp.exp(log_a)
    one_minus = -jnp.tanh(log_a) * (a * a + 1.0)
    bv = one_minus * lax.rsqrt(jnp.maximum(one_minus, F32_MIN_NORMAL)) * (igate * xc)

    ns = tt // SUBLANES
    hs = [bv[0:ns, :]]
    ps = [a[0:ns, :]]
    for i in range(1, SUBLANES):
        ai = a[i * ns:(i + 1) * ns, :]
        hs.append(ai * hs[-1] + bv[i * ns:(i + 1) * ns, :])
        ps.append(ai * ps[-1])
    pg, hg = ps[-1], hs[-1]
    row = _iota2((ns, LRU_WIDTH), 0)
    s = 1
    while s < ns:
        keep = row >= s
        p_prev = jnp.where(keep, pltpu.roll(pg, s, 0), 1.0)
        h_prev = jnp.where(keep, pltpu.roll(hg, s, 0), 0.0)
        hg = pg * h_prev + hg
        pg = pg * p_prev
        s *= 2
    h0 = h_scr[...]
    after = pg * h0 + hg
    h_scr[...] = after[ns - 1:ns, :]
    before = jnp.where(row == 0, h0, pltpu.roll(after, 1, 0))
    hseq = _slabs_to_rows(perm_scr.at[1], jnp.concatenate([hs[i] + ps[i] * before for i in range(SUBLANES)],
                                                          axis=0))
    o_ref[0] = (hseq * l[:, LRU_WIDTH:].astype(F32)).astype(BF16)


def _rope_kernel(pos_ref, inv_ref, cos_ref, sin_ref):
    ang = pos_ref[0].astype(F32) * inv_ref[...]
    first_half = (_iota2(ang.shape, 1) % RET_DK) < (RET_DK // 2)
    sin = jnp.sin(ang)
    cos_ref[0] = jnp.cos(ang)
    sin_ref[0] = jnp.where(first_half, -sin, sin)


def _rope_tables(positions, inv_row):
    b, t = positions.shape
    tt = min(TT_ROPE, t)
    spec = pl.BlockSpec((1, tt, LANES), lambda i, j: (i, j, 0))
    shape = jax.ShapeDtypeStruct((b, t, LANES), F32)
    pos3 = jnp.broadcast_to(positions[:, :, None], (b, t, LANES))
    return pl.pallas_call(
        _rope_kernel,
        grid=(b, t // tt),
        in_specs=[spec, pl.BlockSpec((1, LANES), lambda i, j: (0, 0))],
        out_specs=[spec, spec],
        out_shape=[shape, shape],
        compiler_params=_params(("parallel", "parallel")),
        name="rope_tables",
    )(pos3, inv_row)


N_DN_IN, N_RET_IN, N_LRU_IN = 7, 7, 5


def _mixer_kernel(*refs, tt):
    dn_in = refs[:N_DN_IN]
    ret_in = refs[N_DN_IN:N_DN_IN + N_RET_IN]
    lru_in = refs[N_DN_IN + N_RET_IN:N_DN_IN + N_RET_IN + N_LRU_IN]
    ya_ref, yb_ref, yc_ref, dn_state, ret_state, lru_state, lru_perm = refs[N_DN_IN + N_RET_IN + N_LRU_IN:]

    @pl.when(pl.program_id(1) == 0)
    def _():
        for scr in (dn_state, ret_state, lru_state):
            scr[...] = jnp.zeros(scr.shape, F32)

    _ret_kernel(*ret_in, yb_ref, ret_state, tt=tt)
    _dn_kernel(*dn_in, ya_ref, dn_state, tt=tt)
    _lru_kernel(*lru_in, yc_ref, lru_state, lru_perm, tt=tt)


def _mixer(u3, ab, abt, rope_cos, rope_sin, ret_tables, dn_p, lru_p):
    b, t, _ = u3.shape
    tt = min(TT_MIX, t)
    nt = t // tt
    full2 = lambda i, j: (0, 0)
    full3 = lambda i, j: (0, 0, 0)
    rc = ret_tables[0].shape[1]
    dn_specs = [
        pl.BlockSpec((1, tt, DN_QKV), lambda i, j: (i, j, OFF_DN // DN_QKV)),
        pl.BlockSpec((1, tt, DN_VW), lambda i, j: (i, j, OFF_Z // DN_VW)),
        pl.BlockSpec((tt, LANES), lambda i, j: (i * nt + j, 0)),
        pl.BlockSpec((AB_ROWS, tt), lambda i, j: (0, i * nt + j)),
        pl.BlockSpec((SUBLANES, LANES), full2),
        pl.BlockSpec((AB_ROWS, LANES), full2),
        pl.BlockSpec((1, DN_DV), full2),
    ]
    ret_specs = [
        pl.BlockSpec((1, tt, RET_W), lambda i, j: (i, j, OFF_RET // RET_W)),
        pl.BlockSpec((1, tt, LANES), lambda i, j: (i, j, 0)),
        pl.BlockSpec((1, tt, LANES), lambda i, j: (i, j, 0)),
        pl.BlockSpec((RET_HEADS, rc, rc), full3),
        pl.BlockSpec((rc, LANES), full2),
        pl.BlockSpec((rc, LANES), full2),
        pl.BlockSpec((SUBLANES, LANES), full2),
    ]
    lru_specs = [
        pl.BlockSpec((1, tt, LRU_W), lambda i, j: (i, j, OFF_LRU // LRU_W)),
        pl.BlockSpec((LRU_BLOCKS, LRU_BLOCK, 2 * LRU_BLOCK), full3),
        pl.BlockSpec((1, LRU_WIDTH), full2),
        pl.BlockSpec((1, LRU_WIDTH), full2),
        pl.BlockSpec((1, LRU_WIDTH), full2),
    ]
    assert (len(dn_specs), len(ret_specs), len(lru_specs)) == (N_DN_IN, N_RET_IN, N_LRU_IN)
    out_spec = pl.BlockSpec((1, tt, BRANCH_WIDTH), lambda i, j: (i, j, 0))
    out_shape = jax.ShapeDtypeStruct((b, t, BRANCH_WIDTH), BF16)
    return pl.pallas_call(
        functools.partial(_mixer_kernel, tt=tt),
        grid=(b, nt),
        in_specs=dn_specs + ret_specs + lru_specs,
        out_specs=[out_spec] * N_BRANCH,
        out_shape=[out_shape] * N_BRANCH,
        scratch_shapes=[
            pltpu.VMEM((DN_HEADS, DN_DK, DN_DV), F32),
            pltpu.VMEM((RET_HEADS, RET_DK, RET_DV), F32),
            pltpu.VMEM((1, LRU_WIDTH), F32),
            pltpu.VMEM((2, LRU_WIDTH // LANES, tt, LANES), F32),
        ],
        compiler_params=_params(("parallel", "arbitrary")),
        name="mixer",
    )(u3, u3, ab, abt, *dn_p, u3, rope_cos, rope_sin, *ret_tables, u3, *lru_p)


def _ffn_kernel(ya_ref, yb_ref, yc_ref, g0_ref, g1_ref, g2_ref, wb_ref, wo_ref,
                x_ref, g_ref, wup_ref, cw_ref, cb_ref, wdn_ref, fg_ref, o_ref,
                tail_scr, act_scr, perm_scr, *, tt, final):
    @pl.when(pl.program_id(1) == 0)
    def _():
        tail_scr[...] = jnp.zeros(tail_scr.shape, F32)

    merged = None
    for y_ref, gate_ref, n in ((ya_ref, g0_ref, 0), (yb_ref, g1_ref, 1), (yc_ref, g2_ref, 2)):
        term = _sigmoid(gate_ref[0].astype(F32)) * jnp.dot(y_ref[0], wb_ref[n], preferred_element_type=F32)
        merged = term if merged is None else merged + term
    x_mid = x_ref[0] + jnp.dot(merged.astype(BF16), wo_ref[...], preferred_element_type=F32)

    x = _rows_to_slabs(perm_scr, x_mid)
    hb = (x * lax.rsqrt(jnp.mean(x * x, axis=-1, keepdims=True) + EPS) * g_ref[...]).astype(BF16)

    def conv(pre, col0):
        cols = slice(col0, col0 + FFN_COLS)
        return _causal_conv_slabs(tail_scr, cw_ref, cols, pre, FFN_CONV) + cb_ref[:, cols]

    for c in range(D_FF // FFN_COLS):
        c0 = c * FFN_COLS
        gate = conv(jnp.dot(hb, wup_ref[:, c0:c0 + FFN_COLS], preferred_element_type=F32), c0)
        val = conv(jnp.dot(hb, wup_ref[:, D_FF + c0:D_FF + c0 + FFN_COLS], preferred_element_type=F32),
                   D_FF + c0)
        act_scr[:, c0:c0 + FFN_COLS] = (_silu(gate) * val).astype(BF16)

    y = x + jnp.dot(act_scr[...], wdn_ref[...], preferred_element_type=F32)
    if final:
        y = y * lax.rsqrt(jnp.mean(y * y, axis=-1, keepdims=True) + EPS) * fg_ref[...]
    o_ref[0] = _slabs_to_rows(perm_scr, y)


def _merge_ffn(ys, u3, x3, wb, wo, g, w_up, conv_w, conv_b, w_down, final_g, layer, final):
    b, t, _ = x3.shape
    tt = min(TT_FFN, t)
    full2 = lambda i, j: (0, 0)
    of_layer = lambda i, j: (layer, 0, 0)
    gate_blk = OFF_GATE // D_MODEL
    yspec = pl.BlockSpec((1, tt, BRANCH_WIDTH), lambda i, j: (i, j, 0))
    return pl.pallas_call(
        functools.partial(_ffn_kernel, tt=tt, final=final),
        grid=(b, t // tt),
        in_specs=[
            yspec, yspec, yspec,
            pl.BlockSpec((1, tt, D_MODEL), lambda i, j: (i, j, gate_blk)),
            pl.BlockSpec((1, tt, D_MODEL), lambda i, j: (i, j, gate_blk + 1)),
            pl.BlockSpec((1, tt, D_MODEL), lambda i, j: (i, j, gate_blk + 2)),
            pl.BlockSpec((None, N_BRANCH, BRANCH_WIDTH, D_MODEL), lambda i, j: (layer, 0, 0, 0),
                         pipeline_mode=pl.Buffered(1)),
            pl.BlockSpec((None, D_MODEL, D_MODEL), of_layer, pipeline_mode=pl.Buffered(1)),
            pl.BlockSpec((1, tt, D_MODEL), lambda i, j: (i, j, 0)),
            pl.BlockSpec((1, D_MODEL), full2),
            pl.BlockSpec((None, D_MODEL, 2 * D_FF), of_layer, pipeline_mode=pl.Buffered(1)),
            pl.BlockSpec((FFN_CONV, 2 * D_FF), full2),
            pl.BlockSpec((1, 2 * D_FF), full2),
            pl.BlockSpec((None, D_FF, D_MODEL), of_layer, pipeline_mode=pl.Buffered(1)),
            pl.BlockSpec((1, D_MODEL), full2),
        ],
        out_specs=pl.BlockSpec((1, tt, D_MODEL), lambda i, j: (i, j, 0)),
        out_shape=jax.ShapeDtypeStruct((b, t, D_MODEL), F32),
        scratch_shapes=[
            pltpu.VMEM((SUBLANES, 2 * D_FF), F32),
            pltpu.VMEM((tt, D_FF), BF16),
            pltpu.VMEM((D_MODEL // LANES, tt, LANES), F32),
        ],
        compiler_params=_params(("parallel", "arbitrary")),
        name="merge_ffn",
    )(*ys, u3, u3, u3, wb, wo, x3, g, w_up, conv_w, conv_b, w_down, final_g)


def _retention_tables(chunk):
    log_gamma = np.log(1.0 - 2.0 ** (-5.0 - np.arange(RET_HEADS, dtype=np.float64)))
    idx = np.arange(chunk, dtype=np.float64)
    dist = idx[:, None] - idx[None, :]
    causal = dist >= 0
    dmask = np.where(causal, np.exp(np.where(causal, dist, 0.0) * log_gamma[:, None, None]), 0.0)
    xi = np.zeros((chunk, LANES), np.float64)
    zeta = np.zeros((chunk, LANES), np.float64)
    xi[:, :RET_HEADS] = np.exp((idx[:, None] + 1.0) * log_gamma[None, :])
    zeta[:, :RET_HEADS] = np.exp((chunk - 1.0 - idx[:, None]) * log_gamma[None, :])
    gch = np.zeros((SUBLANES, LANES), np.float64)
    gch[:RET_HEADS, :] = np.exp(chunk * log_gamma)[:, None]
    return tuple(jnp.asarray(t, F32) for t in (dmask, xi, zeta, gch))


def _rotary_inv_row():
    half = RET_DK // 2
    inv = (ROPE_BASE ** (-np.arange(half, dtype=np.float32) / half)).astype(np.float32)
    return jnp.asarray(np.tile(inv, LANES // half)[None, :], F32)


def _split_w_in(w_in):
    sizes = (DN_QKV, DN_HEADS, DN_HEADS, DN_VW, RET_QK, RET_QK, RET_VW, RET_VW, LRU_WIDTH, LRU_WIDTH,
             N_BRANCH * D_MODEL)
    offs = np.concatenate([[0], np.cumsum(sizes)])
    w_in = w_in.astype(BF16)
    part = lambda n: w_in[..., int(offs[n]):int(offs[n + 1])]
    main = jnp.concatenate([part(0), part(4), part(5), part(6), part(7), part(8), part(9), part(10), part(3)],
                           axis=-1)
    ab = jnp.concatenate([part(1), part(2)], axis=-1)
    w_ab = jnp.pad(ab, ((0, 0), (0, 0), (0, LANES - 2 * DN_HEADS)))
    return main, w_ab


def kernel(x, positions, attn_norm, w_in, dn_conv_w, dn_a_log, dn_dt_bias, dn_norm_w, lru_conv_w, lru_conv_b,
           lru_wa, lru_ba, lru_wx, lru_bx, lru_lambda, w_branch, w_out, ffn_norm, w_up, ffn_conv_w, ffn_conv_b,
           w_down, final_norm):
    bsz, seq, _ = x.shape
    m = bsz * seq
    depth = w_in.shape[0]
    ret_tables = _retention_tables(min(RET_CHUNK, seq))
    rope_cos, rope_sin = _rope_tables(positions, _rotary_inv_row())
    w_main, w_ab = _split_w_in(w_in)
    hp = jnp.stack([dn_a_log, dn_dt_bias], axis=1).astype(F32)
    hpr = jnp.pad(hp, ((0, 0), (0, SUBLANES - 2), (0, LANES - DN_HEADS)))
    hpc = jnp.pad(jnp.swapaxes(hp, 1, 2), ((0, 0), (0, AB_ROWS - DN_HEADS), (0, LANES - 2)))
    wb = w_branch.astype(BF16)
    wo = w_out.astype(BF16)
    wup = w_up.astype(BF16)
    wdn = w_down.astype(BF16)
    wax = jnp.concatenate([lru_wa, lru_wx], axis=-1).astype(BF16)

    x2 = x.reshape(m, D_MODEL)
    for l in range(depth):
        u2, ab, abt = _in_proj(x2, attn_norm[l][None, :], w_main, w_ab, dn_conv_w[l], lru_conv_w[l],
                               lru_conv_b[l][None, :], l, seq)
        dn_p = (hpr[l], hpc[l], dn_norm_w[l][None, :])
        lru_p = (wax[l], lru_ba[l].reshape(1, LRU_WIDTH), lru_bx[l].reshape(1, LRU_WIDTH),
                 lru_lambda[l][None, :])
        u3 = u2.reshape(bsz, seq, N_MAIN)
        ys = _mixer(u3, ab, abt, rope_cos, rope_sin, ret_tables, dn_p, lru_p)
        x3 = _merge_ffn(ys, u3, x2.reshape(bsz, seq, D_MODEL), wb, wo, ffn_norm[l][None, :], wup, ffn_conv_w[l],
                        ffn_conv_b[l][None, :], wdn, final_norm[None, :], l, final=(l == depth - 1))
        x2 = x3.reshape(m, D_MODEL)
    return x2.reshape(bsz, seq, D_MODEL)
```

```python
import functools
import math

import numpy as np
import jax
import jax.numpy as jnp
from jax import lax
from jax.experimental import pallas as pl
from jax.experimental.pallas import tpu as pltpu

F32 = jnp.float32
BF16 = jnp.bfloat16
F32_MIN_NORMAL = float(np.finfo(np.float32).tiny)

D_MODEL = 1024
CHUNK = 64
EPS = 1e-6
N_BRANCH = 3
BRANCH_WIDTH = 512

DN_HEADS = 4
DN_DK = 128
DN_DV = 128
DN_CONV = 4
DN_QK = DN_HEADS * DN_DK
DN_VW = DN_HEADS * DN_DV
DN_QKV = 2 * DN_QK + DN_VW

RET_HEADS = 4
RET_DK = 64
RET_DV = 128
RET_QK = RET_HEADS * RET_DK
RET_VW = RET_HEADS * RET_DV
ROPE_BASE = 10000.0

LRU_WIDTH = 512
LRU_BLOCKS = 4
LRU_BLOCK = LRU_WIDTH // LRU_BLOCKS
LRU_CONV = 4
LRU_C = 8.0

D_FF = 2816
FFN_CONV = 3

SUBLANES = 8
LANES = 128
V7X_VMEM_BYTES = 64 * 1024 * 1024
VMEM_LIMIT_BYTES = V7X_VMEM_BYTES * 7 // 8

RET_W = 2 * RET_QK + 2 * RET_VW
LRU_W = 2 * LRU_WIDTH
OFF_DN = 0
OFF_RET = OFF_DN + DN_QKV
OFF_LRU = OFF_RET + RET_W
OFF_GATE = OFF_LRU + LRU_W
OFF_Z = OFF_GATE + N_BRANCH * D_MODEL
N_MAIN = OFF_Z + DN_VW
AB_ROWS = 16

TM_IN = 512
IN_COLS = 256
TT_MIX = 512
RET_CHUNK = 256
DN_BLK = 2 * CHUNK
TT_FFN = 512
TT_ROPE = 2048
FFN_COLS = 256


def _params(sem):
    return pltpu.CompilerParams(dimension_semantics=sem, vmem_limit_bytes=VMEM_LIMIT_BYTES)


def _sigmoid(x):
    return 1.0 / (1.0 + jnp.exp(-x))


def _silu(x):
    return x * _sigmoid(x)


def _gelu_tanh(x):
    return 0.5 * x * (1.0 + jnp.tanh(math.sqrt(2.0 / math.pi) * (x + 0.044715 * (x * x * x))))


def _softplus(x):
    return jnp.maximum(x, 0.0) + jnp.log1p(jnp.exp(-jnp.abs(x)))


def _dot(a, b):
    return jnp.dot(a.astype(BF16), b.astype(BF16), preferred_element_type=F32)


def _dot_nt(a, b):
    return lax.dot_general(a.astype(BF16), b.astype(BF16), (((1,), (1,)), ((), ())),
                           preferred_element_type=F32)


def _dot_tn(a, b):
    return lax.dot_general(a.astype(BF16), b.astype(BF16), (((0,), (0,)), ((), ())),
                           preferred_element_type=F32)


def _iota2(shape, dim):
    return lax.broadcasted_iota(jnp.int32, shape, dim)


def _rows_to_slabs(scr, x):
    tt = x.shape[0]
    ns = tt // SUBLANES
    panels = x.shape[1] // LANES
    for p in range(panels):
        scr[p] = x[:, p * LANES:(p + 1) * LANES]
    return jnp.concatenate(
        [jnp.concatenate([scr[p, pl.ds(i, ns, stride=SUBLANES), :] for p in range(panels)], axis=1)
         for i in range(SUBLANES)], axis=0)


def _slabs_to_rows(scr, y):
    tt = y.shape[0]
    ns = tt // SUBLANES
    panels = y.shape[1] // LANES
    for p in range(panels):
        for i in range(SUBLANES):
            scr[p, pl.ds(i, ns, stride=SUBLANES), :] = y[i * ns:(i + 1) * ns, p * LANES:(p + 1) * LANES]
    return jnp.concatenate([scr[p] for p in range(panels)], axis=1)


def _causal_conv_slabs(tail_scr, w_ref, cols, xs, width):
    ns = xs.shape[0] // SUBLANES
    group0 = _iota2((ns, xs.shape[1]), 0) == 0
    slabs = [xs[i * ns:(i + 1) * ns, :] for i in range(SUBLANES)]
    back = [jnp.where(group0, tail_scr[i:i + 1, cols], pltpu.roll(slabs[i], 1, 0))
            for i in range(SUBLANES - (width - 1), SUBLANES)]
    for i in range(SUBLANES - (width - 1), SUBLANES):
        tail_scr[i:i + 1, cols] = slabs[i][ns - 1:ns, :]
    ext = back + slabs
    out = []
    for i in range(SUBLANES):
        y = ext[i] * w_ref[0:1, cols]
        for j in range(1, width):
            y = y + ext[i + j] * w_ref[j:j + 1, cols]
        out.append(y)
    return jnp.concatenate(out, axis=0)


def _in_proj_chunk_order():
    starts = list(range(0, N_MAIN, IN_COLS))
    heavy = [c for c in starts if c < OFF_DN + DN_QKV or OFF_LRU <= c < OFF_LRU + LRU_WIDTH]
    plain = [c for c in starts if c not in heavy]
    order, per = [], len(plain) // len(heavy)
    for i, c in enumerate(heavy):
        order.append(c)
        order.extend(plain[i * per:(i + 1) * per])
    order.extend(plain[len(heavy) * per:])
    assert sorted(order) == starts
    return [(c, IN_COLS) for c in order]


def _in_proj_kernel(x_ref, g_ref, w_ref, wab_ref, dncw_ref, lrucw_ref, lrucb_ref,
                    u_ref, ab_ref, abt_ref, dn_tail, lru_tail, perm_scr, *, tiles_per_seq):
    @pl.when(pl.program_id(0) % tiles_per_seq == 0)
    def _():
        dn_tail[...] = jnp.zeros(dn_tail.shape, F32)
        lru_tail[...] = jnp.zeros(lru_tail.shape, F32)

    x = x_ref[...]
    hb = (x * lax.rsqrt(jnp.mean(x * x, axis=-1, keepdims=True) + EPS) * g_ref[...]).astype(BF16)

    n_conv = 0
    for c0, width in _in_proj_chunk_order():
        p = jnp.dot(hb, w_ref[:, c0:c0 + width], preferred_element_type=F32)
        is_dn = c0 < OFF_DN + DN_QKV
        is_lru = OFF_LRU <= c0 < OFF_LRU + LRU_WIDTH
        if is_dn or is_lru:
            ps = _rows_to_slabs(perm_scr.at[2 * n_conv], p)
            if is_dn:
                cols = slice(c0 - OFF_DN, c0 - OFF_DN + IN_COLS)
                ps = _silu(_causal_conv_slabs(dn_tail, dncw_ref, cols, ps, DN_CONV))
                if c0 < OFF_DN + 2 * DN_QK:
                    scale = DN_DK ** -0.5 if c0 < OFF_DN + DN_QK else 1.0
                    heads = [ps[:, h0:h0 + DN_DK] for h0 in range(0, IN_COLS, DN_DK)]
                    ps = jnp.concatenate(
                        [t * (lax.rsqrt(jnp.sum(t * t, axis=-1, keepdims=True) + EPS) * scale) for t in heads],
                        axis=1)
            else:
                cols = slice(c0 - OFF_LRU, c0 - OFF_LRU + IN_COLS)
                ps = _causal_conv_slabs(lru_tail, lrucw_ref, cols, ps, LRU_CONV) + lrucb_ref[:, cols]
            p = _slabs_to_rows(perm_scr.at[2 * n_conv + 1], ps)
            n_conv += 1
        elif OFF_RET + 2 * RET_QK + RET_VW <= c0 < OFF_RET + RET_W or c0 >= OFF_Z:
            p = _silu(p)
        elif OFF_LRU + LRU_WIDTH <= c0 < OFF_LRU + LRU_W:
            p = _gelu_tanh(p)
        u_ref[:, c0:c0 + width] = p.astype(BF16)
    ab = jnp.dot(hb, wab_ref[...], preferred_element_type=F32)
    ab_ref[...] = ab
    abt_ref[...] = ab.T[0:AB_ROWS, :]


def _in_proj(x2, g, w_main, w_ab, dn_conv_w, lru_conv_w, lru_conv_b, layer, seq):
    m = x2.shape[0]
    tm = min(TM_IN, seq)
    return pl.pallas_call(
        functools.partial(_in_proj_kernel, tiles_per_seq=seq // tm),
        grid=(m // tm,),
        in_specs=[
            pl.BlockSpec((tm, D_MODEL), lambda i: (i, 0)),
            pl.BlockSpec((1, D_MODEL), lambda i: (0, 0)),
            pl.BlockSpec((None, D_MODEL, N_MAIN), lambda i: (layer, 0, 0), pipeline_mode=pl.Buffered(1)),
            pl.BlockSpec((None, D_MODEL, LANES), lambda i: (layer, 0, 0)),
            pl.BlockSpec((DN_CONV, DN_QKV), lambda i: (0, 0)),
            pl.BlockSpec((LRU_CONV, LRU_WIDTH), lambda i: (0, 0)),
            pl.BlockSpec((1, LRU_WIDTH), lambda i: (0, 0)),
        ],
        out_specs=[
            pl.BlockSpec((tm, N_MAIN), lambda i: (i, 0)),
            pl.BlockSpec((tm, LANES), lambda i: (i, 0)),
            pl.BlockSpec((AB_ROWS, tm), lambda i: (0, i)),
        ],
        out_shape=[
            jax.ShapeDtypeStruct((m, N_MAIN), BF16),
            jax.ShapeDtypeStruct((m, LANES), F32),
            jax.ShapeDtypeStruct((AB_ROWS, m), F32),
        ],
        scratch_shapes=[
            pltpu.VMEM((SUBLANES, DN_QKV), F32),
            pltpu.VMEM((SUBLANES, LRU_WIDTH), F32),
            pltpu.VMEM((2 * (DN_QKV + LRU_WIDTH) // IN_COLS, IN_COLS // LANES, tm, LANES), F32),
        ],
        compiler_params=_params(("arbitrary",)),
        name="in_proj",
    )(x2, g, w_main, w_ab, dn_conv_w, lru_conv_w, lru_conv_b)


def _second_rows(x, half):
    return jnp.concatenate([x[r:r + half, :] for r in range(half, x.shape[0], 2 * half)], axis=0)


def _set_second_rows(x, rows, half):
    parts = []
    for j, r in enumerate(range(0, x.shape[0], 2 * half)):
        parts += [x[r:r + half, :], rows[j * half:(j + 1) * half, :]]
    return jnp.concatenate(parts, axis=0)


def _dn_kernel(qkv_ref, z_ref, ab_ref, abt_ref, hpr_ref, hpc_ref, normw_ref, o_ref, s_scr, *, tt):
    a_in = qkv_ref[0]
    a = a_in.astype(F32)

    ab = ab_ref[...]
    abt = abt_ref[...]
    g_cols = -jnp.exp(hpr_ref[0:1, :]) * _softplus(ab + hpr_ref[1:2, :])
    beta_cols = _sigmoid(ab)
    g_rows = -jnp.exp(hpc_ref[:, 0:1]) * _softplus(abt + hpc_ref[:, 1:2])

    ri = _iota2((DN_BLK, DN_BLK), 0)
    ci = _iota2((DN_BLK, DN_BLK), 1)
    xr = ri ^ ci
    lower = ri > ci
    causal = (xr < CHUNK) & (ri >= ci)
    eye = jnp.where(ri == ci, 1.0, 0.0).astype(F32)

    gc_cols, gc_rows = g_cols, g_rows
    t_col = _iota2(g_cols.shape, 0) % CHUNK
    t_row = _iota2(g_rows.shape, 1) % CHUNK
    s = 1
    while s < CHUNK:
        gc_cols = gc_cols + jnp.where(t_col >= s, pltpu.roll(gc_cols, s, 0), 0.0)
        gc_rows = gc_rows + jnp.where(t_row >= s, pltpu.roll(gc_rows, s, 1), 0.0)
        s *= 2
    normw = normw_ref[...]
    z = z_ref[0].astype(F32)

    heads = range(DN_HEADS)
    q, k, v, gcc, beta, eg, k_beta = [], [], [], [], [], [], []
    for h in heads:
        q.append(a[:, h * DN_DK:(h + 1) * DN_DK])
        k.append(a[:, DN_QK + h * DN_DK:DN_QK + (h + 1) * DN_DK])
        v.append(a[:, 2 * DN_QK + h * DN_DV:2 * DN_QK + (h + 1) * DN_DV])
        gcc.append(gc_cols[:, h:h + 1])
        beta.append(beta_cols[:, DN_HEADS + h:DN_HEADS + h + 1])
        eg.append(jnp.exp(gcc[h]))
        k_beta.append(k[h] * beta[h])
    rhs = [jnp.concatenate([v[h] * beta[h], k_beta[h] * eg[h]], axis=1) for h in heads]
    q_dec = [q[h] * eg[h] for h in heads]

    probs = [(h, b) for h in heads for b in range(tt // DN_BLK)]
    rows_of = lambda b: slice(b * DN_BLK, (b + 1) * DN_BLK)
    decay = [jnp.where(causal, jnp.exp(gcc[h][rows_of(b)] - gc_rows[h:h + 1, rows_of(b)]), 0.0)
             for h, b in probs]
    k_op = lambda h, b: a_in[rows_of(b), DN_QK + h * DN_DK:DN_QK + (h + 1) * DN_DK]
    q_op = lambda h, b: a_in[rows_of(b), h * DN_DK:(h + 1) * DN_DK]
    gram = [_dot_nt(jnp.concatenate([k_beta[h][rows_of(b)].astype(BF16), q_op(h, b)], axis=0), k_op(h, b))
            for h, b in probs]
    low = [jnp.where(lower, gram[i][:DN_BLK] * decay[i], 0.0) for i in range(len(probs))]
    attn = [gram[i][DN_BLK:] * decay[i] for i in range(len(probs))]
    n = range(len(probs))
    d4 = [jnp.where(xr < 4, low[i], 0.0) for i in n]
    d4sq = [_dot(d4[i], d4[i]) for i in n]
    t_inv = [(eye - d4[i]) + _dot(eye - d4[i], d4sq[i]) for i in n]
    for lv in range(2, 6):
        half = 1 << lv
        off = [jnp.where((xr >> lv) == 1, low[i], 0.0) for i in n]
        if half < SUBLANES:
            off_t = [_dot(off[i], t_inv[i]) for i in n]
            t_inv = [t_inv[i] - _dot(t_inv[i], off_t[i]) for i in n]
        else:
            off_t = [_dot(_second_rows(off[i], half), t_inv[i]) for i in n]
            t_low = [_second_rows(t_inv[i], half) for i in n]
            upd = [t_low[i] - _dot(t_low[i], _set_second_rows(jnp.zeros_like(t_inv[i]), off_t[i], half))
                   for i in n]
            t_inv = [_set_second_rows(t_inv[i], upd[i], half) for i in n]
    uw = {hb: _dot(t_inv[i], rhs[hb[0]][rows_of(hb[1])]) for i, hb in enumerate(probs)}
    attn = {hb: attn[i] for i, hb in enumerate(probs)}

    state = [s_scr[h] for h in heads]
    per_blk = DN_BLK // CHUNK
    for c in range(tt // CHUNK):
        rows = slice(c * CHUNK, (c + 1) * CHUNK)
        b = c // per_blk
        loc = slice((c % per_blk) * CHUNK, (c % per_blk + 1) * CHUNK)
        g_last = [gcc[h][(c + 1) * CHUNK - 1:(c + 1) * CHUNK, :] for h in heads]
        ws_qs = [_dot(jnp.concatenate([uw[h, b][loc, DN_DV:], q_dec[h][rows]], axis=0), state[h]) for h in heads]
        v_new = [uw[h, b][loc, :DN_DV] - ws_qs[h][:CHUNK] for h in heads]
        out = [ws_qs[h][CHUNK:] + _dot(attn[h, b][loc, loc], v_new[h]) for h in heads]
        k_dec = [k[h][rows] * jnp.exp(g_last[h] - gcc[h][rows]) for h in heads]
        state = [state[h] * jnp.exp(g_last[h]) + _dot_tn(k_dec[h], v_new[h]) for h in heads]
        for h in heads:
            o = out[h] * lax.rsqrt(jnp.mean(out[h] * out[h], axis=-1, keepdims=True) + EPS)
            o = o * normw * z[rows, h * DN_DV:(h + 1) * DN_DV]
            o_ref[0, rows, h * DN_DV:(h + 1) * DN_DV] = o.astype(BF16)
    for h in heads:
        s_scr[h] = state[h]


def _ret_kernel(r_ref, cos_ref, sin_ref, dmask_ref, xi_ref, zeta_ref, gch_ref, o_ref, s_scr, *, tt):
    r_in = r_ref[0]
    r = r_in[:, 0:2 * RET_QK].astype(F32)
    cos = cos_ref[0]
    sin_signed = sin_ref[0]
    first_half = (_iota2((tt, LANES), 1) % RET_DK) < (RET_DK // 2)

    def rotary(t):
        swapped = jnp.where(first_half, pltpu.roll(t, LANES - RET_DK // 2, 1),
                            pltpu.roll(t, RET_DK // 2, 1))
        return t * cos + swapped * sin_signed

    q = jnp.concatenate([rotary(r[:, i * LANES:(i + 1) * LANES]) for i in range(RET_QK // LANES)], axis=1)
    k = jnp.concatenate([rotary(r[:, RET_QK + i * LANES:RET_QK + (i + 1) * LANES])
                         for i in range(RET_QK // LANES)], axis=1) * (RET_DK ** -0.5)
    v = r_in[:, 2 * RET_QK:2 * RET_QK + RET_VW]
    gate = r_in[:, 2 * RET_QK + RET_VW:].astype(F32)

    heads = range(RET_HEADS)
    state = [s_scr[h] for h in heads]
    chunk = dmask_ref.shape[1]
    for r0 in range(0, tt, chunk):
        rows = slice(r0, r0 + chunk)
        qh = [q[rows, h * RET_DK:(h + 1) * RET_DK] for h in heads]
        kh = [k[rows, h * RET_DK:(h + 1) * RET_DK] for h in heads]
        vh = [v[rows, h * RET_DV:(h + 1) * RET_DV] for h in heads]
        scores = [_dot_nt(qh[h], kh[h]) * dmask_ref[h] for h in heads]
        cross = [_dot(qh[h], state[h]) * xi_ref[:, h:h + 1] for h in heads]
        inner = [_dot(scores[h], vh[h]) for h in heads]
        state = [state[h] * gch_ref[h:h + 1, :] + _dot_tn(kh[h] * zeta_ref[:, h:h + 1], vh[h]) for h in heads]
        for h in heads:
            out = cross[h] + inner[h]
            mu = jnp.mean(out, axis=-1, keepdims=True)
            cen = out - mu
            var = jnp.mean(cen * cen, axis=-1, keepdims=True)
            out = cen * lax.rsqrt(var + EPS) * gate[rows, h * RET_DV:(h + 1) * RET_DV]
            o_ref[0, rows, h * RET_DV:(h + 1) * RET_DV] = out.astype(BF16)
    for h in heads:
        s_scr[h] = state[h]


def _lru_kernel(l_ref, wax_ref, ba_ref, bx_ref, lam_ref, o_ref, h_scr, perm_scr, *, tt):
    l = l_ref[0]
    xc = _rows_to_slabs(perm_scr.at[0], l[:, 0:LRU_WIDTH].astype(F32))
    rs, gs = [], []
    for n in range(LRU_BLOCKS):
        xb = xc[:, n * LRU_BLOCK:(n + 1) * LRU_BLOCK].astype(BF16)
        both = jnp.dot(xb, wax_ref[n], preferred_element_type=F32)
        rs.append(both[:, :LRU_BLOCK])
        gs.append(both[:, LRU_BLOCK:])
    rgate = _sigmoid(jnp.concatenate(rs, axis=1) + ba_ref[...])
    igate = _sigmoid(jnp.concatenate(gs, axis=1) + bx_ref[...])
    log_a = (-LRU_C) * rgate * _softplus(-lam_ref[...])
    a = jnp.exp(log_a)
    one_minus = -jnp.tanh(log_a) * (a * a + 1.0)
    bv = one_minus * lax.rsqrt(jnp.maximum(one_minus, F32_MIN_NORMAL)) * (igate * xc)

    ns = tt // SUBLANES
    hs = [bv[0:ns, :]]
    ps = [a[0:ns, :]]
    for i in range(1, SUBLANES):
        ai = a[i * ns:(i + 1) * ns, :]
        hs.append(ai * hs[-1] + bv[i * ns:(i + 1) * ns, :])
        ps.append(ai * ps[-1])
    pg, hg = ps[-1], hs[-1]
    row = _iota2((ns, LRU_WIDTH), 0)
    s = 1
    while s < ns:
        keep = row >= s
        p_prev = jnp.where(keep, pltpu.roll(pg, s, 0), 1.0)
        h_prev = jnp.where(keep, pltpu.roll(hg, s, 0), 0.0)
        hg = pg * h_prev + hg
        pg = pg * p_prev
        s *= 2
    h0 = h_scr[...]
    after = pg * h0 + hg
    h_scr[...] = after[ns - 1:ns, :]
    before = jnp.where(row == 0, h0, pltpu.roll(after, 1, 0))
    hseq = _slabs_to_rows(perm_scr.at[1], jnp.concatenate([hs[i] + ps[i] * before for i in range(SUBLANES)],
                                                          axis=0))
    o_ref[0] = (hseq * l[:, LRU_WIDTH:].astype(F32)).astype(BF16)


def _rope_kernel(pos_ref, inv_ref, cos_ref, sin_ref):
    ang = pos_ref[0].astype(F32) * inv_ref[...]
    first_half = (_iota2(ang.shape, 1) % RET_DK) < (RET_DK // 2)
    sin = jnp.sin(ang)
    cos_ref[0] = jnp.cos(ang)
    sin_ref[0] = jnp.where(first_half, -sin, sin)


def _rope_tables(positions, inv_row):
    b, t = positions.shape
    tt = min(TT_ROPE, t)
    spec = pl.BlockSpec((1, tt, LANES), lambda i, j: (i, j, 0))
    shape = jax.ShapeDtypeStruct((b, t, LANES), F32)
    pos3 = jnp.broadcast_to(positions[:, :, None], (b, t, LANES))
    return pl.pallas_call(
        _rope_kernel,
        grid=(b, t // tt),
        in_specs=[spec, pl.BlockSpec((1, LANES), lambda i, j: (0, 0))],
        out_specs=[spec, spec],
        out_shape=[shape, shape],
        compiler_params=_params(("parallel", "parallel")),
        name="rope_tables",
    )(pos3, inv_row)


N_DN_IN, N_RET_IN, N_LRU_IN = 7, 7, 5


def _mixer_kernel(*refs, tt):
    dn_in = refs[:N_DN_IN]
    ret_in = refs[N_DN_IN:N_DN_IN + N_RET_IN]
    lru_in = refs[N_DN_IN + N_RET_IN:N_DN_IN + N_RET_IN + N_LRU_IN]
    ya_ref, yb_ref, yc_ref, dn_state, ret_state, lru_state, lru_perm = refs[N_DN_IN + N_RET_IN + N_LRU_IN:]

    @pl.when(pl.program_id(1) == 0)
    def _():
        for scr in (dn_state, ret_state, lru_state):
            scr[...] = jnp.zeros(scr.shape, F32)

    _ret_kernel(*ret_in, yb_ref, ret_state, tt=tt)
    _dn_kernel(*dn_in, ya_ref, dn_state, tt=tt)
    _lru_kernel(*lru_in, yc_ref, lru_state, lru_perm, tt=tt)


def _mixer(u3, ab, abt, rope_cos, rope_sin, ret_tables, dn_p, lru_p):
    b, t, _ = u3.shape
    tt = min(TT_MIX, t)
    nt = t // tt
    full2 = lambda i, j: (0, 0)
    full3 = lambda i, j: (0, 0, 0)
    rc = ret_tables[0].shape[1]
    dn_specs = [
        pl.BlockSpec((1, tt, DN_QKV), lambda i, j: (i, j, OFF_DN // DN_QKV)),
        pl.BlockSpec((1, tt, DN_VW), lambda i, j: (i, j, OFF_Z // DN_VW)),
        pl.BlockSpec((tt, LANES), lambda i, j: (i * nt + j, 0)),
        pl.BlockSpec((AB_ROWS, tt), lambda i, j: (0, i * nt + j)),
        pl.BlockSpec((SUBLANES, LANES), full2),
        pl.BlockSpec((AB_ROWS, LANES), full2),
        pl.BlockSpec((1, DN_DV), full2),
    ]
    ret_specs = [
        pl.BlockSpec((1, tt, RET_W), lambda i, j: (i, j, OFF_RET // RET_W)),
        pl.BlockSpec((1, tt, LANES), lambda i, j: (i, j, 0)),
        pl.BlockSpec((1, tt, LANES), lambda i, j: (i, j, 0)),
        pl.BlockSpec((RET_HEADS, rc, rc), full3),
        pl.BlockSpec((rc, LANES), full2),
        pl.BlockSpec((rc, LANES), full2),
        pl.BlockSpec((SUBLANES, LANES), full2),
    ]
    lru_specs = [
        pl.BlockSpec((1, tt, LRU_W), lambda i, j: (i, j, OFF_LRU // LRU_W)),
        pl.BlockSpec((LRU_BLOCKS, LRU_BLOCK, 2 * LRU_BLOCK), full3),
        pl.BlockSpec((1, LRU_WIDTH), full2),
        pl.BlockSpec((1, LRU_WIDTH), full2),
        pl.BlockSpec((1, LRU_WIDTH), full2),
    ]
    assert (len(dn_specs), len(ret_specs), len(lru_specs)) == (N_DN_IN, N_RET_IN, N_LRU_IN)
    out_spec = pl.BlockSpec((1, tt, BRANCH_WIDTH), lambda i, j: (i, j, 0))
    out_shape = jax.ShapeDtypeStruct((b, t, BRANCH_WIDTH), BF16)
    return pl.pallas_call(
        functools.partial(_mixer_kernel, tt=tt),
        grid=(b, nt),
        in_specs=dn_specs + ret_specs + lru_specs,
        out_specs=[out_spec] * N_BRANCH,
        out_shape=[out_shape] * N_BRANCH,
        scratch_shapes=[
            pltpu.VMEM((DN_HEADS, DN_DK, DN_DV), F32),
            pltpu.VMEM((RET_HEADS, RET_DK, RET_DV), F32),
            pltpu.VMEM((1, LRU_WIDTH), F32),
            pltpu.VMEM((2, LRU_WIDTH // LANES, tt, LANES), F32),
        ],
        compiler_params=_params(("parallel", "arbitrary")),
        name="mixer",
    )(u3, u3, ab, abt, *dn_p, u3, rope_cos, rope_sin, *ret_tables, u3, *lru_p)


def _ffn_kernel(ya_ref, yb_ref, yc_ref, g0_ref, g1_ref, g2_ref, wb_ref, wo_ref,
                x_ref, g_ref, wup_ref, cw_ref, cb_ref, wdn_ref, fg_ref, o_ref,
                tail_scr, act_scr, perm_scr, *, tt, final):
    @pl.when(pl.program_id(1) == 0)
    def _():
        tail_scr[...] = jnp.zeros(tail_scr.shape, F32)

    merged = None
    for y_ref, gate_ref, n in ((ya_ref, g0_ref, 0), (yb_ref, g1_ref, 1), (yc_ref, g2_ref, 2)):
        term = _sigmoid(gate_ref[0].astype(F32)) * jnp.dot(y_ref[0], wb_ref[n], preferred_element_type=F32)
        merged = term if merged is None else merged + term
    x_mid = x_ref[0] + jnp.dot(merged.astype(BF16), wo_ref[...], preferred_element_type=F32)

    x = _rows_to_slabs(perm_scr, x_mid)
    hb = (x * lax.rsqrt(jnp.mean(x * x, axis=-1, keepdims=True) + EPS) * g_ref[...]).astype(BF16)

    def conv(pre, col0):
        cols = slice(col0, col0 + FFN_COLS)
        return _causal_conv_slabs(tail_scr, cw_ref, cols, pre, FFN_CONV) + cb_ref[:, cols]

    for c in range(D_FF // FFN_COLS):
        c0 = c * FFN_COLS
        gate = conv(jnp.dot(hb, wup_ref[:, c0:c0 + FFN_COLS], preferred_element_type=F32), c0)
        val = conv(jnp.dot(hb, wup_ref[:, D_FF + c0:D_FF + c0 + FFN_COLS], preferred_element_type=F32),
                   D_FF + c0)
        act_scr[:, c0:c0 + FFN_COLS] = (_silu(gate) * val).astype(BF16)

    y = x + jnp.dot(act_scr[...], wdn_ref[...], preferred_element_type=F32)
    if final:
        y = y * lax.rsqrt(jnp.mean(y * y, axis=-1, keepdims=True) + EPS) * fg_ref[...]
    o_ref[0] = _slabs_to_rows(perm_scr, y)


def _merge_ffn(ys, u3, x3, wb, wo, g, w_up, conv_w, conv_b, w_down, final_g, layer, final):
    b, t, _ = x3.shape
    tt = min(TT_FFN, t)
    full2 = lambda i, j: (0, 0)
    of_layer = lambda i, j: (layer, 0, 0)
    gate_blk = OFF_GATE // D_MODEL
    yspec = pl.BlockSpec((1, tt, BRANCH_WIDTH), lambda i, j: (i, j, 0))
    return pl.pallas_call(
        functools.partial(_ffn_kernel, tt=tt, final=final),
        grid=(b, t // tt),
        in_specs=[
            yspec, yspec, yspec,
            pl.BlockSpec((1, tt, D_MODEL), lambda i, j: (i, j, gate_blk)),
            pl.BlockSpec((1, tt, D_MODEL), lambda i, j: (i, j, gate_blk + 1)),
            pl.BlockSpec((1, tt, D_MODEL), lambda i, j: (i, j, gate_blk + 2)),
            pl.BlockSpec((None, N_BRANCH, BRANCH_WIDTH, D_MODEL), lambda i, j: (layer, 0, 0, 0),
                         pipeline_mode=pl.Buffered(1)),
            pl.BlockSpec((None, D_MODEL, D_MODEL), of_layer, pipeline_mode=pl.Buffered(1)),
            pl.BlockSpec((1, tt, D_MODEL), lambda i, j: (i, j, 0)),
            pl.BlockSpec((1, D_MODEL), full2),
            pl.BlockSpec((None, D_MODEL, 2 * D_FF), of_layer, pipeline_mode=pl.Buffered(1)),
            pl.BlockSpec((FFN_CONV, 2 * D_FF), full2),
            pl.BlockSpec((1, 2 * D_FF), full2),
            pl.BlockSpec((None, D_FF, D_MODEL), of_layer, pipeline_mode=pl.Buffered(1)),
            pl.BlockSpec((1, D_MODEL), full2),
        ],
        out_specs=pl.BlockSpec((1, tt, D_MODEL), lambda i, j: (i, j, 0)),
        out_shape=jax.ShapeDtypeStruct((b, t, D_MODEL), F32),
        scratch_shapes=[
            pltpu.VMEM((SUBLANES, 2 * D_FF), F32),
            pltpu.VMEM((tt, D_FF), BF16),
            pltpu.VMEM((D_MODEL // LANES, tt, LANES), F32),
        ],
        compiler_params=_params(("parallel", "arbitrary")),
        name="merge_ffn",
    )(*ys, u3, u3, u3, wb, wo, x3, g, w_up, conv_w, conv_b, w_down, final_g)


def _retention_tables(chunk):
    log_gamma = np.log(1.0 - 2.0 ** (-5.0 - np.arange(RET_HEADS, dtype=np.float64)))
    idx = np.arange(chunk, dtype=np.float64)
    dist = idx[:, None] - idx[None, :]
    causal = dist >= 0
    dmask = np.where(causal, np.exp(np.where(causal, dist, 0.0) * log_gamma[:, None, None]), 0.0)
    xi = np.zeros((chunk, LANES), np.float64)
    zeta = np.zeros((chunk, LANES), np.float64)
    xi[:, :RET_HEADS] = np.exp((idx[:, None] + 1.0) * log_gamma[None, :])
    zeta[:, :RET_HEADS] = np.exp((chunk - 1.0 - idx[:, None]) * log_gamma[None, :])
    gch = np.zeros((SUBLANES, LANES), np.float64)
    gch[:RET_HEADS, :] = np.exp(chunk * log_gamma)[:, None]
    return tuple(jnp.asarray(t, F32) for t in (dmask, xi, zeta, gch))


def _rotary_inv_row():
    half = RET_DK // 2
    inv = (ROPE_BASE ** (-np.arange(half, dtype=np.float32) / half)).astype(np.float32)
    return jnp.asarray(np.tile(inv, LANES // half)[None, :], F32)


def _split_w_in(w_in):
    sizes = (DN_QKV, DN_HEADS, DN_HEADS, DN_VW, RET_QK, RET_QK, RET_VW, RET_VW, LRU_WIDTH, LRU_WIDTH,
             N_BRANCH * D_MODEL)
    offs = np.concatenate([[0], np.cumsum(sizes)])
    w_in = w_in.astype(BF16)
    part = lambda n: w_in[..., int(offs[n]):int(offs[n + 1])]
    main = jnp.concatenate([part(0), part(4), part(5), part(6), part(7), part(8), part(9), part(10), part(3)],
                           axis=-1)
    ab = jnp.concatenate([part(1), part(2)], axis=-1)
    w_ab = jnp.pad(ab, ((0, 0), (0, 0), (0, LANES - 2 * DN_HEADS)))
    return main, w_ab


def kernel(x, positions, attn_norm, w_in, dn_conv_w, dn_a_log, dn_dt_bias, dn_norm_w, lru_conv_w, lru_conv_b,
           lru_wa, lru_ba, lru_wx, lru_bx, lru_lambda, w_branch, w_out, ffn_norm, w_up, ffn_conv_w, ffn_conv_b,
           w_down, final_norm):
    bsz, seq, _ = x.shape
    m = bsz * seq
    depth = w_in.shape[0]
    ret_tables = _retention_tables(min(RET_CHUNK, seq))
    rope_cos, rope_sin = _rope_tables(positions, _rotary_inv_row())
    w_main, w_ab = _split_w_in(w_in)
    hp = jnp.stack([dn_a_log, dn_dt_bias], axis=1).astype(F32)
    hpr = jnp.pad(hp, ((0, 0), (0, SUBLANES - 2), (0, LANES - DN_HEADS)))
    hpc = jnp.pad(jnp.swapaxes(hp, 1, 2), ((0, 0), (0, AB_ROWS - DN_HEADS), (0, LANES - 2)))
    wb = w_branch.astype(BF16)
    wo = w_out.astype(BF16)
    wup = w_up.astype(BF16)
    wdn = w_down.astype(BF16)
    wax = jnp.concatenate([lru_wa, lru_wx], axis=-1).astype(BF16)

    x2 = x.reshape(m, D_MODEL)
    for l in range(depth):
        u2, ab, abt = _in_proj(x2, attn_norm[l][None, :], w_main, w_ab, dn_conv_w[l], lru_conv_w[l],
                               lru_conv_b[l][None, :], l, seq)
        dn_p = (hpr[l], hpc[l], dn_norm_w[l][None, :])
        lru_p = (wax[l], lru_ba[l].reshape(1, LRU_WIDTH), lru_bx[l].reshape(1, LRU_WIDTH),
                 lru_lambda[l][None, :])
        u3 = u2.reshape(bsz, seq, N_MAIN)
        ys = _mixer(u3, ab, abt, rope_cos, rope_sin, ret_tables, dn_p, lru_p)
        x3 = _merge_ffn(ys, u3, x2.reshape(bsz, seq, D_MODEL), wb, wo, ffn_norm[l][None, :], wup, ffn_conv_w[l],
                        ffn_conv_b[l][None, :], wdn, final_norm[None, :], l, final=(l == depth - 1))
        x2 = x3.reshape(m, D_MODEL)
    return x2.reshape(bsz, seq, D_MODEL)
```
